```python
import math
import jax, jax.numpy as jnp
from jax import lax
import numpy as np

D_MODEL = 1024
BATCH = 1
SEQ = 16384
DEPTH = 2

N_A_LAYERS = DEPTH // 2
N_B_LAYERS = DEPTH - N_A_LAYERS
HEAD_DIM = 64
DIFF_HEADS = D_MODEL // (2 * HEAD_DIM)
MOBA_HEADS = D_MODEL // HEAD_DIM
D_FF = 2816
ROPE_THETA = 10000.0
MOBA_BLOCK = 256
MOBA_TOPK = 3
DIFF_Q_CHUNK = 128
MOBA_Q_CHUNK = 64
FFN_RESIDUAL_SCALE = 0.5
NORM_EPS = 1e-6

kernel_name = "yoco_diffattn_moba_macaron_sandwich"


def rms_norm(x, g):
    xf = x.astype(jnp.float32)
    y = xf * lax.rsqrt(jnp.mean(xf * xf, axis=-1, keepdims=True) + NORM_EPS)
    return (y * g.astype(jnp.float32)).astype(x.dtype)


def rope_tables(seq, dim):
    pos = jnp.arange(seq, dtype=jnp.float32)
    inv_freq = 1.0 / (ROPE_THETA ** (jnp.arange(0, dim, 2, dtype=jnp.float32) / dim))
    ang = pos[:, None] * inv_freq[None, :]
    ang = jnp.concatenate([ang, ang], axis=-1)
    return jnp.cos(ang), jnp.sin(ang)


def apply_rope(x, cos, sin):
    c = cos[None, :, None, :].astype(x.dtype)
    s = sin[None, :, None, :].astype(x.dtype)
    x1, x2 = jnp.split(x, 2, axis=-1)
    rot = jnp.concatenate([-x2, x1], axis=-1)
    return x * c + rot * s


def macaron_half_ffn(x, g_pre, g_post, w_in, w_out):
    h = rms_norm(x, g_pre)
    gu = h @ w_in
    gate, up = jnp.split(gu, 2, axis=-1)
    y = (jax.nn.silu(gate) * up) @ w_out
    return x + FFN_RESIDUAL_SCALE * rms_norm(y, g_post)


def diff_attention(h, w_qkv, lam_vecs, subln, w_o, lam_init, cos, sin):
    B, S, _ = h.shape
    H, Dh = DIFF_HEADS, HEAD_DIM
    q, k, v = jnp.split(h @ w_qkv, 3, axis=-1)
    q = apply_rope(q.reshape(B, S, 2 * H, Dh), cos, sin).transpose(0, 2, 1, 3)
    k = apply_rope(k.reshape(B, S, 2 * H, Dh), cos, sin).transpose(0, 2, 1, 3)
    v = v.reshape(B, S, H, 2 * Dh).transpose(0, 2, 1, 3)
    lv = lam_vecs.astype(jnp.float32)
    lam = jnp.exp(jnp.sum(lv[0] * lv[1])) - jnp.exp(jnp.sum(lv[2] * lv[3])) + lam_init
    nq = S // DIFF_Q_CHUNK
    q_chunks = q.reshape(B, 2 * H, nq, DIFF_Q_CHUNK, Dh).transpose(2, 0, 1, 3, 4)
    kpos = jnp.arange(S)
    scale = Dh ** -0.5

    def one_chunk(args):
        c, qc = args
        qpos = c * DIFF_Q_CHUNK + jnp.arange(DIFF_Q_CHUNK)
        s = jnp.einsum('bhqd,bhkd->bhqk', qc, k).astype(jnp.float32) * scale
        s = jnp.where(kpos[None, :] <= qpos[:, None], s, -jnp.inf)
        p = jax.nn.softmax(s, axis=-1).reshape(B, H, 2, DIFF_Q_CHUNK, S)
        a = p[:, :, 0] - lam * p[:, :, 1]
        return jnp.einsum('bhqk,bhkd->bhqd', a.astype(v.dtype), v)

    o = lax.map(one_chunk, (jnp.arange(nq), q_chunks))
    o = o.transpose(1, 0, 3, 2, 4).reshape(B, S, H, 2 * Dh)
    o = rms_norm(o, subln) * (1.0 - lam_init)
    return o.reshape(B, S, D_MODEL) @ w_o


def shared_kv(x, kv_norm, kv_w, cos, sin):
    B, S, _ = x.shape
    H, Dh = MOBA_HEADS, HEAD_DIM
    k, v = jnp.split(rms_norm(x, kv_norm) @ kv_w, 2, axis=-1)
    k = apply_rope(k.reshape(B, S, H, Dh), cos, sin).transpose(0, 2, 1, 3)
    v = v.reshape(B, S, H, Dh).transpose(0, 2, 1, 3)
    nb = -(-S // MOBA_BLOCK)
    pad = nb * MOBA_BLOCK - S
    k = jnp.pad(k, ((0, 0), (0, 0), (0, pad), (0, 0)))
    v = jnp.pad(v, ((0, 0), (0, 0), (0, pad), (0, 0)))
    kb = k.reshape(B, H, nb, MOBA_BLOCK, Dh)
    vb = v.reshape(B, H, nb, MOBA_BLOCK, Dh)
    kmean = jnp.mean(kb.astype(jnp.float32), axis=3).astype(kb.dtype)
    return kb, vb, kmean


def moba_attention(h, w_q, w_o, kb, vb, kmean, cos, sin):
    B, S, _ = h.shape
    H, Dh = MOBA_HEADS, HEAD_DIM
    q = apply_rope((h @ w_q).reshape(B, S, H, Dh), cos, sin).transpose(0, 2, 1, 3)
    nb = kb.shape[2]
    topk = min(MOBA_TOPK, nb)
    nq = S // MOBA_Q_CHUNK
    q_chunks = q.reshape(B, H, nq, MOBA_Q_CHUNK, Dh).transpose(2, 0, 1, 3, 4)
    bi = jnp.arange(B)[:, None, None, None]
    hi = jnp.arange(H)[None, :, None, None]
    blk_ids = jnp.arange(nb)
    in_blk = jnp.arange(MOBA_BLOCK)
    scale = Dh ** -0.5

    def one_chunk(args):
        c, qc = args
        q0 = c * MOBA_Q_CHUNK
        qpos = q0 + jnp.arange(MOBA_Q_CHUNK)
        qblk = q0 // MOBA_BLOCK
        gate = jnp.einsum('bhqd,bhnd->bhqn', qc, kmean).astype(jnp.float32)
        gate = jnp.where(blk_ids < qblk, gate, -jnp.inf)
        _, idx = lax.top_k(gate, topk)
        valid = idx < qblk
        ksel = kb[bi, hi, idx]
        vsel = vb[bi, hi, idx]
        s_sel = jnp.einsum('bhqd,bhqjpd->bhqjp', qc, ksel).astype(jnp.float32) * scale
        s_sel = jnp.where(valid[..., None], s_sel, -jnp.inf)
        s_sel = s_sel.reshape(B, H, MOBA_Q_CHUNK, topk * MOBA_BLOCK)
        kown = lax.dynamic_index_in_dim(kb, qblk, axis=2, keepdims=False)
        vown = lax.dynamic_index_in_dim(vb, qblk, axis=2, keepdims=False)
        s_own = jnp.einsum('bhqd,bhpd->bhqp', qc, kown).astype(jnp.float32) * scale
        kpos = qblk * MOBA_BLOCK + in_blk
        s_own = jnp.where(kpos[None, :] <= qpos[:, None], s_own, -jnp.inf)
        p = jax.nn.softmax(jnp.concatenate([s_sel, s_own], axis=-1), axis=-1).astype(vb.dtype)
        p_sel = p[..., :topk * MOBA_BLOCK].reshape(B, H, MOBA_Q_CHUNK, topk, MOBA_BLOCK)
        p_own = p[..., topk * MOBA_BLOCK:]
        return (jnp.einsum('bhqjp,bhqjpd->bhqd', p_sel, vsel)
                + jnp.einsum('bhqp,bhpd->bhqd', p_own, vown))

    o = lax.map(one_chunk, (jnp.arange(nq), q_chunks))
    o = o.transpose(1, 0, 3, 2, 4).reshape(B, S, D_MODEL)
    return o @ w_o


def setup_inputs(seed: int = 0) -> dict:
    key = jax.random.key(seed)
    ks = jax.random.split(key, 16)
    f32 = jnp.float32
    nrm = lambda k, shape, s: jax.random.normal(k, shape, f32) * s
    gain = lambda k, shape: 1.0 + 0.05 * jax.random.normal(k, shape, f32)
    return {
        "x": jax.random.normal(ks[0], (BATCH, SEQ, D_MODEL), f32),
        "ffn_norm": gain(ks[1], (DEPTH, 2, 2, D_MODEL)),
        "ffn_w_in": nrm(ks[2], (DEPTH, 2, D_MODEL, 2 * D_FF), D_MODEL ** -0.5),
        "ffn_w_out": nrm(ks[3], (DEPTH, 2, D_FF, D_MODEL), D_FF ** -0.5),
        "mix_norm": gain(ks[4], (DEPTH, 2, D_MODEL)),
        "diff_w_qkv": nrm(ks[5], (N_A_LAYERS, D_MODEL, 3 * D_MODEL), D_MODEL ** -0.5),
        "diff_lambda": nrm(ks[6], (N_A_LAYERS, 4, HEAD_DIM), 0.1),
        "diff_subln": gain(ks[7], (N_A_LAYERS, 2 * HEAD_DIM)),
        "diff_w_o": nrm(ks[8], (N_A_LAYERS, D_MODEL, D_MODEL), D_MODEL ** -0.5),
        "kv_norm": gain(ks[9], (D_MODEL,)),
        "kv_w": nrm(ks[10], (D_MODEL, 2 * D_MODEL), D_MODEL ** -0.5),
        "moba_w_q": nrm(ks[11], (N_B_LAYERS, D_MODEL, D_MODEL), D_MODEL ** -0.5),
        "moba_w_o": nrm(ks[12], (N_B_LAYERS, D_MODEL, D_MODEL), D_MODEL ** -0.5),
    }


def reference(x, ffn_norm, ffn_w_in, ffn_w_out, mix_norm, diff_w_qkv, diff_lambda,
              diff_subln, diff_w_o, kv_norm, kv_w, moba_w_q, moba_w_o):
    S = x.shape[1]
    cos, sin = rope_tables(S, HEAD_DIM)
    kb = vb = kmean = None
    for l in range(DEPTH):
        if l == N_A_LAYERS:
            kb, vb, kmean = shared_kv(x, kv_norm, kv_w, cos, sin)
        x = macaron_half_ffn(x, ffn_norm[l, 0, 0], ffn_norm[l, 0, 1], ffn_w_in[l, 0], ffn_w_out[l, 0])
        h = rms_norm(x, mix_norm[l, 0])
        if l < N_A_LAYERS:
            lam_init = 0.8 - 0.6 * math.exp(-0.3 * l)
            y = diff_attention(h, diff_w_qkv[l], diff_lambda[l], diff_subln[l], diff_w_o[l],
                               lam_init, cos, sin)
        else:
            j = l - N_A_LAYERS
            y = moba_attention(h, moba_w_q[j], moba_w_o[j], kb, vb, kmean, cos, sin)
        x = x + rms_norm(y, mix_norm[l, 1])
        x = macaron_half_ffn(x, ffn_norm[l, 1, 0], ffn_norm[l, 1, 1], ffn_w_in[l, 1], ffn_w_out[l, 1])
    return x
```

```python
import functools

import jax
import jax.numpy as jnp
from jax import lax
from jax.experimental import pallas as pl
from jax.experimental.pallas import tpu as pltpu

F32 = jnp.float32
BF16 = jnp.bfloat16

D_MODEL = 1024
HEAD_DIM = 64
PAIR = 2 * HEAD_DIM
N_PAIRS = D_MODEL // PAIR
D_FF = 2816
ROPE_THETA = 10000.0
MOBA_BLOCK = 256
MOBA_TOPK = 3
FFN_RESIDUAL_SCALE = 0.5
NORM_EPS = 1e-6
DIFF_LAMBDA_INIT = 0.2
MASK_NEG = -1e30

FFN_TM = 512
FFN_TF = D_FF // 2
PROJ_TM = 512
KV_TM = 1024
OUT_TM = 512
ATT_T = 512
VMEM_LIMIT = 56 * 1024 * 1024


def _rms(x, g):
    return x * lax.rsqrt(jnp.mean(x * x, axis=-1, keepdims=True) + NORM_EPS) * g


def _nt_dot(a, b, precision=None):
    return lax.dot_general(a, b, (((1,), (1,)), ((), ())),
                           preferred_element_type=F32, precision=precision)


def _ffn_kernel(x_ref, gpre_ref, gpost_ref, wg_ref, wu_ref, wo_ref, o_ref, h_ref, acc_ref):
    j = pl.program_id(1)

    @pl.when(j == 0)
    def _():
        h_ref[...] = _rms(x_ref[...], gpre_ref[...]).astype(BF16)

    h = h_ref[...]
    gate = jnp.dot(h, wg_ref[...], preferred_element_type=F32)
    up = jnp.dot(h, wu_ref[...], preferred_element_type=F32)
    act = (gate * (1.0 / (1.0 + jnp.exp(-gate))) * up).astype(BF16)
    part = jnp.dot(act, wo_ref[...], preferred_element_type=F32)

    @pl.when(j == 0)
    def _():
        acc_ref[...] = part

    @pl.when(j > 0)
    def _():
        acc_ref[...] += part

    @pl.when(j == pl.num_programs(1) - 1)
    def _():
        o_ref[...] = x_ref[...] + FFN_RESIDUAL_SCALE * _rms(acc_ref[...], gpost_ref[...])


def _ffn_half(x, g_pre, g_post, w_in, w_out):
    s = x.shape[0]
    nf = D_FF // FFN_TF
    return pl.pallas_call(
        _ffn_kernel,
        grid=(s // FFN_TM, nf),
        in_specs=[
            pl.BlockSpec((FFN_TM, D_MODEL), lambda i, j: (i, 0)),
            pl.BlockSpec((1, D_MODEL), lambda i, j: (0, 0)),
            pl.BlockSpec((1, D_MODEL), lambda i, j: (0, 0)),
            pl.BlockSpec((D_MODEL, FFN_TF), lambda i, j: (0, j)),
            pl.BlockSpec((D_MODEL, FFN_TF), lambda i, j: (0, j + nf)),
            pl.BlockSpec((FFN_TF, D_MODEL), lambda i, j: (j, 0)),
        ],
        out_specs=pl.BlockSpec((FFN_TM, D_MODEL), lambda i, j: (i, 0)),
        out_shape=jax.ShapeDtypeStruct((s, D_MODEL), F32),
        scratch_shapes=[pltpu.VMEM((FFN_TM, D_MODEL), BF16), pltpu.VMEM((FFN_TM, D_MODEL), F32)],
        compiler_params=pltpu.CompilerParams(
            dimension_semantics=("parallel", "arbitrary"), vmem_limit_bytes=VMEM_LIMIT),
        name="ffn_half",
    )(x, g_pre.reshape(1, D_MODEL), g_post.reshape(1, D_MODEL), w_in, w_in, w_out)


def _rope_tables(s):
    pos = jnp.arange(s, dtype=F32)
    inv_freq = 1.0 / (ROPE_THETA ** (jnp.arange(0, HEAD_DIM, 2, dtype=F32) / HEAD_DIM))
    ang = pos[:, None] * inv_freq[None, :]
    cos, sin, zero = jnp.cos(ang), jnp.sin(ang), jnp.zeros_like(ang)
    cos_t = jnp.concatenate([cos] * 4, axis=-1)
    sin_lo = jnp.concatenate([-sin, zero] * 2, axis=-1)
    sin_hi = jnp.concatenate([zero, sin] * 2, axis=-1)
    return cos_t, sin_lo, sin_hi


def _proj_kernel(x_ref, g_ref, w_ref, cos_ref, sinlo_ref, sinhi_ref, *out_refs,
                 n_rope, scale0, with_kmean):
    n_chunks = len(out_refs) - (1 if with_kmean else 0)
    h = _rms(x_ref[...], g_ref[...]).astype(BF16)
    reps = D_MODEL // PAIR
    cos = jnp.concatenate([cos_ref[...]] * reps, axis=1)
    sin_lo = jnp.concatenate([sinlo_ref[...]] * reps, axis=1)
    sin_hi = jnp.concatenate([sinhi_ref[...]] * reps, axis=1)
    for c in range(n_chunks):
        y = jnp.dot(h, w_ref[:, c * D_MODEL:(c + 1) * D_MODEL], preferred_element_type=F32)
        if c < n_rope:
            half = HEAD_DIM // 2
            y = (y * cos + pltpu.roll(y, D_MODEL - half, 1) * sin_lo
                 + pltpu.roll(y, half, 1) * sin_hi)
            if c == 0 and with_kmean:
                tm = y.shape[0]
                km = jnp.mean(y.reshape(tm // MOBA_BLOCK, MOBA_BLOCK, D_MODEL), axis=1)
                out_refs[n_chunks][0] = km
        if c == 0 and scale0 != 1.0:
            y = y * scale0
        out_refs[c][...] = y.astype(BF16)


def _project(x, g, w, tables, *, n_rope, scale0, with_kmean, tm):
    s = x.shape[0]
    n_chunks = w.shape[1] // D_MODEL
    row = lambda i: (i, 0)
    out_shape = [jax.ShapeDtypeStruct((s, D_MODEL), BF16)] * n_chunks
    out_specs = [pl.BlockSpec((tm, D_MODEL), row)] * n_chunks
    if with_kmean:
        nb = tm // MOBA_BLOCK
        out_shape.append(jax.ShapeDtypeStruct((s // tm, nb, D_MODEL), F32))
        out_specs.append(pl.BlockSpec((1, nb, D_MODEL), lambda i: (i, 0, 0)))
    return pl.pallas_call(
        functools.partial(_proj_kernel, n_rope=n_rope, scale0=scale0, with_kmean=with_kmean),
        grid=(s // tm,),
        in_specs=[
            pl.BlockSpec((tm, D_MODEL), row),
            pl.BlockSpec((1, D_MODEL), lambda i: (0, 0)),
            pl.BlockSpec(w.shape, lambda i: (0, 0)),
            pl.BlockSpec((tm, PAIR), row),
            pl.BlockSpec((tm, PAIR), row),
            pl.BlockSpec((tm, PAIR), row),
        ],
        out_specs=out_specs,
        out_shape=out_shape,
        compiler_params=pltpu.CompilerParams(
            dimension_semantics=("parallel",), vmem_limit_bytes=VMEM_LIMIT),
        name="project",
    )(x, g.reshape(1, D_MODEL), w, *tables)


def _stack_heads(q):
    lane = lax.broadcasted_iota(jnp.int32, q.shape, 1)
    zero = jnp.zeros_like(q)
    return jnp.where(lane < HEAD_DIM, q, zero), jnp.where(lane >= HEAD_DIM, q, zero)


def _softmax_step(s, v, m_ref, l_ref, acc_ref):
    m_prev = m_ref[...]
    m_new = jnp.maximum(m_prev, jnp.max(s, axis=-1, keepdims=True))
    alpha = jnp.exp(m_prev - m_new)
    p = jnp.exp(s - m_new)
    l_ref[...] = alpha * l_ref[...] + jnp.sum(p, axis=-1, keepdims=True)
    acc_ref[...] = alpha * acc_ref[...] + jnp.dot(p.astype(BF16), v, preferred_element_type=F32)
    m_ref[...] = m_new


def _causal_mask(s, t):
    row = lax.broadcasted_iota(jnp.int32, s.shape, 0)
    col = lax.broadcasted_iota(jnp.int32, s.shape, 1)
    qpos = jnp.where(row >= t, row - t, row)
    return jnp.where(col <= qpos, s, -jnp.inf)


def _diff_attn_kernel(q_ref, k_ref, v_ref, lam_ref, subln_ref, o_ref,
                      qs_ref, m_ref, l_ref, acc_ref):
    t = ATT_T
    i = pl.program_id(1)
    q_lo, q_hi = _stack_heads(q_ref[...])
    qs_ref[0:t, :] = q_lo
    qs_ref[t:, :] = q_hi
    m_ref[...] = jnp.full(m_ref.shape, -jnp.inf, F32)
    l_ref[...] = jnp.zeros(l_ref.shape, F32)
    acc_ref[...] = jnp.zeros(acc_ref.shape, F32)

    def step(j, masked):
        off = pl.multiple_of(j * t, t)
        s = _nt_dot(qs_ref[...], k_ref[pl.ds(off, t), :])
        if masked:
            s = _causal_mask(s, t)
        _softmax_step(s, v_ref[pl.ds(off, t), :], m_ref, l_ref, acc_ref)

    def body(j, carry):
        step(j, False)
        return carry

    lax.fori_loop(0, i, body, 0)
    step(i, True)

    lv = lam_ref[...]
    lam = (jnp.exp(jnp.sum(lv[0:1] * lv[1:2], axis=-1, keepdims=True))
           - jnp.exp(jnp.sum(lv[2:3] * lv[3:4], axis=-1, keepdims=True)) + DIFF_LAMBDA_INIT)
    o = acc_ref[0:t, :] / l_ref[0:t, :] - lam * (acc_ref[t:, :] / l_ref[t:, :])
    o_ref[...] = (_rms(o, subln_ref[...]) * (1.0 - DIFF_LAMBDA_INIT)).astype(BF16)


def _diff_attention(q, k, v, lam_vecs, subln):
    s = q.shape[0]
    t = ATT_T
    seq = lambda h, i: (0, h)
    return pl.pallas_call(
        _diff_attn_kernel,
        grid=(N_PAIRS, s // t),
        in_specs=[
            pl.BlockSpec((t, PAIR), lambda h, i: (i, h)),
            pl.BlockSpec((s, PAIR), seq),
            pl.BlockSpec((s, PAIR), seq),
            pl.BlockSpec((4, HEAD_DIM), lambda h, i: (0, 0)),
            pl.BlockSpec((1, PAIR), lambda h, i: (0, 0)),
        ],
        out_specs=pl.BlockSpec((t, PAIR), lambda h, i: (i, h)),
        out_shape=jax.ShapeDtypeStruct((s, D_MODEL), BF16),
        scratch_shapes=[pltpu.VMEM((2 * t, PAIR), BF16), pltpu.VMEM((2 * t, 1), F32),
                        pltpu.VMEM((2 * t, 1), F32), pltpu.VMEM((2 * t, PAIR), F32)],
        compiler_params=pltpu.CompilerParams(
            dimension_semantics=("parallel", "arbitrary"), vmem_limit_bytes=VMEM_LIMIT),
        name="diff_attn",
    )(q, k, v, lam_vecs, subln.reshape(1, PAIR))


def _moba_bias(gate, qblk):
    n = lax.broadcasted_iota(jnp.int32, gate.shape, 1)
    nf = n.astype(F32)
    past = n < qblk
    g = jnp.where(past, gate, -jnp.inf)
    picked = jnp.zeros(gate.shape, F32)
    for _ in range(MOBA_TOPK):
        mx = jnp.max(g, axis=-1, keepdims=True)
        first = jnp.min(jnp.where(g == mx, nf, float(PAIR)), axis=-1, keepdims=True)
        hit = nf == first
        picked = jnp.where(hit, 1.0, picked)
        g = jnp.where(hit, -jnp.inf, g)
    sel_past = jnp.where(past, jnp.where(picked > 0.0, 0.0, MASK_NEG), MASK_NEG)
    return jnp.where(n == qblk, 0.0, sel_past)


def _moba_attn_kernel(q_ref, k_ref, v_ref, kmean_ref, o_ref,
                      qa_ref, kaug_ref, m_ref, l_ref, acc_ref):
    t = ATT_T
    i = pl.program_id(1)
    s_len = k_ref.shape[0]

    @pl.when(i == 0)
    def _():
        kaug_ref[:, 0:PAIR] = k_ref[...]
        blk = lax.broadcasted_iota(jnp.int32, (s_len, PAIR), 0) // MOBA_BLOCK
        lane = lax.broadcasted_iota(jnp.int32, (s_len, PAIR), 1)
        kaug_ref[:, PAIR:] = jnp.where(blk == lane, 1.0, 0.0).astype(BF16)

    q_lo, q_hi = _stack_heads(q_ref[...])
    qs = jnp.concatenate([q_lo, q_hi], axis=0)
    gate = _nt_dot(qs.astype(F32), kmean_ref[...], precision=lax.Precision.HIGHEST)
    row = lax.broadcasted_iota(jnp.int32, (2 * t, 1), 0)
    qblk = (i * t + jnp.where(row >= t, row - t, row)) // MOBA_BLOCK
    qa_ref[:, 0:PAIR] = qs
    qa_ref[:, PAIR:] = _moba_bias(gate, qblk).astype(BF16)
    m_ref[...] = jnp.full(m_ref.shape, -jnp.inf, F32)
    l_ref[...] = jnp.zeros(l_ref.shape, F32)
    acc_ref[...] = jnp.zeros(acc_ref.shape, F32)

    def step(j, masked):
        off = pl.multiple_of(j * t, t)
        s = _nt_dot(qa_ref[...], kaug_ref[pl.ds(off, t), :])
        if masked:
            s = _causal_mask(s, t)
        _softmax_step(s, v_ref[pl.ds(off, t), :], m_ref, l_ref, acc_ref)

    def body(j, carry):
        step(j, False)
        return carry

    step(i, True)
    lax.fori_loop(0, i, body, 0)

    lane = lax.broadcasted_iota(jnp.int32, (t, PAIR), 1)
    o = jnp.where(lane < HEAD_DIM, acc_ref[0:t, :] / l_ref[0:t, :], acc_ref[t:, :] / l_ref[t:, :])
    o_ref[...] = o.astype(BF16)


def _moba_attention(q, k, v, kmean):
    s = q.shape[0]
    t = ATT_T
    seq = lambda h, i: (0, h)
    return pl.pallas_call(
        _moba_attn_kernel,
        grid=(N_PAIRS, s // t),
        in_specs=[
            pl.BlockSpec((t, PAIR), lambda h, i: (i, h)),
            pl.BlockSpec((s, PAIR), seq),
            pl.BlockSpec((s, PAIR), seq),
            pl.BlockSpec((PAIR, PAIR), seq),
        ],
        out_specs=pl.BlockSpec((t, PAIR), lambda h, i: (i, h)),
        out_shape=jax.ShapeDtypeStruct((s, D_MODEL), BF16),
        scratch_shapes=[pltpu.VMEM((2 * t, 2 * PAIR), BF16), pltpu.VMEM((s, 2 * PAIR), BF16),
                        pltpu.VMEM((2 * t, 1), F32), pltpu.VMEM((2 * t, 1), F32),
                        pltpu.VMEM((2 * t, PAIR), F32)],
        compiler_params=pltpu.CompilerParams(
            dimension_semantics=("arbitrary", "arbitrary"), vmem_limit_bytes=VMEM_LIMIT),
        name="moba_attn",
    )(q, k, v, kmean)


def _out_proj_kernel(o_ref, w_ref, x_ref, g_ref, y_ref):
    y = jnp.dot(o_ref[...], w_ref[...], preferred_element_type=F32)
    y_ref[...] = x_ref[...] + _rms(y, g_ref[...])


def _out_proj(o, w, x, g):
    s = x.shape[0]
    row = lambda i: (i, 0)
    return pl.pallas_call(
        _out_proj_kernel,
        grid=(s // OUT_TM,),
        in_specs=[
            pl.BlockSpec((OUT_TM, D_MODEL), row),
            pl.BlockSpec((D_MODEL, D_MODEL), lambda i: (0, 0)),
            pl.BlockSpec((OUT_TM, D_MODEL), row),
            pl.BlockSpec((1, D_MODEL), lambda i: (0, 0)),
        ],
        out_specs=pl.BlockSpec((OUT_TM, D_MODEL), row),
        out_shape=jax.ShapeDtypeStruct((s, D_MODEL), F32),
        compiler_params=pltpu.CompilerParams(
            dimension_semantics=("parallel",), vmem_limit_bytes=VMEM_LIMIT),
        name="out_proj",
    )(o, w, x, g.reshape(1, D_MODEL))


def kernel(x, ffn_norm, ffn_w_in, ffn_w_out, mix_norm, diff_w_qkv, diff_lambda, diff_subln,
           diff_w_o, kv_norm, kv_w, moba_w_q, moba_w_o):
    b, s, d = x.shape
    assert b == 1 and d == D_MODEL and s % KV_TM == 0 and s // MOBA_BLOCK <= HEAD_DIM
    assert ffn_w_in.shape[0] == 2 and diff_w_qkv.shape[0] == 1 and moba_w_q.shape[0] == 1
    scale = HEAD_DIM ** -0.5
    tables = _rope_tables(s)
    w_in, w_out = ffn_w_in.astype(BF16), ffn_w_out.astype(BF16)

    def ffn(xx, layer, slot):
        return _ffn_half(xx, ffn_norm[layer, slot, 0], ffn_norm[layer, slot, 1],
                         w_in[layer, slot], w_out[layer, slot])

    xs = x.reshape(s, d)

    xs = ffn(xs, 0, 0)
    q, k, v = _project(xs, mix_norm[0, 0], diff_w_qkv[0].astype(BF16), tables,
                       n_rope=2, scale0=scale, with_kmean=False, tm=PROJ_TM)
    o = _diff_attention(q, k, v, diff_lambda[0], diff_subln[0])
    xs = _out_proj(o, diff_w_o[0].astype(BF16), xs, mix_norm[0, 1])
    xs = ffn(xs, 0, 1)

    k, v, kmean = _project(xs, kv_norm, kv_w.astype(BF16), tables,
                           n_rope=1, scale0=1.0, with_kmean=True, tm=KV_TM)
    kmean = jnp.pad(kmean.reshape(s // MOBA_BLOCK, d), ((0, PAIR - s // MOBA_BLOCK), (0, 0)))
    xs = ffn(xs, 1, 0)
    (q,) = _project(xs, mix_norm[1, 0], moba_w_q[0].astype(BF16), tables,
                    n_rope=1, scale0=scale, with_kmean=False, tm=PROJ_TM)
    o = _moba_attention(q, k, v, kmean)
    xs = _out_proj(o, moba_w_o[0].astype(BF16), xs, mix_norm[1, 1])
    xs = ffn(xs, 1, 1)
    return xs.reshape(b, s, d)
```

```python
import functools
import math

import jax
import jax.numpy as jnp
from jax import lax
from jax.experimental import pallas as pl
from jax.experimental.pallas import tpu as pltpu

F32 = jnp.float32
BF16 = jnp.bfloat16

D_MODEL = 1024
HEAD_DIM = 64
PAIR = 2 * HEAD_DIM
N_PAIRS = D_MODEL // PAIR
D_FF = 2816
ROPE_THETA = 10000.0
MOBA_BLOCK = 256
MOBA_TOPK = 3
FFN_RESIDUAL_SCALE = 0.5
NORM_EPS = 1e-6
DIFF_LAMBDA_INIT = 0.2
MASK_NEG = -1e30
Q_SCALE = HEAD_DIM ** -0.5 * math.log2(math.e)

FFN_TM = 512
FFN_TF = D_FF // 2
ATT_T = 512
OUT_TM = 512
VMEM_LIMIT = 56 * 1024 * 1024


def _rms(x, g):
    return x * lax.rsqrt(jnp.mean(x * x, axis=-1, keepdims=True) + NORM_EPS) * g


def _ffn_kernel(x_ref, gpre_ref, gpost_ref, wg_ref, wu_ref, wo_ref, o_ref, h_ref, acc_ref):
    j = pl.program_id(1)

    @pl.when(j == 0)
    def _():
        h_ref[...] = _rms(x_ref[...], gpre_ref[...]).astype(BF16)

    h = h_ref[...]
    gate = jnp.dot(h, wg_ref[...], preferred_element_type=F32)
    up = jnp.dot(h, wu_ref[...], preferred_element_type=F32)
    act = (gate * (1.0 / (1.0 + jnp.exp(-gate))) * up).astype(BF16)
    part = jnp.dot(act, wo_ref[...], preferred_element_type=F32)

    @pl.when(j == 0)
    def _():
        acc_ref[...] = part

    @pl.when(j > 0)
    def _():
        acc_ref[...] += part

    @pl.when(j == pl.num_programs(1) - 1)
    def _():
        o_ref[...] = x_ref[...] + FFN_RESIDUAL_SCALE * _rms(acc_ref[...], gpost_ref[...])


def _ffn_half(x, g_pre, g_post, w_in, w_out):
    s = x.shape[0]
    nf = D_FF // FFN_TF
    return pl.pallas_call(
        _ffn_kernel,
        grid=(s // FFN_TM, nf),
        in_specs=[
            pl.BlockSpec((FFN_TM, D_MODEL), lambda i, j: (i, 0)),
            pl.BlockSpec((1, D_MODEL), lambda i, j: (0, 0)),
            pl.BlockSpec((1, D_MODEL), lambda i, j: (0, 0)),
            pl.BlockSpec((D_MODEL, FFN_TF), lambda i, j: (0, j)),
            pl.BlockSpec((D_MODEL, FFN_TF), lambda i, j: (0, j + nf)),
            pl.BlockSpec((FFN_TF, D_MODEL), lambda i, j: (j, 0)),
        ],
        out_specs=pl.BlockSpec((FFN_TM, D_MODEL), lambda i, j: (i, 0)),
        out_shape=jax.ShapeDtypeStruct((s, D_MODEL), F32),
        scratch_shapes=[pltpu.VMEM((FFN_TM, D_MODEL), BF16), pltpu.VMEM((FFN_TM, D_MODEL), F32)],
        compiler_params=pltpu.CompilerParams(
            dimension_semantics=("parallel", "arbitrary"), vmem_limit_bytes=VMEM_LIMIT),
        name="ffn_half",
    )(x, g_pre.reshape(1, D_MODEL), g_post.reshape(1, D_MODEL), w_in, w_in, w_out)


def _rope_tables(s):
    pos = jnp.arange(s, dtype=F32)
    inv_freq = 1.0 / (ROPE_THETA ** (jnp.arange(0, HEAD_DIM, 2, dtype=F32) / HEAD_DIM))
    ang = pos[:, None] * inv_freq[None, :]
    cos, sin, zero = jnp.cos(ang), jnp.sin(ang), jnp.zeros_like(ang)
    cos_t = jnp.concatenate([cos] * 4, axis=-1)
    sin_lo = jnp.concatenate([-sin, zero] * 2, axis=-1)
    sin_hi = jnp.concatenate([zero, sin] * 2, axis=-1)
    return cos_t, sin_lo, sin_hi


def _proj_kernel(x_ref, g_ref, w_ref, cos_ref, sinlo_ref, sinhi_ref, *out_refs,
                 n_rope, scale0, transposed, with_kmean):
    n_chunks = len(transposed)
    h = _rms(x_ref[...], g_ref[...]).astype(BF16)
    tm = h.shape[0]
    reps = D_MODEL // PAIR
    cos = jnp.concatenate([cos_ref[...]] * reps, axis=1)
    sin_lo = jnp.concatenate([sinlo_ref[...]] * reps, axis=1)
    sin_hi = jnp.concatenate([sinhi_ref[...]] * reps, axis=1)
    for c in range(n_chunks):
        y = jnp.dot(h, w_ref[:, c * D_MODEL:(c + 1) * D_MODEL], preferred_element_type=F32)
        if c < n_rope:
            half = HEAD_DIM // 2
            y = (y * cos + pltpu.roll(y, D_MODEL - half, 1) * sin_lo
                 + pltpu.roll(y, half, 1) * sin_hi)
            if c == 0 and with_kmean:
                km = jnp.mean(y.reshape(tm // MOBA_BLOCK, MOBA_BLOCK, D_MODEL), axis=1)
                out_refs[n_chunks][0] = km
        if c == 0 and scale0 != 1.0:
            y = y * scale0
        if transposed[c]:
            out_refs[c][...] = y.T.reshape(N_PAIRS, 1, PAIR, tm).astype(BF16)
        else:
            out_refs[c][...] = y.astype(BF16)


def _project(x, g, w, tables, *, n_rope, scale0, transposed, with_kmean):
    s = x.shape[0]
    tm = ATT_T
    row = lambda i: (i, 0)
    out_shape, out_specs = [], []
    for tr in transposed:
        if tr:
            out_shape.append(jax.ShapeDtypeStruct((N_PAIRS, s // tm, PAIR, tm), BF16))
            out_specs.append(pl.BlockSpec((N_PAIRS, 1, PAIR, tm), lambda i: (0, i, 0, 0)))
        else:
            out_shape.append(jax.ShapeDtypeStruct((s, D_MODEL), BF16))
            out_specs.append(pl.BlockSpec((tm, D_MODEL), row))
    if with_kmean:
        nb = tm // MOBA_BLOCK
        out_shape.append(jax.ShapeDtypeStruct((s // tm, nb, D_MODEL), F32))
        out_specs.append(pl.BlockSpec((1, nb, D_MODEL), lambda i: (i, 0, 0)))
    return pl.pallas_call(
        functools.partial(_proj_kernel, n_rope=n_rope, scale0=scale0, transposed=transposed,
                          with_kmean=with_kmean),
        grid=(s // tm,),
        in_specs=[
            pl.BlockSpec((tm, D_MODEL), row),
            pl.BlockSpec((1, D_MODEL), lambda i: (0, 0)),
            pl.BlockSpec(w.shape, lambda i: (0, 0)),
            pl.BlockSpec((tm, PAIR), row),
            pl.BlockSpec((tm, PAIR), row),
            pl.BlockSpec((tm, PAIR), row),
        ],
        out_specs=out_specs,
        out_shape=out_shape,
        compiler_params=pltpu.CompilerParams(
            dimension_semantics=("parallel",), vmem_limit_bytes=VMEM_LIMIT),
        name="project",
    )(x, g.reshape(1, D_MODEL), w, *tables)


def _stack_heads_t(qt):
    feat = lax.broadcasted_iota(jnp.int32, qt.shape, 0)
    zero = jnp.zeros_like(qt)
    return jnp.concatenate([jnp.where(feat < HEAD_DIM, qt, zero),
                            jnp.where(feat >= HEAD_DIM, qt, zero)], axis=1)


def _softmax_step(st, vt, m_ref, l_ref, acc_ref):
    m_prev = m_ref[...]
    m_new = jnp.maximum(m_prev, jnp.max(st, axis=0, keepdims=True))
    alpha = jnp.exp2(m_prev - m_new)
    p = jnp.exp2(st - m_new)
    l_ref[...] = alpha * l_ref[...] + jnp.sum(p, axis=0, keepdims=True)
    acc_ref[...] = alpha * acc_ref[...] + jnp.dot(vt, p.astype(BF16), preferred_element_type=F32)
    m_ref[...] = m_new


def _causal_mask_t(st, t):
    key = lax.broadcasted_iota(jnp.int32, st.shape, 0)
    col = lax.broadcasted_iota(jnp.int32, st.shape, 1)
    qpos = jnp.where(col >= t, col - t, col)
    return jnp.where(key <= qpos, st, -jnp.inf)


def _init_stats(m_ref, l_ref, acc_ref):
    m_ref[...] = jnp.full(m_ref.shape, -jnp.inf, F32)
    l_ref[...] = jnp.zeros(l_ref.shape, F32)
    acc_ref[...] = jnp.zeros(acc_ref.shape, F32)


def _diff_attn_kernel(qt_ref, k_ref, vt_ref, lam_ref, subln_ref, o_ref,
                      qs_ref, m_ref, l_ref, acc_ref):
    t = ATT_T
    i = pl.program_id(1)
    qs_ref[...] = _stack_heads_t(qt_ref[...])
    _init_stats(m_ref, l_ref, acc_ref)

    def step(j, masked):
        off = pl.multiple_of(j * t, t)
        st = jnp.dot(k_ref[pl.ds(off, t), :], qs_ref[...], preferred_element_type=F32)
        if masked:
            st = _causal_mask_t(st, t)
        _softmax_step(st, vt_ref[j], m_ref, l_ref, acc_ref)

    def body(j, carry):
        step(j, False)
        return carry

    lax.fori_loop(0, i, body, 0)
    step(i, True)

    lv = lam_ref[...]
    lam = (jnp.exp(jnp.sum(lv[0:1] * lv[1:2], axis=-1, keepdims=True))
           - jnp.exp(jnp.sum(lv[2:3] * lv[3:4], axis=-1, keepdims=True)) + DIFF_LAMBDA_INIT)
    ot = acc_ref[:, 0:t] / l_ref[:, 0:t] - lam * (acc_ref[:, t:] / l_ref[:, t:])
    o_ref[...] = (_rms(ot.T, subln_ref[...]) * (1.0 - DIFF_LAMBDA_INIT)).astype(BF16)


def _attn_specs(s, t):
    qt_spec = pl.BlockSpec((None, None, PAIR, t), lambda h, i: (h, i, 0, 0))
    k_spec = pl.BlockSpec((s, PAIR), lambda h, i: (0, h))
    vt_spec = pl.BlockSpec((None, s // t, PAIR, t), lambda h, i: (h, 0, 0, 0))
    o_spec = pl.BlockSpec((t, PAIR), lambda h, i: (i, h))
    return qt_spec, k_spec, vt_spec, o_spec


def _diff_attention(qt, k, vt, lam_vecs, subln):
    s = k.shape[0]
    t = ATT_T
    qt_spec, k_spec, vt_spec, o_spec = _attn_specs(s, t)
    return pl.pallas_call(
        _diff_attn_kernel,
        grid=(N_PAIRS, s // t),
        in_specs=[qt_spec, k_spec, vt_spec,
                  pl.BlockSpec((4, HEAD_DIM), lambda h, i: (0, 0)),
                  pl.BlockSpec((1, PAIR), lambda h, i: (0, 0))],
        out_specs=o_spec,
        out_shape=jax.ShapeDtypeStruct((s, D_MODEL), BF16),
        scratch_shapes=[pltpu.VMEM((PAIR, 2 * t), BF16), pltpu.VMEM((1, 2 * t), F32),
                        pltpu.VMEM((1, 2 * t), F32), pltpu.VMEM((PAIR, 2 * t), F32)],
        compiler_params=pltpu.CompilerParams(
            dimension_semantics=("parallel", "arbitrary"), vmem_limit_bytes=VMEM_LIMIT),
        name="diff_attn",
    )(qt, k, vt, lam_vecs, subln.reshape(1, PAIR))


def _moba_bias_t(gate, qblk):
    n = lax.broadcasted_iota(jnp.int32, gate.shape, 0)
    nf = n.astype(F32)
    past = n < qblk
    g = jnp.where(past, gate, -jnp.inf)
    picked = jnp.zeros(gate.shape, F32)
    for _ in range(MOBA_TOPK):
        mx = jnp.max(g, axis=0, keepdims=True)
        first = jnp.min(jnp.where(g == mx, nf, float(PAIR)), axis=0, keepdims=True)
        hit = nf == first
        picked = jnp.where(hit, 1.0, picked)
        g = jnp.where(hit, -jnp.inf, g)
    sel_past = jnp.where(past, jnp.where(picked > 0.0, 0.0, MASK_NEG), MASK_NEG)
    return jnp.where(n == qblk, 0.0, sel_past)


def _moba_attn_kernel(qt_ref, k_ref, vt_ref, kmean_ref, o_ref,
                      qa_ref, kaug_ref, m_ref, l_ref, acc_ref):
    t = ATT_T
    i = pl.program_id(1)
    s_len = k_ref.shape[0]

    @pl.when(i == 0)
    def _():
        kaug_ref[:, 0:PAIR] = k_ref[...]
        blk = lax.broadcasted_iota(jnp.int32, (s_len, PAIR), 0) // MOBA_BLOCK
        lane = lax.broadcasted_iota(jnp.int32, (s_len, PAIR), 1)
        kaug_ref[:, PAIR:] = jnp.where(blk == lane, 1.0, 0.0).astype(BF16)

    qs = _stack_heads_t(qt_ref[...])
    gate = jnp.dot(kmean_ref[...], qs.astype(F32), preferred_element_type=F32,
                   precision=lax.Precision.HIGHEST)
    col = lax.broadcasted_iota(jnp.int32, (1, 2 * t), 1)
    qblk = (i * t + jnp.where(col >= t, col - t, col)) // MOBA_BLOCK
    qa_ref[0:PAIR, :] = qs
    qa_ref[PAIR:, :] = _moba_bias_t(gate, qblk).astype(BF16)
    _init_stats(m_ref, l_ref, acc_ref)

    def step(j, masked):
        off = pl.multiple_of(j * t, t)
        st = jnp.dot(kaug_ref[pl.ds(off, t), :], qa_ref[...], preferred_element_type=F32)
        if masked:
            st = _causal_mask_t(st, t)
        _softmax_step(st, vt_ref[j], m_ref, l_ref, acc_ref)

    def body(j, carry):
        step(j, False)
        return carry

    step(i, True)
    lax.fori_loop(0, i, body, 0)

    feat = lax.broadcasted_iota(jnp.int32, (PAIR, t), 0)
    ot = jnp.where(feat < HEAD_DIM, acc_ref[:, 0:t] / l_ref[:, 0:t], acc_ref[:, t:] / l_ref[:, t:])
    o_ref[...] = ot.T.astype(BF16)


def _moba_attention(qt, k, vt, kmean):
    s = k.shape[0]
    t = ATT_T
    qt_spec, k_spec, vt_spec, o_spec = _attn_specs(s, t)
    return pl.pallas_call(
        _moba_attn_kernel,
        grid=(N_PAIRS, s // t),
        in_specs=[qt_spec, k_spec, vt_spec, pl.BlockSpec((PAIR, PAIR), lambda h, i: (0, h))],
        out_specs=o_spec,
        out_shape=jax.ShapeDtypeStruct((s, D_MODEL), BF16),
        scratch_shapes=[pltpu.VMEM((2 * PAIR, 2 * t), BF16), pltpu.VMEM((s, 2 * PAIR), BF16),
                        pltpu.VMEM((1, 2 * t), F32), pltpu.VMEM((1, 2 * t), F32),
                        pltpu.VMEM((PAIR, 2 * t), F32)],
        compiler_params=pltpu.CompilerParams(
            dimension_semantics=("arbitrary", "arbitrary"), vmem_limit_bytes=VMEM_LIMIT),
        name="moba_attn",
    )(qt, k, vt, kmean)


def _out_proj_kernel(o_ref, w_ref, x_ref, g_ref, y_ref):
    y = jnp.dot(o_ref[...], w_ref[...], preferred_element_type=F32)
    y_ref[...] = x_ref[...] + _rms(y, g_ref[...])


def _out_proj(o, w, x, g):
    s = x.shape[0]
    row = lambda i: (i, 0)
    return pl.pallas_call(
        _out_proj_kernel,
        grid=(s // OUT_TM,),
        in_specs=[
            pl.BlockSpec((OUT_TM, D_MODEL), row),
            pl.BlockSpec((D_MODEL, D_MODEL), lambda i: (0, 0)),
            pl.BlockSpec((OUT_TM, D_MODEL), row),
            pl.BlockSpec((1, D_MODEL), lambda i: (0, 0)),
        ],
        out_specs=pl.BlockSpec((OUT_TM, D_MODEL), row),
        out_shape=jax.ShapeDtypeStruct((s, D_MODEL), F32),
        compiler_params=pltpu.CompilerParams(
            dimension_semantics=("parallel",), vmem_limit_bytes=VMEM_LIMIT),
        name="out_proj",
    )(o, w, x, g.reshape(1, D_MODEL))


def kernel(x, ffn_norm, ffn_w_in, ffn_w_out, mix_norm, diff_w_qkv, diff_lambda, diff_subln,
           diff_w_o, kv_norm, kv_w, moba_w_q, moba_w_o):
    b, s, d = x.shape
    assert b == 1 and d == D_MODEL and s % ATT_T == 0 and s // MOBA_BLOCK <= HEAD_DIM
    assert ffn_w_in.shape[0] == 2 and diff_w_qkv.shape[0] == 1 and moba_w_q.shape[0] == 1
    tables = _rope_tables(s)
    w_in, w_out = ffn_w_in.astype(BF16), ffn_w_out.astype(BF16)

    def ffn(xx, layer, slot):
        return _ffn_half(xx, ffn_norm[layer, slot, 0], ffn_norm[layer, slot, 1],
                         w_in[layer, slot], w_out[layer, slot])

    xs = x.reshape(s, d)

    xs = ffn(xs, 0, 0)
    qt, k, vt = _project(xs, mix_norm[0, 0], diff_w_qkv[0].astype(BF16), tables, n_rope=2,
                         scale0=Q_SCALE, transposed=(True, False, True), with_kmean=False)
    o = _diff_attention(qt, k, vt, diff_lambda[0], diff_subln[0])
    xs = _out_proj(o, diff_w_o[0].astype(BF16), xs, mix_norm[0, 1])
    xs = ffn(xs, 0, 1)

    k, vt, kmean = _project(xs, kv_norm, kv_w.astype(BF16), tables, n_rope=1, scale0=1.0,
                            transposed=(False, True), with_kmean=True)
    kmean = jnp.pad(kmean.reshape(s // MOBA_BLOCK, d), ((0, PAIR - s // MOBA_BLOCK), (0, 0)))
    xs = ffn(xs, 1, 0)
    (qt,) = _project(xs, mix_norm[1, 0], moba_w_q[0].astype(BF16), tables, n_rope=1,
                     scale0=Q_SCALE, transposed=(True,), with_kmean=False)
    o = _moba_attention(qt, k, vt, kmean)
    xs = _out_proj(o, moba_w_o[0].astype(BF16), xs, mix_norm[1, 1])
    xs = ffn(xs, 1, 1)
    return xs.reshape(b, s, d)
```

```python
import functools
import math

import jax
import jax.numpy as jnp
from jax import lax
from jax.experimental import pallas as pl
from jax.experimental.pallas import tpu as pltpu

F32 = jnp.float32
BF16 = jnp.bfloat16

D_MODEL = 1024
HEAD_DIM = 64
PAIR = 2 * HEAD_DIM
N_PAIRS = D_MODEL // PAIR
D_FF = 2816
ROPE_THETA = 10000.0
MOBA_BLOCK = 256
MOBA_TOPK = 3
FFN_RESIDUAL_SCALE = 0.5
NORM_EPS = 1e-6
DIFF_LAMBDA_INIT = 0.2
MASK_NEG = -1e30
Q_SCALE = HEAD_DIM ** -0.5 * math.log2(math.e)

FFN_TM = 512
FFN_TF = D_FF // 2
ATT_T = 512
OUT_TM = 512
VMEM_LIMIT = 56 * 1024 * 1024


def _rms(x, g):
    return x * lax.rsqrt(jnp.mean(x * x, axis=-1, keepdims=True) + NORM_EPS) * g


def _ffn_kernel(x_ref, gpre_ref, gpost_ref, wg_ref, wu_ref, wo_ref, o_ref, h_ref, acc_ref):
    j = pl.program_id(1)

    @pl.when(j == 0)
    def _():
        h_ref[...] = _rms(x_ref[...], gpre_ref[...]).astype(BF16)

    h = h_ref[...]
    gate = jnp.dot(h, wg_ref[...], preferred_element_type=F32)
    up = jnp.dot(h, wu_ref[...], preferred_element_type=F32)
    act = (gate * (1.0 / (1.0 + jnp.exp(-gate))) * up).astype(BF16)
    part = jnp.dot(act, wo_ref[...], preferred_element_type=F32)

    @pl.when(j == 0)
    def _():
        acc_ref[...] = part

    @pl.when(j > 0)
    def _():
        acc_ref[...] += part

    @pl.when(j == pl.num_programs(1) - 1)
    def _():
        o_ref[...] = x_ref[...] + FFN_RESIDUAL_SCALE * _rms(acc_ref[...], gpost_ref[...])


def _ffn_half(x, g_pre, g_post, w_in, w_out):
    s = x.shape[0]
    nf = D_FF // FFN_TF
    return pl.pallas_call(
        _ffn_kernel,
        grid=(s // FFN_TM, nf),
        in_specs=[
            pl.BlockSpec((FFN_TM, D_MODEL), lambda i, j: (i, 0)),
            pl.BlockSpec((1, D_MODEL), lambda i, j: (0, 0)),
            pl.BlockSpec((1, D_MODEL), lambda i, j: (0, 0)),
            pl.BlockSpec((D_MODEL, FFN_TF), lambda i, j: (0, j)),
            pl.BlockSpec((D_MODEL, FFN_TF), lambda i, j: (0, j + nf)),
            pl.BlockSpec((FFN_TF, D_MODEL), lambda i, j: (j, 0)),
        ],
        out_specs=pl.BlockSpec((FFN_TM, D_MODEL), lambda i, j: (i, 0)),
        out_shape=jax.ShapeDtypeStruct((s, D_MODEL), F32),
        scratch_shapes=[pltpu.VMEM((FFN_TM, D_MODEL), BF16), pltpu.VMEM((FFN_TM, D_MODEL), F32)],
        compiler_params=pltpu.CompilerParams(
            dimension_semantics=("parallel", "arbitrary"), vmem_limit_bytes=VMEM_LIMIT),
        name="ffn_half",
    )(x, g_pre.reshape(1, D_MODEL), g_post.reshape(1, D_MODEL), w_in, w_in, w_out)


def _rope_tables(s):
    pos = jnp.arange(s, dtype=F32)
    inv_freq = 1.0 / (ROPE_THETA ** (jnp.arange(0, HEAD_DIM, 2, dtype=F32) / HEAD_DIM))
    ang = pos[:, None] * inv_freq[None, :]
    cos, sin, zero = jnp.cos(ang), jnp.sin(ang), jnp.zeros_like(ang)
    cos_t = jnp.concatenate([cos] * 4, axis=-1)
    sin_lo = jnp.concatenate([-sin, zero] * 2, axis=-1)
    sin_hi = jnp.concatenate([zero, sin] * 2, axis=-1)
    return cos_t, sin_lo, sin_hi


def _proj_kernel(x_ref, g_ref, w_ref, cos_ref, sinlo_ref, sinhi_ref, *out_refs,
                 n_rope, scale0, transposed, with_kmean):
    n_chunks = len(transposed)
    h = _rms(x_ref[...], g_ref[...]).astype(BF16)
    tm = h.shape[0]
    reps = D_MODEL // PAIR
    cos = jnp.concatenate([cos_ref[...]] * reps, axis=1)
    sin_lo = jnp.concatenate([sinlo_ref[...]] * reps, axis=1)
    sin_hi = jnp.concatenate([sinhi_ref[...]] * reps, axis=1)
    for c in range(n_chunks):
        y = jnp.dot(h, w_ref[:, c * D_MODEL:(c + 1) * D_MODEL], preferred_element_type=F32)
        if c < n_rope:
            half = HEAD_DIM // 2
            y = (y * cos + pltpu.roll(y, D_MODEL - half, 1) * sin_lo
                 + pltpu.roll(y, half, 1) * sin_hi)
            if c == 0 and with_kmean:
                km = jnp.mean(y.reshape(tm // MOBA_BLOCK, MOBA_BLOCK, D_MODEL), axis=1)
                out_refs[n_chunks][0] = km
        if c == 0 and scale0 != 1.0:
            y = y * scale0
        if transposed[c]:
            out_refs[c][...] = y.T.reshape(N_PAIRS, 1, PAIR, tm).astype(BF16)
        else:
            out_refs[c][...] = y.astype(BF16)


def _project(x, g, w, tables, *, n_rope, scale0, transposed, with_kmean):
    s = x.shape[0]
    tm = ATT_T
    row = lambda i: (i, 0)
    out_shape, out_specs = [], []
    for tr in transposed:
        if tr:
            out_shape.append(jax.ShapeDtypeStruct((N_PAIRS, s // tm, PAIR, tm), BF16))
            out_specs.append(pl.BlockSpec((N_PAIRS, 1, PAIR, tm), lambda i: (0, i, 0, 0)))
        else:
            out_shape.append(jax.ShapeDtypeStruct((s, D_MODEL), BF16))
            out_specs.append(pl.BlockSpec((tm, D_MODEL), row))
    if with_kmean:
        nb = tm // MOBA_BLOCK
        out_shape.append(jax.ShapeDtypeStruct((s // tm, nb, D_MODEL), F32))
        out_specs.append(pl.BlockSpec((1, nb, D_MODEL), lambda i: (i, 0, 0)))
    return pl.pallas_call(
        functools.partial(_proj_kernel, n_rope=n_rope, scale0=scale0, transposed=transposed,
                          with_kmean=with_kmean),
        grid=(s // tm,),
        in_specs=[
            pl.BlockSpec((tm, D_MODEL), row),
            pl.BlockSpec((1, D_MODEL), lambda i: (0, 0)),
            pl.BlockSpec(w.shape, lambda i: (0, 0)),
            pl.BlockSpec((tm, PAIR), row),
            pl.BlockSpec((tm, PAIR), row),
            pl.BlockSpec((tm, PAIR), row),
        ],
        out_specs=out_specs,
        out_shape=out_shape,
        compiler_params=pltpu.CompilerParams(
            dimension_semantics=("parallel",), vmem_limit_bytes=VMEM_LIMIT),
        name="project",
    )(x, g.reshape(1, D_MODEL), w, *tables)


def _stack_heads_t(qt):
    feat = lax.broadcasted_iota(jnp.int32, qt.shape, 0)
    zero = jnp.zeros_like(qt)
    return jnp.concatenate([jnp.where(feat < HEAD_DIM, qt, zero),
                            jnp.where(feat >= HEAD_DIM, qt, zero)], axis=1)


def _score_stage(st, s_ref, mx_ref, slot, masked):
    if masked:
        st = _causal_mask_t(st, st.shape[0])
    s_ref[slot] = st
    mx_ref[slot] = jnp.max(st, axis=0, keepdims=True)


def _softmax_stage(s_ref, mx_ref, slot, vt, m_ref, l_ref, acc_ref):
    m_prev = m_ref[...]
    m_new = jnp.maximum(m_prev, mx_ref[slot])
    alpha = jnp.exp2(m_prev - m_new)
    p = jnp.exp2(s_ref[slot] - m_new)
    l_ref[...] = alpha * l_ref[...] + jnp.sum(p, axis=0, keepdims=True)
    acc_ref[...] = alpha * acc_ref[...] + jnp.dot(vt, p.astype(BF16), preferred_element_type=F32)
    m_ref[...] = m_new


def _run_kv_blocks(i, score, consume):
    @pl.when(i == 0)
    def _():
        score(0, 0, True)
        consume(0, 0)

    @pl.when(i > 0)
    def _():
        score(0, 0, False)

        def pair(p, carry):
            b = 2 * p
            score(b + 1, 1, False)
            consume(b, 0)
            score(b + 2, 0, False)
            consume(b + 1, 1)
            return carry

        lax.fori_loop(0, lax.shift_right_logical(i - 1, 1), pair, 0)

        @pl.when(lax.rem(i, 2) == 1)
        def _():
            score(i, 1, True)
            consume(i - 1, 0)
            consume(i, 1)

        @pl.when(lax.rem(i, 2) == 0)
        def _():
            score(i - 1, 1, False)
            consume(i - 2, 0)
            score(i, 0, True)
            consume(i - 1, 1)
            consume(i, 0)


def _causal_mask_t(st, t):
    key = lax.broadcasted_iota(jnp.int32, st.shape, 0)
    col = lax.broadcasted_iota(jnp.int32, st.shape, 1)
    qpos = jnp.where(col >= t, col - t, col)
    return jnp.where(key <= qpos, st, -jnp.inf)


def _init_stats(m_ref, l_ref, acc_ref):
    m_ref[...] = jnp.full(m_ref.shape, -jnp.inf, F32)
    l_ref[...] = jnp.zeros(l_ref.shape, F32)
    acc_ref[...] = jnp.zeros(acc_ref.shape, F32)


def _diff_attn_kernel(qt_ref, k_ref, vt_ref, lam_ref, subln_ref, o_ref,
                      qs_ref, s_ref, mx_ref, m_ref, l_ref, acc_ref):
    t = ATT_T
    i = pl.program_id(1)
    qs_ref[...] = _stack_heads_t(qt_ref[...])
    _init_stats(m_ref, l_ref, acc_ref)

    def score(j, slot, masked):
        off = pl.multiple_of(j * t, t)
        st = jnp.dot(k_ref[pl.ds(off, t), :], qs_ref[...], preferred_element_type=F32)
        _score_stage(st, s_ref, mx_ref, slot, masked)

    def consume(j, slot):
        _softmax_stage(s_ref, mx_ref, slot, vt_ref[j], m_ref, l_ref, acc_ref)

    _run_kv_blocks(i, score, consume)

    lv = lam_ref[...]
    lam = (jnp.exp(jnp.sum(lv[0:1] * lv[1:2], axis=-1, keepdims=True))
           - jnp.exp(jnp.sum(lv[2:3] * lv[3:4], axis=-1, keepdims=True)) + DIFF_LAMBDA_INIT)
    ot = acc_ref[:, 0:t] / l_ref[:, 0:t] - lam * (acc_ref[:, t:] / l_ref[:, t:])
    o_ref[...] = (_rms(ot.T, subln_ref[...]) * (1.0 - DIFF_LAMBDA_INIT)).astype(BF16)


def _pipeline_scratch(t):
    return [pltpu.VMEM((2, t, 2 * t), F32), pltpu.VMEM((2, 1, 2 * t), F32),
            pltpu.VMEM((1, 2 * t), F32), pltpu.VMEM((1, 2 * t), F32),
            pltpu.VMEM((PAIR, 2 * t), F32)]


def _attn_specs(s, t):
    qt_spec = pl.BlockSpec((None, None, PAIR, t), lambda h, i: (h, i, 0, 0))
    k_spec = pl.BlockSpec((s, PAIR), lambda h, i: (0, h))
    vt_spec = pl.BlockSpec((None, s // t, PAIR, t), lambda h, i: (h, 0, 0, 0))
    o_spec = pl.BlockSpec((t, PAIR), lambda h, i: (i, h))
    return qt_spec, k_spec, vt_spec, o_spec


def _diff_attention(qt, k, vt, lam_vecs, subln):
    s = k.shape[0]
    t = ATT_T
    qt_spec, k_spec, vt_spec, o_spec = _attn_specs(s, t)
    return pl.pallas_call(
        _diff_attn_kernel,
        grid=(N_PAIRS, s // t),
        in_specs=[qt_spec, k_spec, vt_spec,
                  pl.BlockSpec((4, HEAD_DIM), lambda h, i: (0, 0)),
                  pl.BlockSpec((1, PAIR), lambda h, i: (0, 0))],
        out_specs=o_spec,
        out_shape=jax.ShapeDtypeStruct((s, D_MODEL), BF16),
        scratch_shapes=[pltpu.VMEM((PAIR, 2 * t), BF16)] + _pipeline_scratch(t),
        compiler_params=pltpu.CompilerParams(
            dimension_semantics=("parallel", "arbitrary"), vmem_limit_bytes=VMEM_LIMIT),
        name="diff_attn",
    )(qt, k, vt, lam_vecs, subln.reshape(1, PAIR))


def _moba_bias_t(gate, qblk):
    n = lax.broadcasted_iota(jnp.int32, gate.shape, 0)
    nf = n.astype(F32)
    past = n < qblk
    g = jnp.where(past, gate, -jnp.inf)
    picked = jnp.zeros(gate.shape, F32)
    for _ in range(MOBA_TOPK):
        mx = jnp.max(g, axis=0, keepdims=True)
        first = jnp.min(jnp.where(g == mx, nf, float(PAIR)), axis=0, keepdims=True)
        hit = nf == first
        picked = jnp.where(hit, 1.0, picked)
        g = jnp.where(hit, -jnp.inf, g)
    sel_past = jnp.where(past, jnp.where(picked > 0.0, 0.0, MASK_NEG), MASK_NEG)
    return jnp.where(n == qblk, 0.0, sel_past)


def _moba_attn_kernel(qt_ref, k_ref, vt_ref, kmean_ref, o_ref,
                      qa_ref, kaug_ref, s_ref, mx_ref, m_ref, l_ref, acc_ref):
    t = ATT_T
    i = pl.program_id(1)
    s_len = k_ref.shape[0]

    @pl.when(i == 0)
    def _():
        kaug_ref[:, 0:PAIR] = k_ref[...]
        blk = lax.broadcasted_iota(jnp.int32, (s_len, PAIR), 0) // MOBA_BLOCK
        lane = lax.broadcasted_iota(jnp.int32, (s_len, PAIR), 1)
        kaug_ref[:, PAIR:] = jnp.where(blk == lane, 1.0, 0.0).astype(BF16)

    qs = _stack_heads_t(qt_ref[...])
    gate = jnp.dot(kmean_ref[...], qs.astype(F32), preferred_element_type=F32,
                   precision=lax.Precision.HIGHEST)
    col = lax.broadcasted_iota(jnp.int32, (1, 2 * t), 1)
    qblk = (i * t + jnp.where(col >= t, col - t, col)) // MOBA_BLOCK
    qa_ref[0:PAIR, :] = qs
    qa_ref[PAIR:, :] = _moba_bias_t(gate, qblk).astype(BF16)
    _init_stats(m_ref, l_ref, acc_ref)

    def score(j, slot, masked):
        off = pl.multiple_of(j * t, t)
        st = jnp.dot(kaug_ref[pl.ds(off, t), :], qa_ref[...], preferred_element_type=F32)
        _score_stage(st, s_ref, mx_ref, slot, masked)

    def consume(j, slot):
        _softmax_stage(s_ref, mx_ref, slot, vt_ref[j], m_ref, l_ref, acc_ref)

    _run_kv_blocks(i, score, consume)

    feat = lax.broadcasted_iota(jnp.int32, (PAIR, t), 0)
    ot = jnp.where(feat < HEAD_DIM, acc_ref[:, 0:t] / l_ref[:, 0:t], acc_ref[:, t:] / l_ref[:, t:])
    o_ref[...] = ot.T.astype(BF16)


def _moba_attention(qt, k, vt, kmean):
    s = k.shape[0]
    t = ATT_T
    qt_spec, k_spec, vt_spec, o_spec = _attn_specs(s, t)
    return pl.pallas_call(
        _moba_attn_kernel,
        grid=(N_PAIRS, s // t),
        in_specs=[qt_spec, k_spec, vt_spec, pl.BlockSpec((PAIR, PAIR), lambda h, i: (0, h))],
        out_specs=o_spec,
        out_shape=jax.ShapeDtypeStruct((s, D_MODEL), BF16),
        scratch_shapes=[pltpu.VMEM((2 * PAIR, 2 * t), BF16), pltpu.VMEM((s, 2 * PAIR), BF16)]
        + _pipeline_scratch(t),
        compiler_params=pltpu.CompilerParams(
            dimension_semantics=("arbitrary", "arbitrary"), vmem_limit_bytes=VMEM_LIMIT),
        name="moba_attn",
    )(qt, k, vt, kmean)


def _out_proj_kernel(o_ref, w_ref, x_ref, g_ref, y_ref):
    y = jnp.dot(o_ref[...], w_ref[...], preferred_element_type=F32)
    y_ref[...] = x_ref[...] + _rms(y, g_ref[...])


def _out_proj(o, w, x, g):
    s = x.shape[0]
    row = lambda i: (i, 0)
    return pl.pallas_call(
        _out_proj_kernel,
        grid=(s // OUT_TM,),
        in_specs=[
            pl.BlockSpec((OUT_TM, D_MODEL), row),
            pl.BlockSpec((D_MODEL, D_MODEL), lambda i: (0, 0)),
            pl.BlockSpec((OUT_TM, D_MODEL), row),
            pl.BlockSpec((1, D_MODEL), lambda i: (0, 0)),
        ],
        out_specs=pl.BlockSpec((OUT_TM, D_MODEL), row),
        out_shape=jax.ShapeDtypeStruct((s, D_MODEL), F32),
        compiler_params=pltpu.CompilerParams(
            dimension_semantics=("parallel",), vmem_limit_bytes=VMEM_LIMIT),
        name="out_proj",
    )(o, w, x, g.reshape(1, D_MODEL))


def kernel(x, ffn_norm, ffn_w_in, ffn_w_out, mix_norm, diff_w_qkv, diff_lambda, diff_subln,
           diff_w_o, kv_norm, kv_w, moba_w_q, moba_w_o):
    b, s, d = x.shape
    assert b == 1 and d == D_MODEL and s % ATT_T == 0 and s // MOBA_BLOCK <= HEAD_DIM
    assert ffn_w_in.shape[0] == 2 and diff_w_qkv.shape[0] == 1 and moba_w_q.shape[0] == 1
    tables = _rope_tables(s)
    w_in, w_out = ffn_w_in.astype(BF16), ffn_w_out.astype(BF16)

    def ffn(xx, layer, slot):
        return _ffn_half(xx, ffn_norm[layer, slot, 0], ffn_norm[layer, slot, 1],
                         w_in[layer, slot], w_out[layer, slot])

    xs = x.reshape(s, d)

    xs = ffn(xs, 0, 0)
    qt, k, vt = _project(xs, mix_norm[0, 0], diff_w_qkv[0].astype(BF16), tables, n_rope=2,
                         scale0=Q_SCALE, transposed=(True, False, True), with_kmean=False)
    o = _diff_attention(qt, k, vt, diff_lambda[0], diff_subln[0])
    xs = _out_proj(o, diff_w_o[0].astype(BF16), xs, mix_norm[0, 1])
    xs = ffn(xs, 0, 1)

    k, vt, kmean = _project(xs, kv_norm, kv_w.astype(BF16), tables, n_rope=1, scale0=1.0,
                            transposed=(False, True), with_kmean=True)
    kmean = jnp.pad(kmean.reshape(s // MOBA_BLOCK, d), ((0, PAIR - s // MOBA_BLOCK), (0, 0)))
    xs = ffn(xs, 1, 0)
    (qt,) = _project(xs, mix_norm[1, 0], moba_w_q[0].astype(BF16), tables, n_rope=1,
                     scale0=Q_SCALE, transposed=(True,), with_kmean=False)
    o = _moba_attention(qt, k, vt, kmean)
    xs = _out_proj(o, moba_w_o[0].astype(BF16), xs, mix_norm[1, 1])
    xs = ffn(xs, 1, 1)
    return xs.reshape(b, s, d)
```

```python
import functools
import math

import jax
import jax.numpy as jnp
from jax import lax
from jax.experimental import pallas as pl
from jax.experimental.pallas import tpu as pltpu

F32 = jnp.float32
BF16 = jnp.bfloat16

D_MODEL = 1024
HEAD_DIM = 64
PAIR = 2 * HEAD_DIM
N_PAIRS = D_MODEL // PAIR
D_FF = 2816
ROPE_THETA = 10000.0
MOBA_BLOCK = 256
MOBA_TOPK = 3
FFN_RESIDUAL_SCALE = 0.5
NORM_EPS = 1e-6
DIFF_LAMBDA_INIT = 0.2
MASK_NEG = -1e30
Q_SCALE = HEAD_DIM ** -0.5 * math.log2(math.e)

FFN_TM = 512
FFN_TF = D_FF // 2
ATT_TK = 512
ATT_TQ = 2 * ATT_TK
OUT_TM = 512
VMEM_LIMIT = 56 * 1024 * 1024


def _rms(x, g):
    return x * lax.rsqrt(jnp.mean(x * x, axis=-1, keepdims=True) + NORM_EPS) * g


def _ffn_kernel(x_ref, gpre_ref, gpost_ref, wg_ref, wu_ref, wo_ref, o_ref, h_ref, acc_ref):
    j = pl.program_id(1)

    @pl.when(j == 0)
    def _():
        h_ref[...] = _rms(x_ref[...], gpre_ref[...]).astype(BF16)

    h = h_ref[...]
    gate = jnp.dot(h, wg_ref[...], preferred_element_type=F32)
    up = jnp.dot(h, wu_ref[...], preferred_element_type=F32)
    act = (gate * (1.0 / (1.0 + jnp.exp(-gate))) * up).astype(BF16)
    part = jnp.dot(act, wo_ref[...], preferred_element_type=F32)

    @pl.when(j == 0)
    def _():
        acc_ref[...] = part

    @pl.when(j > 0)
    def _():
        acc_ref[...] += part

    @pl.when(j == pl.num_programs(1) - 1)
    def _():
        o_ref[...] = x_ref[...] + FFN_RESIDUAL_SCALE * _rms(acc_ref[...], gpost_ref[...])


def _ffn_half(x, g_pre, g_post, w_in, w_out):
    s = x.shape[0]
    nf = D_FF // FFN_TF
    return pl.pallas_call(
        _ffn_kernel,
        grid=(s // FFN_TM, nf),
        in_specs=[
            pl.BlockSpec((FFN_TM, D_MODEL), lambda i, j: (i, 0)),
            pl.BlockSpec((1, D_MODEL), lambda i, j: (0, 0)),
            pl.BlockSpec((1, D_MODEL), lambda i, j: (0, 0)),
            pl.BlockSpec((D_MODEL, FFN_TF), lambda i, j: (0, j)),
            pl.BlockSpec((D_MODEL, FFN_TF), lambda i, j: (0, j + nf)),
            pl.BlockSpec((FFN_TF, D_MODEL), lambda i, j: (j, 0)),
        ],
        out_specs=pl.BlockSpec((FFN_TM, D_MODEL), lambda i, j: (i, 0)),
        out_shape=jax.ShapeDtypeStruct((s, D_MODEL), F32),
        scratch_shapes=[pltpu.VMEM((FFN_TM, D_MODEL), BF16), pltpu.VMEM((FFN_TM, D_MODEL), F32)],
        compiler_params=pltpu.CompilerParams(
            dimension_semantics=("parallel", "arbitrary"), vmem_limit_bytes=VMEM_LIMIT),
        name="ffn_half",
    )(x, g_pre.reshape(1, D_MODEL), g_post.reshape(1, D_MODEL), w_in, w_in, w_out)


def _rope_tables(s):
    pos = jnp.arange(s, dtype=F32)
    inv_freq = 1.0 / (ROPE_THETA ** (jnp.arange(0, HEAD_DIM, 2, dtype=F32) / HEAD_DIM))
    ang = pos[:, None] * inv_freq[None, :]
    cos, sin, zero = jnp.cos(ang), jnp.sin(ang), jnp.zeros_like(ang)
    cos_t = jnp.concatenate([cos] * 4, axis=-1)
    sin_lo = jnp.concatenate([-sin, zero] * 2, axis=-1)
    sin_hi = jnp.concatenate([zero, sin] * 2, axis=-1)
    return cos_t, sin_lo, sin_hi


def _proj_kernel(x_ref, g_ref, w_ref, cos_ref, sinlo_ref, sinhi_ref, *out_refs,
                 n_rope, scale0, transposed, with_kmean):
    n_chunks = len(transposed)
    h = _rms(x_ref[...], g_ref[...]).astype(BF16)
    tm = h.shape[0]
    reps = D_MODEL // PAIR
    cos = jnp.concatenate([cos_ref[...]] * reps, axis=1)
    sin_lo = jnp.concatenate([sinlo_ref[...]] * reps, axis=1)
    sin_hi = jnp.concatenate([sinhi_ref[...]] * reps, axis=1)
    for c in range(n_chunks):
        y = jnp.dot(h, w_ref[:, c * D_MODEL:(c + 1) * D_MODEL], preferred_element_type=F32)
        if c < n_rope:
            half = HEAD_DIM // 2
            y = (y * cos + pltpu.roll(y, D_MODEL - half, 1) * sin_lo
                 + pltpu.roll(y, half, 1) * sin_hi)
            if c == 0 and with_kmean:
                km = jnp.mean(y.reshape(tm // MOBA_BLOCK, MOBA_BLOCK, D_MODEL), axis=1)
                out_refs[n_chunks][0] = km
        if c == 0 and scale0 != 1.0:
            y = y * scale0
        if transposed[c]:
            out_refs[c][...] = y.T.reshape(N_PAIRS, 1, PAIR, tm).astype(BF16)
        else:
            out_refs[c][...] = y.astype(BF16)


def _project(x, g, w, tables, *, n_rope, scale0, transposed, with_kmean):
    s = x.shape[0]
    tm = ATT_TK
    row = lambda i: (i, 0)
    out_shape, out_specs = [], []
    for tr in transposed:
        if tr:
            out_shape.append(jax.ShapeDtypeStruct((N_PAIRS, s // tm, PAIR, tm), BF16))
            out_specs.append(pl.BlockSpec((N_PAIRS, 1, PAIR, tm), lambda i: (0, i, 0, 0)))
        else:
            out_shape.append(jax.ShapeDtypeStruct((s, D_MODEL), BF16))
            out_specs.append(pl.BlockSpec((tm, D_MODEL), row))
    if with_kmean:
        nb = tm // MOBA_BLOCK
        out_shape.append(jax.ShapeDtypeStruct((s // tm, nb, D_MODEL), F32))
        out_specs.append(pl.BlockSpec((1, nb, D_MODEL), lambda i: (i, 0, 0)))
    return pl.pallas_call(
        functools.partial(_proj_kernel, n_rope=n_rope, scale0=scale0, transposed=transposed,
                          with_kmean=with_kmean),
        grid=(s // tm,),
        in_specs=[
            pl.BlockSpec((tm, D_MODEL), row),
            pl.BlockSpec((1, D_MODEL), lambda i: (0, 0)),
            pl.BlockSpec(w.shape, lambda i: (0, 0)),
            pl.BlockSpec((tm, PAIR), row),
            pl.BlockSpec((tm, PAIR), row),
            pl.BlockSpec((tm, PAIR), row),
        ],
        out_specs=out_specs,
        out_shape=out_shape,
        compiler_params=pltpu.CompilerParams(
            dimension_semantics=("parallel",), vmem_limit_bytes=VMEM_LIMIT),
        name="project",
    )(x, g.reshape(1, D_MODEL), w, *tables)


def _stacked_queries_t(qt_ref):
    qt = jnp.concatenate([qt_ref[r] for r in range(qt_ref.shape[0])], axis=1)
    feat = lax.broadcasted_iota(jnp.int32, qt.shape, 0)
    zero = jnp.zeros_like(qt)
    return jnp.concatenate([jnp.where(feat < HEAD_DIM, qt, zero),
                            jnp.where(feat >= HEAD_DIM, qt, zero)], axis=1)


def _causal_mask_t(st, key_offset):
    tq = st.shape[1] // 2
    key = lax.broadcasted_iota(jnp.int32, st.shape, 0) + key_offset
    col = lax.broadcasted_iota(jnp.int32, st.shape, 1)
    qpos = jnp.where(col >= tq, col - tq, col)
    return jnp.where(key <= qpos, st, -jnp.inf)


def _score_stage(st, s_ref, mx_ref, slot, key_offset):
    if key_offset is not None:
        st = _causal_mask_t(st, key_offset)
    s_ref[slot] = st
    mx_ref[slot] = jnp.max(st, axis=0, keepdims=True)


def _softmax_stage(s_ref, mx_ref, slot, vt, m_ref, l_ref, acc_ref):
    m_prev = m_ref[...]
    m_new = jnp.maximum(m_prev, mx_ref[slot])
    alpha = jnp.exp2(m_prev - m_new)
    p = jnp.exp2(s_ref[slot] - m_new)
    l_ref[...] = alpha * l_ref[...] + jnp.sum(p, axis=0, keepdims=True)
    acc_ref[...] = alpha * acc_ref[...] + jnp.dot(vt, p.astype(BF16), preferred_element_type=F32)
    m_ref[...] = m_new


def _run_kv_blocks(i, score, consume):
    @pl.when(i == 0)
    def _():
        score(0, 0, 0)
        score(1, 1, ATT_TK)
        consume(0, 0)
        consume(1, 1)

    @pl.when(i > 0)
    def _():
        score(0, 0, None)

        def pair(p, carry):
            b = 2 * p
            score(b + 1, 1, None)
            consume(b, 0)
            score(b + 2, 0, None)
            consume(b + 1, 1)
            return carry

        lax.fori_loop(0, i - 1, pair, 0)
        b = 2 * i - 2
        score(b + 1, 1, None)
        consume(b, 0)
        score(b + 2, 0, 0)
        consume(b + 1, 1)
        score(b + 3, 1, ATT_TK)
        consume(b + 2, 0)
        consume(b + 3, 1)


def _init_stats(m_ref, l_ref, acc_ref):
    m_ref[...] = jnp.full(m_ref.shape, -jnp.inf, F32)
    l_ref[...] = jnp.zeros(l_ref.shape, F32)
    acc_ref[...] = jnp.zeros(acc_ref.shape, F32)


def _pipeline_scratch():
    cols = 2 * ATT_TQ
    return [pltpu.VMEM((2, ATT_TK, cols), F32), pltpu.VMEM((2, 1, cols), F32),
            pltpu.VMEM((1, cols), F32), pltpu.VMEM((1, cols), F32), pltpu.VMEM((PAIR, cols), F32)]


def _attn_specs(s):
    qt_spec = pl.BlockSpec((None, ATT_TQ // ATT_TK, PAIR, ATT_TK), lambda h, i: (h, i, 0, 0))
    k_spec = pl.BlockSpec((s, PAIR), lambda h, i: (0, h))
    vt_spec = pl.BlockSpec((None, s // ATT_TK, PAIR, ATT_TK), lambda h, i: (h, 0, 0, 0))
    o_spec = pl.BlockSpec((ATT_TQ, PAIR), lambda h, i: (i, h))
    return qt_spec, k_spec, vt_spec, o_spec


def _diff_attn_kernel(qt_ref, k_ref, vt_ref, lam_ref, subln_ref, o_ref,
                      qs_ref, s_ref, mx_ref, m_ref, l_ref, acc_ref):
    tq, tk = ATT_TQ, ATT_TK
    i = pl.program_id(1)
    qs_ref[...] = _stacked_queries_t(qt_ref)
    _init_stats(m_ref, l_ref, acc_ref)

    def score(j, slot, key_offset):
        off = pl.multiple_of(j * tk, tk)
        st = jnp.dot(k_ref[pl.ds(off, tk), :], qs_ref[...], preferred_element_type=F32)
        _score_stage(st, s_ref, mx_ref, slot, key_offset)

    def consume(j, slot):
        _softmax_stage(s_ref, mx_ref, slot, vt_ref[j], m_ref, l_ref, acc_ref)

    _run_kv_blocks(i, score, consume)

    lv = lam_ref[...]
    lam = (jnp.exp(jnp.sum(lv[0:1] * lv[1:2], axis=-1, keepdims=True))
           - jnp.exp(jnp.sum(lv[2:3] * lv[3:4], axis=-1, keepdims=True)) + DIFF_LAMBDA_INIT)
    ot = acc_ref[:, 0:tq] / l_ref[:, 0:tq] - lam * (acc_ref[:, tq:] / l_ref[:, tq:])
    o_ref[...] = (_rms(ot.T, subln_ref[...]) * (1.0 - DIFF_LAMBDA_INIT)).astype(BF16)


def _diff_attention(qt, k, vt, lam_vecs, subln):
    s = k.shape[0]
    qt_spec, k_spec, vt_spec, o_spec = _attn_specs(s)
    return pl.pallas_call(
        _diff_attn_kernel,
        grid=(N_PAIRS, s // ATT_TQ),
        in_specs=[qt_spec, k_spec, vt_spec,
                  pl.BlockSpec((4, HEAD_DIM), lambda h, i: (0, 0)),
                  pl.BlockSpec((1, PAIR), lambda h, i: (0, 0))],
        out_specs=o_spec,
        out_shape=jax.ShapeDtypeStruct((s, D_MODEL), BF16),
        scratch_shapes=[pltpu.VMEM((PAIR, 2 * ATT_TQ), BF16)] + _pipeline_scratch(),
        compiler_params=pltpu.CompilerParams(
            dimension_semantics=("parallel", "arbitrary"), vmem_limit_bytes=VMEM_LIMIT),
        name="diff_attn",
    )(qt, k, vt, lam_vecs, subln.reshape(1, PAIR))


def _moba_bias_t(gate, qblk):
    n = lax.broadcasted_iota(jnp.int32, gate.shape, 0)
    nf = n.astype(F32)
    past = n < qblk
    g = jnp.where(past, gate, -jnp.inf)
    picked = jnp.zeros(gate.shape, F32)
    for _ in range(MOBA_TOPK):
        mx = jnp.max(g, axis=0, keepdims=True)
        first = jnp.min(jnp.where(g == mx, nf, float(PAIR)), axis=0, keepdims=True)
        hit = nf == first
        picked = jnp.where(hit, 1.0, picked)
        g = jnp.where(hit, -jnp.inf, g)
    sel_past = jnp.where(past, jnp.where(picked > 0.0, 0.0, MASK_NEG), MASK_NEG)
    return jnp.where(n == qblk, 0.0, sel_past)


def _moba_attn_kernel(qt_ref, k_ref, vt_ref, kmean_ref, o_ref,
                      qa_ref, kaug_ref, s_ref, mx_ref, m_ref, l_ref, acc_ref):
    tq, tk = ATT_TQ, ATT_TK
    i = pl.program_id(1)
    s_len = k_ref.shape[0]

    @pl.when(i == 0)
    def _():
        kaug_ref[:, 0:PAIR] = k_ref[...]
        blk = lax.broadcasted_iota(jnp.int32, (s_len, PAIR), 0) // MOBA_BLOCK
        lane = lax.broadcasted_iota(jnp.int32, (s_len, PAIR), 1)
        kaug_ref[:, PAIR:] = jnp.where(blk == lane, 1.0, 0.0).astype(BF16)

    qs = _stacked_queries_t(qt_ref)
    gate = jnp.dot(kmean_ref[...], qs.astype(F32), preferred_element_type=F32,
                   precision=lax.Precision.HIGHEST)
    col = lax.broadcasted_iota(jnp.int32, (1, 2 * tq), 1)
    qblk = (i * tq + jnp.where(col >= tq, col - tq, col)) // MOBA_BLOCK
    qa_ref[0:PAIR, :] = qs
    qa_ref[PAIR:, :] = _moba_bias_t(gate, qblk).astype(BF16)
    _init_stats(m_ref, l_ref, acc_ref)

    def score(j, slot, key_offset):
        off = pl.multiple_of(j * tk, tk)
        st = jnp.dot(kaug_ref[pl.ds(off, tk), :], qa_ref[...], preferred_element_type=F32)
        _score_stage(st, s_ref, mx_ref, slot, key_offset)

    def consume(j, slot):
        _softmax_stage(s_ref, mx_ref, slot, vt_ref[j], m_ref, l_ref, acc_ref)

    _run_kv_blocks(i, score, consume)

    feat = lax.broadcasted_iota(jnp.int32, (PAIR, tq), 0)
    ot = jnp.where(feat < HEAD_DIM, acc_ref[:, 0:tq] / l_ref[:, 0:tq],
                   acc_ref[:, tq:] / l_ref[:, tq:])
    o_ref[...] = ot.T.astype(BF16)


def _moba_attention(qt, k, vt, kmean):
    s = k.shape[0]
    qt_spec, k_spec, vt_spec, o_spec = _attn_specs(s)
    return pl.pallas_call(
        _moba_attn_kernel,
        grid=(N_PAIRS, s // ATT_TQ),
        in_specs=[qt_spec, k_spec, vt_spec, pl.BlockSpec((PAIR, PAIR), lambda h, i: (0, h))],
        out_specs=o_spec,
        out_shape=jax.ShapeDtypeStruct((s, D_MODEL), BF16),
        scratch_shapes=[pltpu.VMEM((2 * PAIR, 2 * ATT_TQ), BF16),
                        pltpu.VMEM((s, 2 * PAIR), BF16)] + _pipeline_scratch(),
        compiler_params=pltpu.CompilerParams(
            dimension_semantics=("arbitrary", "arbitrary"), vmem_limit_bytes=VMEM_LIMIT),
        name="moba_attn",
    )(qt, k, vt, kmean)


def _out_proj_kernel(o_ref, w_ref, x_ref, g_ref, y_ref):
    y = jnp.dot(o_ref[...], w_ref[...], preferred_element_type=F32)
    y_ref[...] = x_ref[...] + _rms(y, g_ref[...])


def _out_proj(o, w, x, g):
    s = x.shape[0]
    row = lambda i: (i, 0)
    return pl.pallas_call(
        _out_proj_kernel,
        grid=(s // OUT_TM,),
        in_specs=[
            pl.BlockSpec((OUT_TM, D_MODEL), row),
            pl.BlockSpec((D_MODEL, D_MODEL), lambda i: (0, 0)),
            pl.BlockSpec((OUT_TM, D_MODEL), row),
            pl.BlockSpec((1, D_MODEL), lambda i: (0, 0)),
        ],
        out_specs=pl.BlockSpec((OUT_TM, D_MODEL), row),
        out_shape=jax.ShapeDtypeStruct((s, D_MODEL), F32),
        compiler_params=pltpu.CompilerParams(
            dimension_semantics=("parallel",), vmem_limit_bytes=VMEM_LIMIT),
        name="out_proj",
    )(o, w, x, g.reshape(1, D_MODEL))


def kernel(x, ffn_norm, ffn_w_in, ffn_w_out, mix_norm, diff_w_qkv, diff_lambda, diff_subln,
           diff_w_o, kv_norm, kv_w, moba_w_q, moba_w_o):
    b, s, d = x.shape
    assert b == 1 and d == D_MODEL and s % ATT_TQ == 0 and s // MOBA_BLOCK <= HEAD_DIM
    assert ffn_w_in.shape[0] == 2 and diff_w_qkv.shape[0] == 1 and moba_w_q.shape[0] == 1
    tables = _rope_tables(s)
    w_in, w_out = ffn_w_in.astype(BF16), ffn_w_out.astype(BF16)

    def ffn(xx, layer, slot):
        return _ffn_half(xx, ffn_norm[layer, slot, 0], ffn_norm[layer, slot, 1],
                         w_in[layer, slot], w_out[layer, slot])

    xs = x.reshape(s, d)

    xs = ffn(xs, 0, 0)
    qt, k, vt = _project(xs, mix_norm[0, 0], diff_w_qkv[0].astype(BF16), tables, n_rope=2,
                         scale0=Q_SCALE, transposed=(True, False, True), with_kmean=False)
    o = _diff_attention(qt, k, vt, diff_lambda[0], diff_subln[0])
    xs = _out_proj(o, diff_w_o[0].astype(BF16), xs, mix_norm[0, 1])
    xs = ffn(xs, 0, 1)

    k, vt, kmean = _project(xs, kv_norm, kv_w.astype(BF16), tables, n_rope=1, scale0=1.0,
                            transposed=(False, True), with_kmean=True)
    kmean = jnp.pad(kmean.reshape(s // MOBA_BLOCK, d), ((0, PAIR - s // MOBA_BLOCK), (0, 0)))
    xs = ffn(xs, 1, 0)
    (qt,) = _project(xs, mix_norm[1, 0], moba_w_q[0].astype(BF16), tables, n_rope=1,
                     scale0=Q_SCALE, transposed=(True,), with_kmean=False)
    o = _moba_attention(qt, k, vt, kmean)
    xs = _out_proj(o, moba_w_o[0].astype(BF16), xs, mix_norm[1, 1])
    xs = ffn(xs, 1, 1)
    return xs.reshape(b, s, d)
```

```python
import functools
import math

import jax
import jax.numpy as jnp
from jax import lax
from jax.experimental import pallas as pl
from jax.experimental.pallas import tpu as pltpu

F32 = jnp.float32
BF16 = jnp.bfloat16

D_MODEL = 1024
HEAD_DIM = 64
PAIR = 2 * HEAD_DIM
N_PAIRS = D_MODEL // PAIR
D_FF = 2816
ROPE_THETA = 10000.0
MOBA_BLOCK = 256
MOBA_TOPK = 3
FFN_RESIDUAL_SCALE = 0.5
NORM_EPS = 1e-6
DIFF_LAMBDA_INIT = 0.2
MASK_NEG = -1e30
Q_SCALE = HEAD_DIM ** -0.5 * math.log2(math.e)

FFN_TM = 512
FFN_TF = D_FF // 2
ATT_TK = 512
ATT_TQ = 2 * ATT_TK
SUM_ROWS = 16
OUT_TM = 512
VMEM_LIMIT = 56 * 1024 * 1024


def _rms(x, g):
    return x * lax.rsqrt(jnp.mean(x * x, axis=-1, keepdims=True) + NORM_EPS) * g


def _ffn_kernel(x_ref, gpre_ref, gpost_ref, wg_ref, wu_ref, wo_ref, o_ref, h_ref, acc_ref):
    j = pl.program_id(1)

    @pl.when(j == 0)
    def _():
        h_ref[...] = _rms(x_ref[...], gpre_ref[...]).astype(BF16)

    h = h_ref[...]
    gate = jnp.dot(h, wg_ref[...], preferred_element_type=F32)
    up = jnp.dot(h, wu_ref[...], preferred_element_type=F32)
    act = (gate * (1.0 / (1.0 + jnp.exp(-gate))) * up).astype(BF16)
    part = jnp.dot(act, wo_ref[...], preferred_element_type=F32)

    @pl.when(j == 0)
    def _():
        acc_ref[...] = part

    @pl.when(j > 0)
    def _():
        acc_ref[...] += part

    @pl.when(j == pl.num_programs(1) - 1)
    def _():
        o_ref[...] = x_ref[...] + FFN_RESIDUAL_SCALE * _rms(acc_ref[...], gpost_ref[...])


def _ffn_half(x, g_pre, g_post, w_in, w_out):
    s = x.shape[0]
    nf = D_FF // FFN_TF
    return pl.pallas_call(
        _ffn_kernel,
        grid=(s // FFN_TM, nf),
        in_specs=[
            pl.BlockSpec((FFN_TM, D_MODEL), lambda i, j: (i, 0)),
            pl.BlockSpec((1, D_MODEL), lambda i, j: (0, 0)),
            pl.BlockSpec((1, D_MODEL), lambda i, j: (0, 0)),
            pl.BlockSpec((D_MODEL, FFN_TF), lambda i, j: (0, j)),
            pl.BlockSpec((D_MODEL, FFN_TF), lambda i, j: (0, j + nf)),
            pl.BlockSpec((FFN_TF, D_MODEL), lambda i, j: (j, 0)),
        ],
        out_specs=pl.BlockSpec((FFN_TM, D_MODEL), lambda i, j: (i, 0)),
        out_shape=jax.ShapeDtypeStruct((s, D_MODEL), F32),
        scratch_shapes=[pltpu.VMEM((FFN_TM, D_MODEL), BF16), pltpu.VMEM((FFN_TM, D_MODEL), F32)],
        compiler_params=pltpu.CompilerParams(
            dimension_semantics=("parallel", "arbitrary"), vmem_limit_bytes=VMEM_LIMIT),
        name="ffn_half",
    )(x, g_pre.reshape(1, D_MODEL), g_post.reshape(1, D_MODEL), w_in, w_in, w_out)


def _rope_tables(s):
    pos = jnp.arange(s, dtype=F32)
    inv_freq = 1.0 / (ROPE_THETA ** (jnp.arange(0, HEAD_DIM, 2, dtype=F32) / HEAD_DIM))
    ang = pos[:, None] * inv_freq[None, :]
    cos, sin, zero = jnp.cos(ang), jnp.sin(ang), jnp.zeros_like(ang)
    cos_t = jnp.concatenate([cos] * 4, axis=-1)
    sin_lo = jnp.concatenate([-sin, zero] * 2, axis=-1)
    sin_hi = jnp.concatenate([zero, sin] * 2, axis=-1)
    return cos_t, sin_lo, sin_hi


def _proj_kernel(x_ref, g_ref, w_ref, cos_ref, sinlo_ref, sinhi_ref, *out_refs,
                 n_rope, scale0, transposed, with_kmean):
    n_chunks = len(transposed)
    h = _rms(x_ref[...], g_ref[...]).astype(BF16)
    tm = h.shape[0]
    reps = D_MODEL // PAIR
    cos = jnp.concatenate([cos_ref[...]] * reps, axis=1)
    sin_lo = jnp.concatenate([sinlo_ref[...]] * reps, axis=1)
    sin_hi = jnp.concatenate([sinhi_ref[...]] * reps, axis=1)
    for c in range(n_chunks):
        y = jnp.dot(h, w_ref[:, c * D_MODEL:(c + 1) * D_MODEL], preferred_element_type=F32)
        if c < n_rope:
            half = HEAD_DIM // 2
            y = (y * cos + pltpu.roll(y, D_MODEL - half, 1) * sin_lo
                 + pltpu.roll(y, half, 1) * sin_hi)
            if c == 0 and with_kmean:
                km = jnp.mean(y.reshape(tm // MOBA_BLOCK, MOBA_BLOCK, D_MODEL), axis=1)
                out_refs[n_chunks][0] = km
        if c == 0 and scale0 != 1.0:
            y = y * scale0
        if transposed[c]:
            out_refs[c][...] = y.T.reshape(N_PAIRS, 1, PAIR, tm).astype(BF16)
        else:
            out_refs[c][...] = y.astype(BF16)


def _project(x, g, w, tables, *, n_rope, scale0, transposed, with_kmean):
    s = x.shape[0]
    tm = ATT_TK
    row = lambda i: (i, 0)
    out_shape, out_specs = [], []
    for tr in transposed:
        if tr:
            out_shape.append(jax.ShapeDtypeStruct((N_PAIRS, s // tm, PAIR, tm), BF16))
            out_specs.append(pl.BlockSpec((N_PAIRS, 1, PAIR, tm), lambda i: (0, i, 0, 0)))
        else:
            out_shape.append(jax.ShapeDtypeStruct((s, D_MODEL), BF16))
            out_specs.append(pl.BlockSpec((tm, D_MODEL), row))
    if with_kmean:
        nb = tm // MOBA_BLOCK
        out_shape.append(jax.ShapeDtypeStruct((s // tm, nb, D_MODEL), F32))
        out_specs.append(pl.BlockSpec((1, nb, D_MODEL), lambda i: (i, 0, 0)))
    return pl.pallas_call(
        functools.partial(_proj_kernel, n_rope=n_rope, scale0=scale0, transposed=transposed,
                          with_kmean=with_kmean),
        grid=(s // tm,),
        in_specs=[
            pl.BlockSpec((tm, D_MODEL), row),
            pl.BlockSpec((1, D_MODEL), lambda i: (0, 0)),
            pl.BlockSpec(w.shape, lambda i: (0, 0)),
            pl.BlockSpec((tm, PAIR), row),
            pl.BlockSpec((tm, PAIR), row),
            pl.BlockSpec((tm, PAIR), row),
        ],
        out_specs=out_specs,
        out_shape=out_shape,
        compiler_params=pltpu.CompilerParams(
            dimension_semantics=("parallel",), vmem_limit_bytes=VMEM_LIMIT),
        name="project",
    )(x, g.reshape(1, D_MODEL), w, *tables)


def _stacked_queries_t(qt_ref):
    qt = jnp.concatenate([qt_ref[r] for r in range(qt_ref.shape[0])], axis=1)
    feat = lax.broadcasted_iota(jnp.int32, qt.shape, 0)
    zero = jnp.zeros_like(qt)
    return jnp.concatenate([jnp.where(feat < HEAD_DIM, qt, zero),
                            jnp.where(feat >= HEAD_DIM, qt, zero)], axis=1)


def _causal_mask_t(st, key_offset):
    tq = st.shape[1] // 2
    key = lax.broadcasted_iota(jnp.int32, st.shape, 0) + key_offset
    col = lax.broadcasted_iota(jnp.int32, st.shape, 1)
    qpos = jnp.where(col >= tq, col - tq, col)
    return jnp.where(key <= qpos, st, -jnp.inf)


def _score_stage(st, s_ref, mx_ref, slot, key_offset):
    if key_offset is not None:
        st = _causal_mask_t(st, key_offset)
    s_ref[slot] = st
    mx_ref[slot] = jnp.max(st, axis=0, keepdims=True)


def _softmax_stage(s_ref, mx_ref, slot, vt, m_ref, acc_ref):
    m_prev = m_ref[...]
    m_new = jnp.maximum(m_prev, mx_ref[slot])
    alpha = jnp.exp2(m_prev - m_new)
    p = jnp.exp2(s_ref[slot] - m_new).astype(BF16)
    vt_sum = jnp.concatenate([vt, jnp.ones((SUM_ROWS, vt.shape[1]), BF16)], axis=0)
    acc_ref[...] = alpha * acc_ref[...] + jnp.dot(vt_sum, p, preferred_element_type=F32)
    m_ref[...] = m_new


def _run_kv_blocks(i, score, consume):
    @pl.when(i == 0)
    def _():
        score(0, 0, 0)
        score(1, 1, ATT_TK)
        consume(0, 0)
        consume(1, 1)

    @pl.when(i > 0)
    def _():
        score(0, 0, None)

        def pair(p, carry):
            b = 2 * p
            score(b + 1, 1, None)
            consume(b, 0)
            score(b + 2, 0, None)
            consume(b + 1, 1)
            return carry

        lax.fori_loop(0, i - 1, pair, 0)
        b = 2 * i - 2
        score(b + 1, 1, None)
        consume(b, 0)
        score(b + 2, 0, 0)
        consume(b + 1, 1)
        score(b + 3, 1, ATT_TK)
        consume(b + 2, 0)
        consume(b + 3, 1)


def _init_stats(m_ref, acc_ref):
    m_ref[...] = jnp.full(m_ref.shape, -jnp.inf, F32)
    acc_ref[...] = jnp.zeros(acc_ref.shape, F32)


def _normalized(acc_ref, lo, hi):
    return acc_ref[0:PAIR, lo:hi] / acc_ref[PAIR:PAIR + 1, lo:hi]


def _pipeline_scratch():
    cols = 2 * ATT_TQ
    return [pltpu.VMEM((2, ATT_TK, cols), F32), pltpu.VMEM((2, 1, cols), F32),
            pltpu.VMEM((1, cols), F32), pltpu.VMEM((PAIR + SUM_ROWS, cols), F32)]


def _attn_specs(s):
    qt_spec = pl.BlockSpec((None, ATT_TQ // ATT_TK, PAIR, ATT_TK), lambda h, i: (h, i, 0, 0))
    k_spec = pl.BlockSpec((s, PAIR), lambda h, i: (0, h))
    vt_spec = pl.BlockSpec((None, s // ATT_TK, PAIR, ATT_TK), lambda h, i: (h, 0, 0, 0))
    o_spec = pl.BlockSpec((ATT_TQ, PAIR), lambda h, i: (i, h))
    return qt_spec, k_spec, vt_spec, o_spec


def _diff_attn_kernel(qt_ref, k_ref, vt_ref, lam_ref, subln_ref, o_ref,
                      qs_ref, s_ref, mx_ref, m_ref, acc_ref):
    tq, tk = ATT_TQ, ATT_TK
    i = pl.program_id(1)
    qs_ref[...] = _stacked_queries_t(qt_ref)
    _init_stats(m_ref, acc_ref)

    def score(j, slot, key_offset):
        off = pl.multiple_of(j * tk, tk)
        st = jnp.dot(k_ref[pl.ds(off, tk), :], qs_ref[...], preferred_element_type=F32)
        _score_stage(st, s_ref, mx_ref, slot, key_offset)

    def consume(j, slot):
        _softmax_stage(s_ref, mx_ref, slot, vt_ref[j], m_ref, acc_ref)

    _run_kv_blocks(i, score, consume)

    lv = lam_ref[...]
    lam = (jnp.exp(jnp.sum(lv[0:1] * lv[1:2], axis=-1, keepdims=True))
           - jnp.exp(jnp.sum(lv[2:3] * lv[3:4], axis=-1, keepdims=True)) + DIFF_LAMBDA_INIT)
    ot = _normalized(acc_ref, 0, tq) - lam * _normalized(acc_ref, tq, 2 * tq)
    o_ref[...] = (_rms(ot.T, subln_ref[...]) * (1.0 - DIFF_LAMBDA_INIT)).astype(BF16)


def _diff_attention(qt, k, vt, lam_vecs, subln):
    s = k.shape[0]
    qt_spec, k_spec, vt_spec, o_spec = _attn_specs(s)
    return pl.pallas_call(
        _diff_attn_kernel,
        grid=(N_PAIRS, s // ATT_TQ),
        in_specs=[qt_spec, k_spec, vt_spec,
                  pl.BlockSpec((4, HEAD_DIM), lambda h, i: (0, 0)),
                  pl.BlockSpec((1, PAIR), lambda h, i: (0, 0))],
        out_specs=o_spec,
        out_shape=jax.ShapeDtypeStruct((s, D_MODEL), BF16),
        scratch_shapes=[pltpu.VMEM((PAIR, 2 * ATT_TQ), BF16)] + _pipeline_scratch(),
        compiler_params=pltpu.CompilerParams(
            dimension_semantics=("parallel", "arbitrary"), vmem_limit_bytes=VMEM_LIMIT),
        name="diff_attn",
    )(qt, k, vt, lam_vecs, subln.reshape(1, PAIR))


def _moba_bias_t(gate, qblk):
    n = lax.broadcasted_iota(jnp.int32, gate.shape, 0)
    nf = n.astype(F32)
    past = n < qblk
    g = jnp.where(past, gate, -jnp.inf)
    picked = jnp.zeros(gate.shape, F32)
    for _ in range(MOBA_TOPK):
        mx = jnp.max(g, axis=0, keepdims=True)
        first = jnp.min(jnp.where(g == mx, nf, float(PAIR)), axis=0, keepdims=True)
        hit = nf == first
        picked = jnp.where(hit, 1.0, picked)
        g = jnp.where(hit, -jnp.inf, g)
    sel_past = jnp.where(past, jnp.where(picked > 0.0, 0.0, MASK_NEG), MASK_NEG)
    return jnp.where(n == qblk, 0.0, sel_past)


def _moba_attn_kernel(qt_ref, k_ref, vt_ref, kmean_ref, o_ref,
                      qa_ref, kaug_ref, s_ref, mx_ref, m_ref, acc_ref):
    tq, tk = ATT_TQ, ATT_TK
    i = pl.program_id(1)
    s_len = k_ref.shape[0]

    @pl.when(i == 0)
    def _():
        kaug_ref[:, 0:PAIR] = k_ref[...]
        blk = lax.broadcasted_iota(jnp.int32, (s_len, PAIR), 0) // MOBA_BLOCK
        lane = lax.broadcasted_iota(jnp.int32, (s_len, PAIR), 1)
        kaug_ref[:, PAIR:] = jnp.where(blk == lane, 1.0, 0.0).astype(BF16)

    qs = _stacked_queries_t(qt_ref)
    gate = jnp.dot(kmean_ref[...], qs.astype(F32), preferred_element_type=F32,
                   precision=lax.Precision.HIGHEST)
    col = lax.broadcasted_iota(jnp.int32, (1, 2 * tq), 1)
    qblk = (i * tq + jnp.where(col >= tq, col - tq, col)) // MOBA_BLOCK
    qa_ref[0:PAIR, :] = qs
    qa_ref[PAIR:, :] = _moba_bias_t(gate, qblk).astype(BF16)
    _init_stats(m_ref, acc_ref)

    def score(j, slot, key_offset):
        off = pl.multiple_of(j * tk, tk)
        st = jnp.dot(kaug_ref[pl.ds(off, tk), :], qa_ref[...], preferred_element_type=F32)
        _score_stage(st, s_ref, mx_ref, slot, key_offset)

    def consume(j, slot):
        _softmax_stage(s_ref, mx_ref, slot, vt_ref[j], m_ref, acc_ref)

    _run_kv_blocks(i, score, consume)

    feat = lax.broadcasted_iota(jnp.int32, (PAIR, tq), 0)
    ot = jnp.where(feat < HEAD_DIM, _normalized(acc_ref, 0, tq), _normalized(acc_ref, tq, 2 * tq))
    o_ref[...] = ot.T.astype(BF16)


def _moba_attention(qt, k, vt, kmean):
    s = k.shape[0]
    qt_spec, k_spec, vt_spec, o_spec = _attn_specs(s)
    return pl.pallas_call(
        _moba_attn_kernel,
        grid=(N_PAIRS, s // ATT_TQ),
        in_specs=[qt_spec, k_spec, vt_spec, pl.BlockSpec((PAIR, PAIR), lambda h, i: (0, h))],
        out_specs=o_spec,
        out_shape=jax.ShapeDtypeStruct((s, D_MODEL), BF16),
        scratch_shapes=[pltpu.VMEM((2 * PAIR, 2 * ATT_TQ), BF16),
                        pltpu.VMEM((s, 2 * PAIR), BF16)] + _pipeline_scratch(),
        compiler_params=pltpu.CompilerParams(
            dimension_semantics=("arbitrary", "arbitrary"), vmem_limit_bytes=VMEM_LIMIT),
        name="moba_attn",
    )(qt, k, vt, kmean)


def _out_proj_kernel(o_ref, w_ref, x_ref, g_ref, y_ref):
    y = jnp.dot(o_ref[...], w_ref[...], preferred_element_type=F32)
    y_ref[...] = x_ref[...] + _rms(y, g_ref[...])


def _out_proj(o, w, x, g):
    s = x.shape[0]
    row = lambda i: (i, 0)
    return pl.pallas_call(
        _out_proj_kernel,
        grid=(s // OUT_TM,),
        in_specs=[
            pl.BlockSpec((OUT_TM, D_MODEL), row),
            pl.BlockSpec((D_MODEL, D_MODEL), lambda i: (0, 0)),
            pl.BlockSpec((OUT_TM, D_MODEL), row),
            pl.BlockSpec((1, D_MODEL), lambda i: (0, 0)),
        ],
        out_specs=pl.BlockSpec((OUT_TM, D_MODEL), row),
        out_shape=jax.ShapeDtypeStruct((s, D_MODEL), F32),
        compiler_params=pltpu.CompilerParams(
            dimension_semantics=("parallel",), vmem_limit_bytes=VMEM_LIMIT),
        name="out_proj",
    )(o, w, x, g.reshape(1, D_MODEL))


def kernel(x, ffn_norm, ffn_w_in, ffn_w_out, mix_norm, diff_w_qkv, diff_lambda, diff_subln,
           diff_w_o, kv_norm, kv_w, moba_w_q, moba_w_o):
    b, s, d = x.shape
    assert b == 1 and d == D_MODEL and s % ATT_TQ == 0 and s // MOBA_BLOCK <= HEAD_DIM
    assert ffn_w_in.shape[0] == 2 and diff_w_qkv.shape[0] == 1 and moba_w_q.shape[0] == 1
    tables = _rope_tables(s)
    w_in, w_out = ffn_w_in.astype(BF16), ffn_w_out.astype(BF16)

    def ffn(xx, layer, slot):
        return _ffn_half(xx, ffn_norm[layer, slot, 0], ffn_norm[layer, slot, 1],
                         w_in[layer, slot], w_out[layer, slot])

    xs = x.reshape(s, d)

    xs = ffn(xs, 0, 0)
    qt, k, vt = _project(xs, mix_norm[0, 0], diff_w_qkv[0].astype(BF16), tables, n_rope=2,
                         scale0=Q_SCALE, transposed=(True, False, True), with_kmean=False)
    o = _diff_attention(qt, k, vt, diff_lambda[0], diff_subln[0])
    xs = _out_proj(o, diff_w_o[0].astype(BF16), xs, mix_norm[0, 1])
    xs = ffn(xs, 0, 1)

    k, vt, kmean = _project(xs, kv_norm, kv_w.astype(BF16), tables, n_rope=1, scale0=1.0,
                            transposed=(False, True), with_kmean=True)
    kmean = jnp.pad(kmean.reshape(s // MOBA_BLOCK, d), ((0, PAIR - s // MOBA_BLOCK), (0, 0)))
    xs = ffn(xs, 1, 0)
    (qt,) = _project(xs, mix_norm[1, 0], moba_w_q[0].astype(BF16), tables, n_rope=1,
                     scale0=Q_SCALE, transposed=(True,), with_kmean=False)
    o = _moba_attention(qt, k, vt, kmean)
    xs = _out_proj(o, moba_w_o[0].astype(BF16), xs, mix_norm[1, 1])
    xs = ffn(xs, 1, 1)
    return xs.reshape(b, s, d)
```

```python
import functools
import math

import jax
import jax.numpy as jnp
from jax import lax
from jax.experimental import pallas as pl
from jax.experimental.pallas import tpu as pltpu

F32 = jnp.float32
BF16 = jnp.bfloat16

D_MODEL = 1024
HEAD_DIM = 64
PAIR = 2 * HEAD_DIM
N_PAIRS = D_MODEL // PAIR
D_FF = 2816
ROPE_THETA = 10000.0
MOBA_BLOCK = 256
MOBA_TOPK = 3
FFN_RESIDUAL_SCALE = 0.5
NORM_EPS = 1e-6
DIFF_LAMBDA_INIT = 0.2
MASK_NEG = -1e30
Q_SCALE = HEAD_DIM ** -0.5 * math.log2(math.e)

FFN_TM = 512
FFN_TF = D_FF // 2
ATT_TK = 512
ATT_TQ = 2 * ATT_TK
SUM_ROWS = 16
OUT_TM = 512
VMEM_LIMIT = 56 * 1024 * 1024


def _rms(x, g):
    return x * lax.rsqrt(jnp.mean(x * x, axis=-1, keepdims=True) + NORM_EPS) * g


def _ffn_kernel(x_ref, gpre_ref, gpost_ref, wg_ref, wu_ref, wo_ref, o_ref, h_ref, acc_ref):
    j = pl.program_id(1)

    @pl.when(j == 0)
    def _():
        h_ref[...] = _rms(x_ref[...], gpre_ref[...]).astype(BF16)

    h = h_ref[...]
    gate = jnp.dot(h, wg_ref[...], preferred_element_type=F32)
    up = jnp.dot(h, wu_ref[...], preferred_element_type=F32)
    act = (gate * (1.0 / (1.0 + jnp.exp(-gate))) * up).astype(BF16)
    part = jnp.dot(act, wo_ref[...], preferred_element_type=F32)

    @pl.when(j == 0)
    def _():
        acc_ref[...] = part

    @pl.when(j > 0)
    def _():
        acc_ref[...] += part

    @pl.when(j == pl.num_programs(1) - 1)
    def _():
        o_ref[...] = x_ref[...] + FFN_RESIDUAL_SCALE * _rms(acc_ref[...], gpost_ref[...])


def _ffn_half(x, g_pre, g_post, w_in, w_out):
    s = x.shape[0]
    nf = D_FF // FFN_TF
    return pl.pallas_call(
        _ffn_kernel,
        grid=(s // FFN_TM, nf),
        in_specs=[
            pl.BlockSpec((FFN_TM, D_MODEL), lambda i, j: (i, 0)),
            pl.BlockSpec((1, D_MODEL), lambda i, j: (0, 0)),
            pl.BlockSpec((1, D_MODEL), lambda i, j: (0, 0)),
            pl.BlockSpec((D_MODEL, FFN_TF), lambda i, j: (0, j)),
            pl.BlockSpec((D_MODEL, FFN_TF), lambda i, j: (0, j + nf)),
            pl.BlockSpec((FFN_TF, D_MODEL), lambda i, j: (j, 0)),
        ],
        out_specs=pl.BlockSpec((FFN_TM, D_MODEL), lambda i, j: (i, 0)),
        out_shape=jax.ShapeDtypeStruct((s, D_MODEL), F32),
        scratch_shapes=[pltpu.VMEM((FFN_TM, D_MODEL), BF16), pltpu.VMEM((FFN_TM, D_MODEL), F32)],
        compiler_params=pltpu.CompilerParams(
            dimension_semantics=("parallel", "arbitrary"), vmem_limit_bytes=VMEM_LIMIT),
        name="ffn_half",
    )(x, g_pre.reshape(1, D_MODEL), g_post.reshape(1, D_MODEL), w_in, w_in, w_out)


def _rope_tables(s):
    pos = jnp.arange(s, dtype=F32)
    inv_freq = 1.0 / (ROPE_THETA ** (jnp.arange(0, HEAD_DIM, 2, dtype=F32) / HEAD_DIM))
    ang = pos[:, None] * inv_freq[None, :]
    cos, sin, zero = jnp.cos(ang), jnp.sin(ang), jnp.zeros_like(ang)
    cos_t = jnp.concatenate([cos] * 4, axis=-1)
    sin_lo = jnp.concatenate([-sin, zero] * 2, axis=-1)
    sin_hi = jnp.concatenate([zero, sin] * 2, axis=-1)
    return cos_t, sin_lo, sin_hi


def _proj_kernel(x_ref, g_ref, w_ref, cos_ref, sinlo_ref, sinhi_ref, *out_refs,
                 n_rope, scale0, transposed, with_kmean):
    n_chunks = len(transposed)
    h = _rms(x_ref[...], g_ref[...]).astype(BF16)
    tm = h.shape[0]
    reps = D_MODEL // PAIR
    cos = jnp.concatenate([cos_ref[...]] * reps, axis=1)
    sin_lo = jnp.concatenate([sinlo_ref[...]] * reps, axis=1)
    sin_hi = jnp.concatenate([sinhi_ref[...]] * reps, axis=1)
    for c in range(n_chunks):
        y = jnp.dot(h, w_ref[:, c * D_MODEL:(c + 1) * D_MODEL], preferred_element_type=F32)
        if c < n_rope:
            half = HEAD_DIM // 2
            y = (y * cos + pltpu.roll(y, D_MODEL - half, 1) * sin_lo
                 + pltpu.roll(y, half, 1) * sin_hi)
            if c == 0 and with_kmean:
                km = jnp.mean(y.reshape(tm // MOBA_BLOCK, MOBA_BLOCK, D_MODEL), axis=1)
                out_refs[n_chunks][0] = km
        if c == 0 and scale0 != 1.0:
            y = y * scale0
        if transposed[c]:
            out_refs[c][...] = y.T.reshape(N_PAIRS, 1, PAIR, tm).astype(BF16)
        else:
            out_refs[c][...] = y.astype(BF16)


def _project(x, g, w, tables, *, n_rope, scale0, transposed, with_kmean):
    s = x.shape[0]
    tm = ATT_TK
    row = lambda i: (i, 0)
    out_shape, out_specs = [], []
    for tr in transposed:
        if tr:
            out_shape.append(jax.ShapeDtypeStruct((N_PAIRS, s // tm, PAIR, tm), BF16))
            out_specs.append(pl.BlockSpec((N_PAIRS, 1, PAIR, tm), lambda i: (0, i, 0, 0)))
        else:
            out_shape.append(jax.ShapeDtypeStruct((s, D_MODEL), BF16))
            out_specs.append(pl.BlockSpec((tm, D_MODEL), row))
    if with_kmean:
        nb = tm // MOBA_BLOCK
        out_shape.append(jax.ShapeDtypeStruct((s // tm, nb, D_MODEL), F32))
        out_specs.append(pl.BlockSpec((1, nb, D_MODEL), lambda i: (i, 0, 0)))
    return pl.pallas_call(
        functools.partial(_proj_kernel, n_rope=n_rope, scale0=scale0, transposed=transposed,
                          with_kmean=with_kmean),
        grid=(s // tm,),
        in_specs=[
            pl.BlockSpec((tm, D_MODEL), row),
            pl.BlockSpec((1, D_MODEL), lambda i: (0, 0)),
            pl.BlockSpec(w.shape, lambda i: (0, 0)),
            pl.BlockSpec((tm, PAIR), row),
            pl.BlockSpec((tm, PAIR), row),
            pl.BlockSpec((tm, PAIR), row),
        ],
        out_specs=out_specs,
        out_shape=out_shape,
        compiler_params=pltpu.CompilerParams(
            dimension_semantics=("parallel",), vmem_limit_bytes=VMEM_LIMIT),
        name="project",
    )(x, g.reshape(1, D_MODEL), w, *tables)


def _stacked_queries_t(qt_ref):
    qt = jnp.concatenate([qt_ref[r] for r in range(qt_ref.shape[0])], axis=1)
    feat = lax.broadcasted_iota(jnp.int32, qt.shape, 0)
    zero = jnp.zeros_like(qt)
    return jnp.concatenate([jnp.where(feat < HEAD_DIM, qt, zero),
                            jnp.where(feat >= HEAD_DIM, qt, zero)], axis=1)


def _causal_mask_t(st, key_offset):
    tq = st.shape[1] // 2
    key = lax.broadcasted_iota(jnp.int32, st.shape, 0) + key_offset
    col = lax.broadcasted_iota(jnp.int32, st.shape, 1)
    qpos = jnp.where(col >= tq, col - tq, col)
    return jnp.where(key <= qpos, st, -jnp.inf)


def _score_stage(st, s_ref, mx_ref, slot, key_offset):
    if key_offset is not None:
        st = _causal_mask_t(st, key_offset)
    s_ref[slot] = st
    mx_ref[slot] = jnp.max(st, axis=0, keepdims=True)


def _softmax_stage(s_ref, mx_ref, slot, vt, m_ref, acc_ref, split_heads):
    m_prev = m_ref[...]
    m_new = jnp.maximum(m_prev, mx_ref[slot])
    alpha = jnp.exp2(m_prev - m_new)
    p = jnp.exp2(s_ref[slot] - m_new).astype(BF16)
    ones = jnp.ones((SUM_ROWS, vt.shape[1]), BF16)
    if split_heads:
        tq = p.shape[1] // 2
        pv = jnp.concatenate(
            [jnp.dot(jnp.concatenate([vt[0:HEAD_DIM], ones], axis=0), p[:, 0:tq],
                     preferred_element_type=F32),
             jnp.dot(jnp.concatenate([vt[HEAD_DIM:], ones], axis=0), p[:, tq:],
                     preferred_element_type=F32)], axis=1)
    else:
        pv = jnp.dot(jnp.concatenate([vt, ones], axis=0), p, preferred_element_type=F32)
    acc_ref[...] = alpha * acc_ref[...] + pv
    m_ref[...] = m_new


def _run_kv_blocks(i, score, consume):
    @pl.when(i == 0)
    def _():
        score(0, 0, 0)
        score(1, 1, ATT_TK)
        consume(0, 0)
        consume(1, 1)

    @pl.when(i > 0)
    def _():
        score(0, 0, None)

        def pair(p, carry):
            b = 2 * p
            score(b + 1, 1, None)
            consume(b, 0)
            score(b + 2, 0, None)
            consume(b + 1, 1)
            return carry

        lax.fori_loop(0, i - 1, pair, 0)
        b = 2 * i - 2
        score(b + 1, 1, None)
        consume(b, 0)
        score(b + 2, 0, 0)
        consume(b + 1, 1)
        score(b + 3, 1, ATT_TK)
        consume(b + 2, 0)
        consume(b + 3, 1)


def _init_stats(m_ref, acc_ref):
    m_ref[...] = jnp.full(m_ref.shape, -jnp.inf, F32)
    acc_ref[...] = jnp.zeros(acc_ref.shape, F32)


def _normalized(acc_ref, lo, hi):
    rows = acc_ref.shape[0] - SUM_ROWS
    return acc_ref[0:rows, lo:hi] / acc_ref[rows:rows + 1, lo:hi]


def _pipeline_scratch(value_rows):
    cols = 2 * ATT_TQ
    return [pltpu.VMEM((2, ATT_TK, cols), F32), pltpu.VMEM((2, 1, cols), F32),
            pltpu.VMEM((1, cols), F32), pltpu.VMEM((value_rows + SUM_ROWS, cols), F32)]


def _attn_specs(s):
    qt_spec = pl.BlockSpec((None, ATT_TQ // ATT_TK, PAIR, ATT_TK), lambda h, i: (h, i, 0, 0))
    k_spec = pl.BlockSpec((s, PAIR), lambda h, i: (0, h))
    vt_spec = pl.BlockSpec((None, s // ATT_TK, PAIR, ATT_TK), lambda h, i: (h, 0, 0, 0))
    o_spec = pl.BlockSpec((ATT_TQ, PAIR), lambda h, i: (i, h))
    return qt_spec, k_spec, vt_spec, o_spec


def _diff_attn_kernel(qt_ref, k_ref, vt_ref, lam_ref, subln_ref, o_ref,
                      qs_ref, s_ref, mx_ref, m_ref, acc_ref):
    tq, tk = ATT_TQ, ATT_TK
    i = pl.program_id(1)
    qs_ref[...] = _stacked_queries_t(qt_ref)
    _init_stats(m_ref, acc_ref)

    def score(j, slot, key_offset):
        off = pl.multiple_of(j * tk, tk)
        st = jnp.dot(k_ref[pl.ds(off, tk), :], qs_ref[...], preferred_element_type=F32)
        _score_stage(st, s_ref, mx_ref, slot, key_offset)

    def consume(j, slot):
        _softmax_stage(s_ref, mx_ref, slot, vt_ref[j], m_ref, acc_ref, split_heads=False)

    _run_kv_blocks(i, score, consume)

    lv = lam_ref[...]
    lam = (jnp.exp(jnp.sum(lv[0:1] * lv[1:2], axis=-1, keepdims=True))
           - jnp.exp(jnp.sum(lv[2:3] * lv[3:4], axis=-1, keepdims=True)) + DIFF_LAMBDA_INIT)
    ot = _normalized(acc_ref, 0, tq) - lam * _normalized(acc_ref, tq, 2 * tq)
    o_ref[...] = (_rms(ot.T, subln_ref[...]) * (1.0 - DIFF_LAMBDA_INIT)).astype(BF16)


def _diff_attention(qt, k, vt, lam_vecs, subln):
    s = k.shape[0]
    qt_spec, k_spec, vt_spec, o_spec = _attn_specs(s)
    return pl.pallas_call(
        _diff_attn_kernel,
        grid=(N_PAIRS, s // ATT_TQ),
        in_specs=[qt_spec, k_spec, vt_spec,
                  pl.BlockSpec((4, HEAD_DIM), lambda h, i: (0, 0)),
                  pl.BlockSpec((1, PAIR), lambda h, i: (0, 0))],
        out_specs=o_spec,
        out_shape=jax.ShapeDtypeStruct((s, D_MODEL), BF16),
        scratch_shapes=[pltpu.VMEM((PAIR, 2 * ATT_TQ), BF16)] + _pipeline_scratch(PAIR),
        compiler_params=pltpu.CompilerParams(
            dimension_semantics=("parallel", "arbitrary"), vmem_limit_bytes=VMEM_LIMIT),
        name="diff_attn",
    )(qt, k, vt, lam_vecs, subln.reshape(1, PAIR))


def _moba_bias_t(gate, qblk):
    n = lax.broadcasted_iota(jnp.int32, gate.shape, 0)
    nf = n.astype(F32)
    past = n < qblk
    g = jnp.where(past, gate, -jnp.inf)
    picked = jnp.zeros(gate.shape, F32)
    for _ in range(MOBA_TOPK):
        mx = jnp.max(g, axis=0, keepdims=True)
        first = jnp.min(jnp.where(g == mx, nf, float(PAIR)), axis=0, keepdims=True)
        hit = nf == first
        picked = jnp.where(hit, 1.0, picked)
        g = jnp.where(hit, -jnp.inf, g)
    sel_past = jnp.where(past, jnp.where(picked > 0.0, 0.0, MASK_NEG), MASK_NEG)
    return jnp.where(n == qblk, 0.0, sel_past)


def _moba_attn_kernel(qt_ref, k_ref, vt_ref, kmean_ref, o_ref,
                      qa_ref, kaug_ref, s_ref, mx_ref, m_ref, acc_ref):
    tq, tk = ATT_TQ, ATT_TK
    i = pl.program_id(1)
    s_len = k_ref.shape[0]

    @pl.when(i == 0)
    def _():
        kaug_ref[:, 0:PAIR] = k_ref[...]
        blk = lax.broadcasted_iota(jnp.int32, (s_len, PAIR), 0) // MOBA_BLOCK
        lane = lax.broadcasted_iota(jnp.int32, (s_len, PAIR), 1)
        kaug_ref[:, PAIR:] = jnp.where(blk == lane, 1.0, 0.0).astype(BF16)

    qs = _stacked_queries_t(qt_ref)
    gate = jnp.dot(kmean_ref[...], qs.astype(F32), preferred_element_type=F32,
                   precision=lax.Precision.HIGHEST)
    n_blocks = gate.shape[0]
    col = lax.broadcasted_iota(jnp.int32, (1, 2 * tq), 1)
    qblk = (i * tq + jnp.where(col >= tq, col - tq, col)) // MOBA_BLOCK
    qa_ref[0:PAIR, :] = qs
    qa_ref[PAIR:PAIR + n_blocks, :] = _moba_bias_t(gate, qblk).astype(BF16)
    qa_ref[PAIR + n_blocks:, :] = jnp.zeros((PAIR - n_blocks, 2 * tq), BF16)
    _init_stats(m_ref, acc_ref)

    def score(j, slot, key_offset):
        off = pl.multiple_of(j * tk, tk)
        st = jnp.dot(kaug_ref[pl.ds(off, tk), :], qa_ref[...], preferred_element_type=F32)
        _score_stage(st, s_ref, mx_ref, slot, key_offset)

    def consume(j, slot):
        _softmax_stage(s_ref, mx_ref, slot, vt_ref[j], m_ref, acc_ref, split_heads=True)

    _run_kv_blocks(i, score, consume)

    ot = jnp.concatenate([_normalized(acc_ref, 0, tq), _normalized(acc_ref, tq, 2 * tq)], axis=0)
    o_ref[...] = ot.T.astype(BF16)


def _moba_attention(qt, k, vt, kmean):
    s = k.shape[0]
    qt_spec, k_spec, vt_spec, o_spec = _attn_specs(s)
    return pl.pallas_call(
        _moba_attn_kernel,
        grid=(N_PAIRS, s // ATT_TQ),
        in_specs=[qt_spec, k_spec, vt_spec,
                  pl.BlockSpec((s // MOBA_BLOCK, PAIR), lambda h, i: (0, h))],
        out_specs=o_spec,
        out_shape=jax.ShapeDtypeStruct((s, D_MODEL), BF16),
        scratch_shapes=[pltpu.VMEM((2 * PAIR, 2 * ATT_TQ), BF16),
                        pltpu.VMEM((s, 2 * PAIR), BF16)] + _pipeline_scratch(HEAD_DIM),
        compiler_params=pltpu.CompilerParams(
            dimension_semantics=("arbitrary", "arbitrary"), vmem_limit_bytes=VMEM_LIMIT),
        name="moba_attn",
    )(qt, k, vt, kmean)


def _out_proj_kernel(o_ref, w_ref, x_ref, g_ref, y_ref):
    y = jnp.dot(o_ref[...], w_ref[...], preferred_element_type=F32)
    y_ref[...] = x_ref[...] + _rms(y, g_ref[...])


def _out_proj(o, w, x, g):
    s = x.shape[0]
    row = lambda i: (i, 0)
    return pl.pallas_call(
        _out_proj_kernel,
        grid=(s // OUT_TM,),
        in_specs=[
            pl.BlockSpec((OUT_TM, D_MODEL), row),
            pl.BlockSpec((D_MODEL, D_MODEL), lambda i: (0, 0)),
            pl.BlockSpec((OUT_TM, D_MODEL), row),
            pl.BlockSpec((1, D_MODEL), lambda i: (0, 0)),
        ],
        out_specs=pl.BlockSpec((OUT_TM, D_MODEL), row),
        out_shape=jax.ShapeDtypeStruct((s, D_MODEL), F32),
        compiler_params=pltpu.CompilerParams(
            dimension_semantics=("parallel",), vmem_limit_bytes=VMEM_LIMIT),
        name="out_proj",
    )(o, w, x, g.reshape(1, D_MODEL))


def kernel(x, ffn_norm, ffn_w_in, ffn_w_out, mix_norm, diff_w_qkv, diff_lambda, diff_subln,
           diff_w_o, kv_norm, kv_w, moba_w_q, moba_w_o):
    b, s, d = x.shape
    assert b == 1 and d == D_MODEL and s % ATT_TQ == 0
    assert (s // MOBA_BLOCK) % SUM_ROWS == 0 and s // MOBA_BLOCK < PAIR
    assert ffn_w_in.shape[0] == 2 and diff_w_qkv.shape[0] == 1 and moba_w_q.shape[0] == 1
    tables = _rope_tables(s)
    w_in, w_out = ffn_w_in.astype(BF16), ffn_w_out.astype(BF16)

    def ffn(xx, layer, slot):
        return _ffn_half(xx, ffn_norm[layer, slot, 0], ffn_norm[layer, slot, 1],
                         w_in[layer, slot], w_out[layer, slot])

    xs = x.reshape(s, d)

    xs = ffn(xs, 0, 0)
    qt, k, vt = _project(xs, mix_norm[0, 0], diff_w_qkv[0].astype(BF16), tables, n_rope=2,
                         scale0=Q_SCALE, transposed=(True, False, True), with_kmean=False)
    o = _diff_attention(qt, k, vt, diff_lambda[0], diff_subln[0])
    xs = _out_proj(o, diff_w_o[0].astype(BF16), xs, mix_norm[0, 1])
    xs = ffn(xs, 0, 1)

    k, vt, kmean = _project(xs, kv_norm, kv_w.astype(BF16), tables, n_rope=1, scale0=1.0,
                            transposed=(False, True), with_kmean=True)
    kmean = kmean.reshape(s // MOBA_BLOCK, d)
    xs = ffn(xs, 1, 0)
    (qt,) = _project(xs, mix_norm[1, 0], moba_w_q[0].astype(BF16), tables, n_rope=1,
                     scale0=Q_SCALE, transposed=(True,), with_kmean=False)
    o = _moba_attention(qt, k, vt, kmean)
    xs = _out_proj(o, moba_w_o[0].astype(BF16), xs, mix_norm[1, 1])
    xs = ffn(xs, 1, 1)
    return xs.reshape(b, s, d)
```

```python
import functools
import math

import jax
import jax.numpy as jnp
from jax import lax
from jax.experimental import pallas as pl
from jax.experimental.pallas import tpu as pltpu

F32 = jnp.float32
BF16 = jnp.bfloat16

D_MODEL = 1024
HEAD_DIM = 64
PAIR = 2 * HEAD_DIM
N_PAIRS = D_MODEL // PAIR
D_FF = 2816
ROPE_THETA = 10000.0
MOBA_BLOCK = 256
MOBA_TOPK = 3
FFN_RESIDUAL_SCALE = 0.5
NORM_EPS = 1e-6
DIFF_LAMBDA_INIT = 0.2
MASK_NEG = -1e30
Q_SCALE = HEAD_DIM ** -0.5 * math.log2(math.e)

FFN_TM = 512
FFN_TF = D_FF // 2
ATT_TK = 512
ATT_TQ = 2 * ATT_TK
ATT_CW = 256
SUM_ROWS = 16
OUT_TM = 512
VMEM_LIMIT = 56 * 1024 * 1024


def _rms(x, g):
    return x * lax.rsqrt(jnp.mean(x * x, axis=-1, keepdims=True) + NORM_EPS) * g


def _ffn_kernel(x_ref, gpre_ref, gpost_ref, wg_ref, wu_ref, wo_ref, o_ref, h_ref, acc_ref):
    j = pl.program_id(1)

    @pl.when(j == 0)
    def _():
        h_ref[...] = _rms(x_ref[...], gpre_ref[...]).astype(BF16)

    h = h_ref[...]
    gate = jnp.dot(h, wg_ref[...], preferred_element_type=F32)
    up = jnp.dot(h, wu_ref[...], preferred_element_type=F32)
    act = (gate * (1.0 / (1.0 + jnp.exp(-gate))) * up).astype(BF16)
    part = jnp.dot(act, wo_ref[...], preferred_element_type=F32)

    @pl.when(j == 0)
    def _():
        acc_ref[...] = part

    @pl.when(j > 0)
    def _():
        acc_ref[...] += part

    @pl.when(j == pl.num_programs(1) - 1)
    def _():
        o_ref[...] = x_ref[...] + FFN_RESIDUAL_SCALE * _rms(acc_ref[...], gpost_ref[...])


def _ffn_half(x, g_pre, g_post, w_in, w_out):
    s = x.shape[0]
    nf = D_FF // FFN_TF
    return pl.pallas_call(
        _ffn_kernel,
        grid=(s // FFN_TM, nf),
        in_specs=[
            pl.BlockSpec((FFN_TM, D_MODEL), lambda i, j: (i, 0)),
            pl.BlockSpec((1, D_MODEL), lambda i, j: (0, 0)),
            pl.BlockSpec((1, D_MODEL), lambda i, j: (0, 0)),
            pl.BlockSpec((D_MODEL, FFN_TF), lambda i, j: (0, j)),
            pl.BlockSpec((D_MODEL, FFN_TF), lambda i, j: (0, j + nf)),
            pl.BlockSpec((FFN_TF, D_MODEL), lambda i, j: (j, 0)),
        ],
        out_specs=pl.BlockSpec((FFN_TM, D_MODEL), lambda i, j: (i, 0)),
        out_shape=jax.ShapeDtypeStruct((s, D_MODEL), F32),
        scratch_shapes=[pltpu.VMEM((FFN_TM, D_MODEL), BF16), pltpu.VMEM((FFN_TM, D_MODEL), F32)],
        compiler_params=pltpu.CompilerParams(
            dimension_semantics=("parallel", "arbitrary"), vmem_limit_bytes=VMEM_LIMIT),
        name="ffn_half",
    )(x, g_pre.reshape(1, D_MODEL), g_post.reshape(1, D_MODEL), w_in, w_in, w_out)


def _rope_tables(s):
    pos = jnp.arange(s, dtype=F32)
    inv_freq = 1.0 / (ROPE_THETA ** (jnp.arange(0, HEAD_DIM, 2, dtype=F32) / HEAD_DIM))
    ang = pos[:, None] * inv_freq[None, :]
    cos, sin, zero = jnp.cos(ang), jnp.sin(ang), jnp.zeros_like(ang)
    cos_t = jnp.concatenate([cos] * 4, axis=-1)
    sin_lo = jnp.concatenate([-sin, zero] * 2, axis=-1)
    sin_hi = jnp.concatenate([zero, sin] * 2, axis=-1)
    return cos_t, sin_lo, sin_hi


def _proj_kernel(x_ref, g_ref, w_ref, cos_ref, sinlo_ref, sinhi_ref, *out_refs,
                 n_rope, scale0, transposed, with_kmean):
    n_chunks = len(transposed)
    h = _rms(x_ref[...], g_ref[...]).astype(BF16)
    tm = h.shape[0]
    reps = D_MODEL // PAIR
    cos = jnp.concatenate([cos_ref[...]] * reps, axis=1)
    sin_lo = jnp.concatenate([sinlo_ref[...]] * reps, axis=1)
    sin_hi = jnp.concatenate([sinhi_ref[...]] * reps, axis=1)
    for c in range(n_chunks):
        y = jnp.dot(h, w_ref[:, c * D_MODEL:(c + 1) * D_MODEL], preferred_element_type=F32)
        if c < n_rope:
            half = HEAD_DIM // 2
            y = (y * cos + pltpu.roll(y, D_MODEL - half, 1) * sin_lo
                 + pltpu.roll(y, half, 1) * sin_hi)
            if c == 0 and with_kmean:
                km = jnp.mean(y.reshape(tm // MOBA_BLOCK, MOBA_BLOCK, D_MODEL), axis=1)
                out_refs[n_chunks][0] = km
        if c == 0 and scale0 != 1.0:
            y = y * scale0
        if transposed[c]:
            out_refs[c][...] = y.T.reshape(N_PAIRS, 1, PAIR, tm).astype(BF16)
        else:
            out_refs[c][...] = y.astype(BF16)


def _project(x, g, w, tables, *, n_rope, scale0, transposed, with_kmean):
    s = x.shape[0]
    tm = ATT_TK
    row = lambda i: (i, 0)
    out_shape, out_specs = [], []
    for tr in transposed:
        if tr:
            out_shape.append(jax.ShapeDtypeStruct((N_PAIRS, s // tm, PAIR, tm), BF16))
            out_specs.append(pl.BlockSpec((N_PAIRS, 1, PAIR, tm), lambda i: (0, i, 0, 0)))
        else:
            out_shape.append(jax.ShapeDtypeStruct((s, D_MODEL), BF16))
            out_specs.append(pl.BlockSpec((tm, D_MODEL), row))
    if with_kmean:
        nb = tm // MOBA_BLOCK
        out_shape.append(jax.ShapeDtypeStruct((s // tm, nb, D_MODEL), F32))
        out_specs.append(pl.BlockSpec((1, nb, D_MODEL), lambda i: (i, 0, 0)))
    return pl.pallas_call(
        functools.partial(_proj_kernel, n_rope=n_rope, scale0=scale0, transposed=transposed,
                          with_kmean=with_kmean),
        grid=(s // tm,),
        in_specs=[
            pl.BlockSpec((tm, D_MODEL), row),
            pl.BlockSpec((1, D_MODEL), lambda i: (0, 0)),
            pl.BlockSpec(w.shape, lambda i: (0, 0)),
            pl.BlockSpec((tm, PAIR), row),
            pl.BlockSpec((tm, PAIR), row),
            pl.BlockSpec((tm, PAIR), row),
        ],
        out_specs=out_specs,
        out_shape=out_shape,
        compiler_params=pltpu.CompilerParams(
            dimension_semantics=("parallel",), vmem_limit_bytes=VMEM_LIMIT),
        name="project",
    )(x, g.reshape(1, D_MODEL), w, *tables)


def _stacked_queries_t(qt_ref):
    qt = jnp.concatenate([qt_ref[r] for r in range(qt_ref.shape[0])], axis=1)
    feat = lax.broadcasted_iota(jnp.int32, qt.shape, 0)
    zero = jnp.zeros_like(qt)
    return jnp.concatenate([jnp.where(feat < HEAD_DIM, qt, zero),
                            jnp.where(feat >= HEAD_DIM, qt, zero)], axis=1)


def _chunk_visibility(key_offset, c):
    if key_offset is None:
        return "all"
    q0 = (c * ATT_CW) % ATT_TQ
    if key_offset > q0 + ATT_CW - 1:
        return "none"
    if key_offset + ATT_TK - 1 <= q0:
        return "all"
    return "part"


def _score_chunk(k_ref, row0, q_ref, s_ref, mx_ref, key_offset, c):
    vis = _chunk_visibility(key_offset, c)
    if vis == "none":
        return
    cols = slice(c * ATT_CW, (c + 1) * ATT_CW)
    st = jnp.dot(k_ref[pl.ds(row0, ATT_TK), :], q_ref[:, cols], preferred_element_type=F32)
    if vis == "part":
        key = lax.broadcasted_iota(jnp.int32, st.shape, 0) + key_offset
        qpos = lax.broadcasted_iota(jnp.int32, st.shape, 1) + (c * ATT_CW) % ATT_TQ
        st = jnp.where(key <= qpos, st, -jnp.inf)
    s_ref[:, cols] = st
    mx_ref[:, cols] = jnp.max(st, axis=0, keepdims=True)


def _consume_chunk(s_ref, mx_ref, vt, m_ref, acc_ref, key_offset, c, split_heads):
    if _chunk_visibility(key_offset, c) == "none":
        return
    cols = slice(c * ATT_CW, (c + 1) * ATT_CW)
    m_prev = m_ref[:, cols]
    m_new = jnp.maximum(m_prev, mx_ref[:, cols])
    alpha = jnp.exp2(m_prev - m_new)
    p = jnp.exp2(s_ref[:, cols] - m_new).astype(BF16)
    if split_heads:
        vt = vt[0:HEAD_DIM] if c * ATT_CW < ATT_TQ else vt[HEAD_DIM:]
    vt_sum = jnp.concatenate([vt, jnp.ones((SUM_ROWS, vt.shape[1]), BF16)], axis=0)
    acc_ref[:, cols] = alpha * acc_ref[:, cols] + jnp.dot(vt_sum, p, preferred_element_type=F32)
    m_ref[:, cols] = m_new


def _run_kv_blocks(i, score_chunk, consume_chunk):
    def fused(sc, cs):
        for c in range(2 * ATT_TQ // ATT_CW):
            if sc is not None:
                score_chunk(*sc, c)
            if cs is not None:
                consume_chunk(*cs, c)

    @pl.when(i == 0)
    def _():
        fused((0, 0, 0), None)
        fused((1, 1, ATT_TK), (0, 0, 0))
        fused(None, (1, 1, ATT_TK))

    @pl.when(i > 0)
    def _():
        fused((0, 0, None), None)

        def pair(p, carry):
            b = 2 * p
            fused((b + 1, 1, None), (b, 0, None))
            fused((b + 2, 0, None), (b + 1, 1, None))
            return carry

        lax.fori_loop(0, i - 1, pair, 0)
        b = 2 * i - 2
        fused((b + 1, 1, None), (b, 0, None))
        fused((b + 2, 0, 0), (b + 1, 1, None))
        fused((b + 3, 1, ATT_TK), (b + 2, 0, 0))
        fused(None, (b + 3, 1, ATT_TK))


def _init_stats(m_ref, acc_ref):
    m_ref[...] = jnp.full(m_ref.shape, -jnp.inf, F32)
    acc_ref[...] = jnp.zeros(acc_ref.shape, F32)


def _normalized(acc_ref, lo, hi):
    rows = acc_ref.shape[0] - SUM_ROWS
    return acc_ref[0:rows, lo:hi] / acc_ref[rows:rows + 1, lo:hi]


def _pipeline_scratch(value_rows):
    cols = 2 * ATT_TQ
    return [pltpu.VMEM((ATT_TK, cols), F32), pltpu.VMEM((ATT_TK, cols), F32),
            pltpu.VMEM((1, cols), F32), pltpu.VMEM((1, cols), F32),
            pltpu.VMEM((1, cols), F32), pltpu.VMEM((value_rows + SUM_ROWS, cols), F32)]


def _attn_specs(s):
    qt_spec = pl.BlockSpec((None, ATT_TQ // ATT_TK, PAIR, ATT_TK), lambda h, i: (h, i, 0, 0))
    k_spec = pl.BlockSpec((s, PAIR), lambda h, i: (0, h))
    vt_spec = pl.BlockSpec((None, s // ATT_TK, PAIR, ATT_TK), lambda h, i: (h, 0, 0, 0))
    o_spec = pl.BlockSpec((ATT_TQ, PAIR), lambda h, i: (i, h))
    return qt_spec, k_spec, vt_spec, o_spec


def _diff_attn_kernel(qt_ref, k_ref, vt_ref, lam_ref, subln_ref, o_ref,
                      qs_ref, s0_ref, s1_ref, mx0_ref, mx1_ref, m_ref, acc_ref):
    tq, tk = ATT_TQ, ATT_TK
    i = pl.program_id(1)
    qs_ref[...] = _stacked_queries_t(qt_ref)
    _init_stats(m_ref, acc_ref)

    def score_chunk(j, slot, key_offset, c):
        _score_chunk(k_ref, pl.multiple_of(j * tk, tk), qs_ref, (s0_ref, s1_ref)[slot],
                     (mx0_ref, mx1_ref)[slot], key_offset, c)

    def consume_chunk(j, slot, key_offset, c):
        _consume_chunk((s0_ref, s1_ref)[slot], (mx0_ref, mx1_ref)[slot], vt_ref[j], m_ref, acc_ref,
                       key_offset, c, split_heads=False)

    _run_kv_blocks(i, score_chunk, consume_chunk)

    lv = lam_ref[...]
    lam = (jnp.exp(jnp.sum(lv[0:1] * lv[1:2], axis=-1, keepdims=True))
           - jnp.exp(jnp.sum(lv[2:3] * lv[3:4], axis=-1, keepdims=True)) + DIFF_LAMBDA_INIT)
    ot = _normalized(acc_ref, 0, tq) - lam * _normalized(acc_ref, tq, 2 * tq)
    o_ref[...] = (_rms(ot.T, subln_ref[...]) * (1.0 - DIFF_LAMBDA_INIT)).astype(BF16)


def _diff_attention(qt, k, vt, lam_vecs, subln):
    s = k.shape[0]
    qt_spec, k_spec, vt_spec, o_spec = _attn_specs(s)
    return pl.pallas_call(
        _diff_attn_kernel,
        grid=(N_PAIRS, s // ATT_TQ),
        in_specs=[qt_spec, k_spec, vt_spec,
                  pl.BlockSpec((4, HEAD_DIM), lambda h, i: (0, 0)),
                  pl.BlockSpec((1, PAIR), lambda h, i: (0, 0))],
        out_specs=o_spec,
        out_shape=jax.ShapeDtypeStruct((s, D_MODEL), BF16),
        scratch_shapes=[pltpu.VMEM((PAIR, 2 * ATT_TQ), BF16)] + _pipeline_scratch(PAIR),
        compiler_params=pltpu.CompilerParams(
            dimension_semantics=("parallel", "arbitrary"), vmem_limit_bytes=VMEM_LIMIT),
        name="diff_attn",
    )(qt, k, vt, lam_vecs, subln.reshape(1, PAIR))


def _moba_bias_t(gate, qblk):
    n = lax.broadcasted_iota(jnp.int32, gate.shape, 0)
    nf = n.astype(F32)
    past = n < qblk
    g = jnp.where(past, gate, -jnp.inf)
    picked = jnp.zeros(gate.shape, F32)
    for _ in range(MOBA_TOPK):
        mx = jnp.max(g, axis=0, keepdims=True)
        first = jnp.min(jnp.where(g == mx, nf, float(PAIR)), axis=0, keepdims=True)
        hit = nf == first
        picked = jnp.where(hit, 1.0, picked)
        g = jnp.where(hit, -jnp.inf, g)
    sel_past = jnp.where(past, jnp.where(picked > 0.0, 0.0, MASK_NEG), MASK_NEG)
    return jnp.where(n == qblk, 0.0, sel_past)


def _moba_attn_kernel(qt_ref, k_ref, vt_ref, kmean_ref, o_ref,
                      qa_ref, kaug_ref, s0_ref, s1_ref, mx0_ref, mx1_ref, m_ref, acc_ref):
    tq, tk = ATT_TQ, ATT_TK
    i = pl.program_id(1)
    s_len = k_ref.shape[0]

    @pl.when(i == 0)
    def _():
        kaug_ref[:, 0:PAIR] = k_ref[...]
        blk = lax.broadcasted_iota(jnp.int32, (s_len, PAIR), 0) // MOBA_BLOCK
        lane = lax.broadcasted_iota(jnp.int32, (s_len, PAIR), 1)
        kaug_ref[:, PAIR:] = jnp.where(blk == lane, 1.0, 0.0).astype(BF16)

    qs = _stacked_queries_t(qt_ref)
    gate = jnp.dot(kmean_ref[...], qs.astype(F32), preferred_element_type=F32,
                   precision=lax.Precision.HIGHEST)
    n_blocks = gate.shape[0]
    col = lax.broadcasted_iota(jnp.int32, (1, 2 * tq), 1)
    qblk = (i * tq + jnp.where(col >= tq, col - tq, col)) // MOBA_BLOCK
    qa_ref[0:PAIR, :] = qs
    qa_ref[PAIR:PAIR + n_blocks, :] = _moba_bias_t(gate, qblk).astype(BF16)
    qa_ref[PAIR + n_blocks:, :] = jnp.zeros((PAIR - n_blocks, 2 * tq), BF16)
    _init_stats(m_ref, acc_ref)

    def score_chunk(j, slot, key_offset, c):
        _score_chunk(kaug_ref, pl.multiple_of(j * tk, tk), qa_ref, (s0_ref, s1_ref)[slot],
                     (mx0_ref, mx1_ref)[slot], key_offset, c)

    def consume_chunk(j, slot, key_offset, c):
        _consume_chunk((s0_ref, s1_ref)[slot], (mx0_ref, mx1_ref)[slot], vt_ref[j], m_ref, acc_ref,
                       key_offset, c, split_heads=True)

    _run_kv_blocks(i, score_chunk, consume_chunk)

    ot = jnp.concatenate([_normalized(acc_ref, 0, tq), _normalized(acc_ref, tq, 2 * tq)], axis=0)
    o_ref[...] = ot.T.astype(BF16)


def _moba_attention(qt, k, vt, kmean):
    s = k.shape[0]
    qt_spec, k_spec, vt_spec, o_spec = _attn_specs(s)
    return pl.pallas_call(
        _moba_attn_kernel,
        grid=(N_PAIRS, s // ATT_TQ),
        in_specs=[qt_spec, k_spec, vt_spec,
                  pl.BlockSpec((s // MOBA_BLOCK, PAIR), lambda h, i: (0, h))],
        out_specs=o_spec,
        out_shape=jax.ShapeDtypeStruct((s, D_MODEL), BF16),
        scratch_shapes=[pltpu.VMEM((2 * PAIR, 2 * ATT_TQ), BF16),
                        pltpu.VMEM((s, 2 * PAIR), BF16)] + _pipeline_scratch(HEAD_DIM),
        compiler_params=pltpu.CompilerParams(
            dimension_semantics=("arbitrary", "arbitrary"), vmem_limit_bytes=VMEM_LIMIT),
        name="moba_attn",
    )(qt, k, vt, kmean)


def _out_proj_kernel(o_ref, w_ref, x_ref, g_ref, y_ref):
    y = jnp.dot(o_ref[...], w_ref[...], preferred_element_type=F32)
    y_ref[...] = x_ref[...] + _rms(y, g_ref[...])


def _out_proj(o, w, x, g):
    s = x.shape[0]
    row = lambda i: (i, 0)
    return pl.pallas_call(
        _out_proj_kernel,
        grid=(s // OUT_TM,),
        in_specs=[
            pl.BlockSpec((OUT_TM, D_MODEL), row),
            pl.BlockSpec((D_MODEL, D_MODEL), lambda i: (0, 0)),
            pl.BlockSpec((OUT_TM, D_MODEL), row),
            pl.BlockSpec((1, D_MODEL), lambda i: (0, 0)),
        ],
        out_specs=pl.BlockSpec((OUT_TM, D_MODEL), row),
        out_shape=jax.ShapeDtypeStruct((s, D_MODEL), F32),
        compiler_params=pltpu.CompilerParams(
            dimension_semantics=("parallel",), vmem_limit_bytes=VMEM_LIMIT),
        name="out_proj",
    )(o, w, x, g.reshape(1, D_MODEL))


def kernel(x, ffn_norm, ffn_w_in, ffn_w_out, mix_norm, diff_w_qkv, diff_lambda, diff_subln,
           diff_w_o, kv_norm, kv_w, moba_w_q, moba_w_o):
    b, s, d = x.shape
    assert b == 1 and d == D_MODEL and s % ATT_TQ == 0
    assert (s // MOBA_BLOCK) % SUM_ROWS == 0 and s // MOBA_BLOCK < PAIR
    assert ffn_w_in.shape[0] == 2 and diff_w_qkv.shape[0] == 1 and moba_w_q.shape[0] == 1
    tables = _rope_tables(s)
    w_in, w_out = ffn_w_in.astype(BF16), ffn_w_out.astype(BF16)

    def ffn(xx, layer, slot):
        return _ffn_half(xx, ffn_norm[layer, slot, 0], ffn_norm[layer, slot, 1],
                         w_in[layer, slot], w_out[layer, slot])

    xs = x.reshape(s, d)

    xs = ffn(xs, 0, 0)
    qt, k, vt = _project(xs, mix_norm[0, 0], diff_w_qkv[0].astype(BF16), tables, n_rope=2,
                         scale0=Q_SCALE, transposed=(True, False, True), with_kmean=False)
    o = _diff_attention(qt, k, vt, diff_lambda[0], diff_subln[0])
    xs = _out_proj(o, diff_w_o[0].astype(BF16), xs, mix_norm[0, 1])
    xs = ffn(xs, 0, 1)

    k, vt, kmean = _project(xs, kv_norm, kv_w.astype(BF16), tables, n_rope=1, scale0=1.0,
                            transposed=(False, True), with_kmean=True)
    kmean = kmean.reshape(s // MOBA_BLOCK, d)
    xs = ffn(xs, 1, 0)
    (qt,) = _project(xs, mix_norm[1, 0], moba_w_q[0].astype(BF16), tables, n_rope=1,
                     scale0=Q_SCALE, transposed=(True,), with_kmean=False)
    o = _moba_attention(qt, k, vt, kmean)
    xs = _out_proj(o, moba_w_o[0].astype(BF16), xs, mix_norm[1, 1])
    xs = ffn(xs, 1, 1)
    return xs.reshape(b, s, d)
```

```python
import functools
import math

import jax
import jax.numpy as jnp
from jax import lax
from jax.experimental import pallas as pl
from jax.experimental.pallas import tpu as pltpu

F32 = jnp.float32
BF16 = jnp.bfloat16

D_MODEL = 1024
HEAD_DIM = 64
PAIR = 2 * HEAD_DIM
N_PAIRS = D_MODEL // PAIR
D_FF = 2816
ROPE_THETA = 10000.0
MOBA_BLOCK = 256
MOBA_TOPK = 3
FFN_RESIDUAL_SCALE = 0.5
NORM_EPS = 1e-6
DIFF_LAMBDA_INIT = 0.2
MASK_NEG = -1e30
Q_SCALE = HEAD_DIM ** -0.5 * math.log2(math.e)

FFN_TM = 512
FFN_CW = 256
ATT_TK = 512
ATT_TQ = 2 * ATT_TK
ATT_CW = 256
SUM_ROWS = 16
OUT_TM = 512
VMEM_LIMIT = 56 * 1024 * 1024


def _rms(x, g):
    return x * lax.rsqrt(jnp.mean(x * x, axis=-1, keepdims=True) + NORM_EPS) * g


def _ffn_kernel(x_ref, gpre_ref, gpost_ref, win_ref, wout_ref, o_ref, act_ref):
    x = x_ref[...]
    h = _rms(x, gpre_ref[...]).astype(BF16)
    for c in range(D_FF // FFN_CW):
        lo = c * FFN_CW
        gate = jnp.dot(h, win_ref[:, lo:lo + FFN_CW], preferred_element_type=F32)
        up = jnp.dot(h, win_ref[:, D_FF + lo:D_FF + lo + FFN_CW], preferred_element_type=F32)
        act_ref[:, lo:lo + FFN_CW] = (gate * (1.0 / (1.0 + jnp.exp(-gate))) * up).astype(BF16)
    y = jnp.dot(act_ref[...], wout_ref[...], preferred_element_type=F32)
    o_ref[...] = x + FFN_RESIDUAL_SCALE * _rms(y, gpost_ref[...])


def _ffn_half(x, g_pre, g_post, w_in, w_out):
    s = x.shape[0]
    row = lambda i: (i, 0)
    const = lambda i: (0, 0)
    resident = pl.Buffered(1)
    return pl.pallas_call(
        _ffn_kernel,
        grid=(s // FFN_TM,),
        in_specs=[
            pl.BlockSpec((FFN_TM, D_MODEL), row),
            pl.BlockSpec((1, D_MODEL), const),
            pl.BlockSpec((1, D_MODEL), const),
            pl.BlockSpec((D_MODEL, 2 * D_FF), const, pipeline_mode=resident),
            pl.BlockSpec((D_FF, D_MODEL), const, pipeline_mode=resident),
        ],
        out_specs=pl.BlockSpec((FFN_TM, D_MODEL), row),
        out_shape=jax.ShapeDtypeStruct((s, D_MODEL), F32),
        scratch_shapes=[pltpu.VMEM((FFN_TM, D_FF), BF16)],
        compiler_params=pltpu.CompilerParams(
            dimension_semantics=("parallel",), vmem_limit_bytes=VMEM_LIMIT),
        name="ffn_half",
    )(x, g_pre.reshape(1, D_MODEL), g_post.reshape(1, D_MODEL), w_in, w_out)


def _rope_tables(s):
    pos = jnp.arange(s, dtype=F32)
    inv_freq = 1.0 / (ROPE_THETA ** (jnp.arange(0, HEAD_DIM, 2, dtype=F32) / HEAD_DIM))
    ang = pos[:, None] * inv_freq[None, :]
    cos, sin, zero = jnp.cos(ang), jnp.sin(ang), jnp.zeros_like(ang)
    cos_t = jnp.concatenate([cos] * 4, axis=-1)
    sin_lo = jnp.concatenate([-sin, zero] * 2, axis=-1)
    sin_hi = jnp.concatenate([zero, sin] * 2, axis=-1)
    return cos_t, sin_lo, sin_hi


def _proj_kernel(x_ref, g_ref, w_ref, cos_ref, sinlo_ref, sinhi_ref, *out_refs,
                 n_rope, scale0, transposed, with_kmean):
    n_chunks = len(transposed)
    h = _rms(x_ref[...], g_ref[...]).astype(BF16)
    tm = h.shape[0]
    reps = D_MODEL // PAIR
    cos = jnp.concatenate([cos_ref[...]] * reps, axis=1)
    sin_lo = jnp.concatenate([sinlo_ref[...]] * reps, axis=1)
    sin_hi = jnp.concatenate([sinhi_ref[...]] * reps, axis=1)
    for c in range(n_chunks):
        y = jnp.dot(h, w_ref[:, c * D_MODEL:(c + 1) * D_MODEL], preferred_element_type=F32)
        if c < n_rope:
            half = HEAD_DIM // 2
            y = (y * cos + pltpu.roll(y, D_MODEL - half, 1) * sin_lo
                 + pltpu.roll(y, half, 1) * sin_hi)
            if c == 0 and with_kmean:
                km = jnp.mean(y.reshape(tm // MOBA_BLOCK, MOBA_BLOCK, D_MODEL), axis=1)
                out_refs[n_chunks][0] = km
        if c == 0 and scale0 != 1.0:
            y = y * scale0
        if transposed[c]:
            out_refs[c][...] = y.T.reshape(N_PAIRS, 1, PAIR, tm).astype(BF16)
        else:
            out_refs[c][...] = y.astype(BF16)


def _project(x, g, w, tables, *, n_rope, scale0, transposed, with_kmean):
    s = x.shape[0]
    tm = ATT_TK
    row = lambda i: (i, 0)
    out_shape, out_specs = [], []
    for tr in transposed:
        if tr:
            out_shape.append(jax.ShapeDtypeStruct((N_PAIRS, s // tm, PAIR, tm), BF16))
            out_specs.append(pl.BlockSpec((N_PAIRS, 1, PAIR, tm), lambda i: (0, i, 0, 0)))
        else:
            out_shape.append(jax.ShapeDtypeStruct((s, D_MODEL), BF16))
            out_specs.append(pl.BlockSpec((tm, D_MODEL), row))
    if with_kmean:
        nb = tm // MOBA_BLOCK
        out_shape.append(jax.ShapeDtypeStruct((s // tm, nb, D_MODEL), F32))
        out_specs.append(pl.BlockSpec((1, nb, D_MODEL), lambda i: (i, 0, 0)))
    return pl.pallas_call(
        functools.partial(_proj_kernel, n_rope=n_rope, scale0=scale0, transposed=transposed,
                          with_kmean=with_kmean),
        grid=(s // tm,),
        in_specs=[
            pl.BlockSpec((tm, D_MODEL), row),
            pl.BlockSpec((1, D_MODEL), lambda i: (0, 0)),
            pl.BlockSpec(w.shape, lambda i: (0, 0)),
            pl.BlockSpec((tm, PAIR), row),
            pl.BlockSpec((tm, PAIR), row),
            pl.BlockSpec((tm, PAIR), row),
        ],
        out_specs=out_specs,
        out_shape=out_shape,
        compiler_params=pltpu.CompilerParams(
            dimension_semantics=("parallel",), vmem_limit_bytes=VMEM_LIMIT),
        name="project",
    )(x, g.reshape(1, D_MODEL), w, *tables)


def _stacked_queries_t(qt_ref):
    qt = jnp.concatenate([qt_ref[r] for r in range(qt_ref.shape[0])], axis=1)
    feat = lax.broadcasted_iota(jnp.int32, qt.shape, 0)
    zero = jnp.zeros_like(qt)
    return jnp.concatenate([jnp.where(feat < HEAD_DIM, qt, zero),
                            jnp.where(feat >= HEAD_DIM, qt, zero)], axis=1)


def _chunk_visibility(key_offset, c):
    if key_offset is None:
        return "all"
    q0 = (c * ATT_CW) % ATT_TQ
    if key_offset > q0 + ATT_CW - 1:
        return "none"
    if key_offset + ATT_TK - 1 <= q0:
        return "all"
    return "part"


def _score_chunk(k_ref, row0, q_ref, s_ref, mx_ref, key_offset, c):
    vis = _chunk_visibility(key_offset, c)
    if vis == "none":
        return
    cols = slice(c * ATT_CW, (c + 1) * ATT_CW)
    st = jnp.dot(k_ref[pl.ds(row0, ATT_TK), :], q_ref[:, cols], preferred_element_type=F32)
    if vis == "part":
        key = lax.broadcasted_iota(jnp.int32, st.shape, 0) + key_offset
        qpos = lax.broadcasted_iota(jnp.int32, st.shape, 1) + (c * ATT_CW) % ATT_TQ
        st = jnp.where(key <= qpos, st, -jnp.inf)
    s_ref[:, cols] = st
    mx_ref[:, cols] = jnp.max(st, axis=0, keepdims=True)


def _consume_chunk(s_ref, mx_ref, vt, m_ref, acc_ref, key_offset, c, split_heads):
    if _chunk_visibility(key_offset, c) == "none":
        return
    cols = slice(c * ATT_CW, (c + 1) * ATT_CW)
    m_prev = m_ref[:, cols]
    m_new = jnp.maximum(m_prev, mx_ref[:, cols])
    alpha = jnp.exp2(m_prev - m_new)
    p = jnp.exp2(s_ref[:, cols] - m_new).astype(BF16)
    if split_heads:
        vt = vt[0:HEAD_DIM] if c * ATT_CW < ATT_TQ else vt[HEAD_DIM:]
    vt_sum = jnp.concatenate([vt, jnp.ones((SUM_ROWS, vt.shape[1]), BF16)], axis=0)
    acc_ref[:, cols] = alpha * acc_ref[:, cols] + jnp.dot(vt_sum, p, preferred_element_type=F32)
    m_ref[:, cols] = m_new


def _run_kv_blocks(i, score_chunk, consume_chunk):
    def fused(sc, cs):
        for c in range(2 * ATT_TQ // ATT_CW):
            if sc is not None:
                score_chunk(*sc, c)
            if cs is not None:
                consume_chunk(*cs, c)

    @pl.when(i == 0)
    def _():
        fused((0, 0, 0), None)
        fused((1, 1, ATT_TK), (0, 0, 0))
        fused(None, (1, 1, ATT_TK))

    @pl.when(i > 0)
    def _():
        fused((0, 0, None), None)

        def pair(p, carry):
            b = 2 * p
            fused((b + 1, 1, None), (b, 0, None))
            fused((b + 2, 0, None), (b + 1, 1, None))
            return carry

        lax.fori_loop(0, i - 1, pair, 0)
        b = 2 * i - 2
        fused((b + 1, 1, None), (b, 0, None))
        fused((b + 2, 0, 0), (b + 1, 1, None))
        fused((b + 3, 1, ATT_TK), (b + 2, 0, 0))
        fused(None, (b + 3, 1, ATT_TK))


def _init_stats(m_ref, acc_ref):
    m_ref[...] = jnp.full(m_ref.shape, -jnp.inf, F32)
    acc_ref[...] = jnp.zeros(acc_ref.shape, F32)


def _normalized(acc_ref, lo, hi):
    rows = acc_ref.shape[0] - SUM_ROWS
    return acc_ref[0:rows, lo:hi] / acc_ref[rows:rows + 1, lo:hi]


def _pipeline_scratch(value_rows):
    cols = 2 * ATT_TQ
    return [pltpu.VMEM((ATT_TK, cols), F32), pltpu.VMEM((ATT_TK, cols), F32),
            pltpu.VMEM((1, cols), F32), pltpu.VMEM((1, cols), F32),
            pltpu.VMEM((1, cols), F32), pltpu.VMEM((value_rows + SUM_ROWS, cols), F32)]


def _attn_specs(s):
    qt_spec = pl.BlockSpec((None, ATT_TQ // ATT_TK, PAIR, ATT_TK), lambda h, i: (h, i, 0, 0))
    k_spec = pl.BlockSpec((s, PAIR), lambda h, i: (0, h))
    vt_spec = pl.BlockSpec((None, s // ATT_TK, PAIR, ATT_TK), lambda h, i: (h, 0, 0, 0))
    o_spec = pl.BlockSpec((ATT_TQ, PAIR), lambda h, i: (i, h))
    return qt_spec, k_spec, vt_spec, o_spec


def _diff_attn_kernel(qt_ref, k_ref, vt_ref, lam_ref, subln_ref, o_ref,
                      qs_ref, s0_ref, s1_ref, mx0_ref, mx1_ref, m_ref, acc_ref):
    tq, tk = ATT_TQ, ATT_TK
    i = pl.program_id(1)
    qs_ref[...] = _stacked_queries_t(qt_ref)
    _init_stats(m_ref, acc_ref)

    def score_chunk(j, slot, key_offset, c):
        _score_chunk(k_ref, pl.multiple_of(j * tk, tk), qs_ref, (s0_ref, s1_ref)[slot],
                     (mx0_ref, mx1_ref)[slot], key_offset, c)

    def consume_chunk(j, slot, key_offset, c):
        _consume_chunk((s0_ref, s1_ref)[slot], (mx0_ref, mx1_ref)[slot], vt_ref[j], m_ref, acc_ref,
                       key_offset, c, split_heads=False)

    _run_kv_blocks(i, score_chunk, consume_chunk)

    lv = lam_ref[...]
    lam = (jnp.exp(jnp.sum(lv[0:1] * lv[1:2], axis=-1, keepdims=True))
           - jnp.exp(jnp.sum(lv[2:3] * lv[3:4], axis=-1, keepdims=True)) + DIFF_LAMBDA_INIT)
    ot = _normalized(acc_ref, 0, tq) - lam * _normalized(acc_ref, tq, 2 * tq)
    o_ref[...] = (_rms(ot.T, subln_ref[...]) * (1.0 - DIFF_LAMBDA_INIT)).astype(BF16)


def _diff_attention(qt, k, vt, lam_vecs, subln):
    s = k.shape[0]
    qt_spec, k_spec, vt_spec, o_spec = _attn_specs(s)
    return pl.pallas_call(
        _diff_attn_kernel,
        grid=(N_PAIRS, s // ATT_TQ),
        in_specs=[qt_spec, k_spec, vt_spec,
                  pl.BlockSpec((4, HEAD_DIM), lambda h, i: (0, 0)),
                  pl.BlockSpec((1, PAIR), lambda h, i: (0, 0))],
        out_specs=o_spec,
        out_shape=jax.ShapeDtypeStruct((s, D_MODEL), BF16),
        scratch_shapes=[pltpu.VMEM((PAIR, 2 * ATT_TQ), BF16)] + _pipeline_scratch(PAIR),
        compiler_params=pltpu.CompilerParams(
            dimension_semantics=("parallel", "arbitrary"), vmem_limit_bytes=VMEM_LIMIT),
        name="diff_attn",
    )(qt, k, vt, lam_vecs, subln.reshape(1, PAIR))


def _moba_bias_t(gate, qblk):
    n = lax.broadcasted_iota(jnp.int32, gate.shape, 0)
    nf = n.astype(F32)
    past = n < qblk
    g = jnp.where(past, gate, -jnp.inf)
    picked = jnp.zeros(gate.shape, F32)
    for _ in range(MOBA_TOPK):
        mx = jnp.max(g, axis=0, keepdims=True)
        first = jnp.min(jnp.where(g == mx, nf, float(PAIR)), axis=0, keepdims=True)
        hit = nf == first
        picked = jnp.where(hit, 1.0, picked)
        g = jnp.where(hit, -jnp.inf, g)
    sel_past = jnp.where(past, jnp.where(picked > 0.0, 0.0, MASK_NEG), MASK_NEG)
    return jnp.where(n == qblk, 0.0, sel_past)


def _moba_attn_kernel(qt_ref, k_ref, vt_ref, kmean_ref, o_ref,
                      qa_ref, kaug_ref, s0_ref, s1_ref, mx0_ref, mx1_ref, m_ref, acc_ref):
    tq, tk = ATT_TQ, ATT_TK
    i = pl.program_id(1)
    s_len = k_ref.shape[0]

    @pl.when(i == 0)
    def _():
        kaug_ref[:, 0:PAIR] = k_ref[...]
        blk = lax.broadcasted_iota(jnp.int32, (s_len, PAIR), 0) // MOBA_BLOCK
        lane = lax.broadcasted_iota(jnp.int32, (s_len, PAIR), 1)
        kaug_ref[:, PAIR:] = jnp.where(blk == lane, 1.0, 0.0).astype(BF16)

    qs = _stacked_queries_t(qt_ref)
    gate, rest = None, kmean_ref[...]
    for _ in range(3):
        piece = rest.astype(BF16)
        part = jnp.dot(piece, qs, preferred_element_type=F32)
        gate = part if gate is None else gate + part
        rest = rest - piece.astype(F32)
    n_blocks = gate.shape[0]
    col = lax.broadcasted_iota(jnp.int32, (1, 2 * tq), 1)
    qblk = (i * tq + jnp.where(col >= tq, col - tq, col)) // MOBA_BLOCK
    qa_ref[0:PAIR, :] = qs
    qa_ref[PAIR:PAIR + n_blocks, :] = _moba_bias_t(gate, qblk).astype(BF16)
    qa_ref[PAIR + n_blocks:, :] = jnp.zeros((PAIR - n_blocks, 2 * tq), BF16)
    _init_stats(m_ref, acc_ref)

    def score_chunk(j, slot, key_offset, c):
        _score_chunk(kaug_ref, pl.multiple_of(j * tk, tk), qa_ref, (s0_ref, s1_ref)[slot],
                     (mx0_ref, mx1_ref)[slot], key_offset, c)

    def consume_chunk(j, slot, key_offset, c):
        _consume_chunk((s0_ref, s1_ref)[slot], (mx0_ref, mx1_ref)[slot], vt_ref[j], m_ref, acc_ref,
                       key_offset, c, split_heads=True)

    _run_kv_blocks(i, score_chunk, consume_chunk)

    ot = jnp.concatenate([_normalized(acc_ref, 0, tq), _normalized(acc_ref, tq, 2 * tq)], axis=0)
    o_ref[...] = ot.T.astype(BF16)


def _moba_attention(qt, k, vt, kmean):
    s = k.shape[0]
    qt_spec, k_spec, vt_spec, o_spec = _attn_specs(s)
    return pl.pallas_call(
        _moba_attn_kernel,
        grid=(N_PAIRS, s // ATT_TQ),
        in_specs=[qt_spec, k_spec, vt_spec,
                  pl.BlockSpec((s // MOBA_BLOCK, PAIR), lambda h, i: (0, h))],
        out_specs=o_spec,
        out_shape=jax.ShapeDtypeStruct((s, D_MODEL), BF16),
        scratch_shapes=[pltpu.VMEM((2 * PAIR, 2 * ATT_TQ), BF16),
                        pltpu.VMEM((s, 2 * PAIR), BF16)] + _pipeline_scratch(HEAD_DIM),
        compiler_params=pltpu.CompilerParams(
            dimension_semantics=("arbitrary", "arbitrary"), vmem_limit_bytes=VMEM_LIMIT),
        name="moba_attn",
    )(qt, k, vt, kmean)


def _out_proj_kernel(o_ref, w_ref, x_ref, g_ref, y_ref):
    y = jnp.dot(o_ref[...], w_ref[...], preferred_element_type=F32)
    y_ref[...] = x_ref[...] + _rms(y, g_ref[...])


def _out_proj(o, w, x, g):
    s = x.shape[0]
    row = lambda i: (i, 0)
    return pl.pallas_call(
        _out_proj_kernel,
        grid=(s // OUT_TM,),
        in_specs=[
            pl.BlockSpec((OUT_TM, D_MODEL), row),
            pl.BlockSpec((D_MODEL, D_MODEL), lambda i: (0, 0)),
            pl.BlockSpec((OUT_TM, D_MODEL), row),
            pl.BlockSpec((1, D_MODEL), lambda i: (0, 0)),
        ],
        out_specs=pl.BlockSpec((OUT_TM, D_MODEL), row),
        out_shape=jax.ShapeDtypeStruct((s, D_MODEL), F32),
        compiler_params=pltpu.CompilerParams(
            dimension_semantics=("parallel",), vmem_limit_bytes=VMEM_LIMIT),
        name="out_proj",
    )(o, w, x, g.reshape(1, D_MODEL))


def kernel(x, ffn_norm, ffn_w_in, ffn_w_out, mix_norm, diff_w_qkv, diff_lambda, diff_subln,
           diff_w_o, kv_norm, kv_w, moba_w_q, moba_w_o):
    b, s, d = x.shape
    assert b == 1 and d == D_MODEL and s % ATT_TQ == 0
    assert (s // MOBA_BLOCK) % SUM_ROWS == 0 and s // MOBA_BLOCK < PAIR
    assert ffn_w_in.shape[0] == 2 and diff_w_qkv.shape[0] == 1 and moba_w_q.shape[0] == 1
    tables = _rope_tables(s)
    w_in, w_out = ffn_w_in.astype(BF16), ffn_w_out.astype(BF16)

    def ffn(xx, layer, slot):
        return _ffn_half(xx, ffn_norm[layer, slot, 0], ffn_norm[layer, slot, 1],
                         w_in[layer, slot], w_out[layer, slot])

    xs = x.reshape(s, d)

    xs = ffn(xs, 0, 0)
    qt, k, vt = _project(xs, mix_norm[0, 0], diff_w_qkv[0].astype(BF16), tables, n_rope=2,
                         scale0=Q_SCALE, transposed=(True, False, True), with_kmean=False)
    o = _diff_attention(qt, k, vt, diff_lambda[0], diff_subln[0])
    xs = _out_proj(o, diff_w_o[0].astype(BF16), xs, mix_norm[0, 1])
    xs = ffn(xs, 0, 1)

    k, vt, kmean = _project(xs, kv_norm, kv_w.astype(BF16), tables, n_rope=1, scale0=1.0,
                            transposed=(False, True), with_kmean=True)
    kmean = kmean.reshape(s // MOBA_BLOCK, d)
    xs = ffn(xs, 1, 0)
    (qt,) = _project(xs, mix_norm[1, 0], moba_w_q[0].astype(BF16), tables, n_rope=1,
                     scale0=Q_SCALE, transposed=(True,), with_kmean=False)
    o = _moba_attention(qt, k, vt, kmean)
    xs = _out_proj(o, moba_w_o[0].astype(BF16), xs, mix_norm[1, 1])
    xs = ffn(xs, 1, 1)
    return xs.reshape(b, s, d)
```

```python
import functools
import math

import jax
import jax.numpy as jnp
from jax import lax
from jax.experimental import pallas as pl
from jax.experimental.pallas import tpu as pltpu

F32 = jnp.float32
BF16 = jnp.bfloat16

D_MODEL = 1024
HEAD_DIM = 64
PAIR = 2 * HEAD_DIM
N_PAIRS = D_MODEL // PAIR
D_FF = 2816
ROPE_THETA = 10000.0
MOBA_BLOCK = 256
MOBA_TOPK = 3
FFN_RESIDUAL_SCALE = 0.5
NORM_EPS = 1e-6
DIFF_LAMBDA_INIT = 0.2
MASK_NEG = -1e30
Q_SCALE = HEAD_DIM ** -0.5 * math.log2(math.e)

FFN_TM = 512
FFN_CW = 256
ATT_TK = 512
ATT_TQ = 2 * ATT_TK
ATT_CW = 256
SUM_ROWS = 16
VMEM_LIMIT = 56 * 1024 * 1024


def _rms(x, g):
    return x * lax.rsqrt(jnp.mean(x * x, axis=-1, keepdims=True) + NORM_EPS) * g


def _ffn_kernel(*refs, with_mixer_out):
    if with_mixer_out:
        attn_ref, wmix_ref, gmix_ref, *refs = refs
    x_ref, gpre_ref, gpost_ref, win_ref, wout_ref, o_ref, act_ref = refs
    x = x_ref[...]
    if with_mixer_out:
        mixed = jnp.dot(attn_ref[...], wmix_ref[...], preferred_element_type=F32)
        x = x + _rms(mixed, gmix_ref[...])
    h = _rms(x, gpre_ref[...]).astype(BF16)
    for c in range(D_FF // FFN_CW):
        lo = c * FFN_CW
        gate = jnp.dot(h, win_ref[:, lo:lo + FFN_CW], preferred_element_type=F32)
        up = jnp.dot(h, win_ref[:, D_FF + lo:D_FF + lo + FFN_CW], preferred_element_type=F32)
        act_ref[:, lo:lo + FFN_CW] = (gate * (1.0 / (1.0 + jnp.exp(-gate))) * up).astype(BF16)
    y = jnp.dot(act_ref[...], wout_ref[...], preferred_element_type=F32)
    o_ref[...] = x + FFN_RESIDUAL_SCALE * _rms(y, gpost_ref[...])


def _ffn_half(x, norms, w_in, w_out, layer, slot, mixer_out=None):
    s = x.shape[0]
    row = lambda i: (i, 0)
    const = lambda i: (0, 0)
    weights = lambda i: (layer, slot, 0, 0)
    resident = pl.Buffered(1)
    g_pre, g_post = norms[layer, slot, 0], norms[layer, slot, 1]
    operands = [x, g_pre.reshape(1, D_MODEL), g_post.reshape(1, D_MODEL), w_in, w_out]
    in_specs = [
        pl.BlockSpec((FFN_TM, D_MODEL), row),
        pl.BlockSpec((1, D_MODEL), const),
        pl.BlockSpec((1, D_MODEL), const),
        pl.BlockSpec((None, None, D_MODEL, 2 * D_FF), weights, pipeline_mode=resident),
        pl.BlockSpec((None, None, D_FF, D_MODEL), weights, pipeline_mode=resident),
    ]
    if mixer_out is not None:
        attn, w_o, g_mix = mixer_out
        operands = [attn, w_o, g_mix.reshape(1, D_MODEL)] + operands
        in_specs = [pl.BlockSpec((FFN_TM, D_MODEL), row),
                    pl.BlockSpec((D_MODEL, D_MODEL), const, pipeline_mode=resident),
                    pl.BlockSpec((1, D_MODEL), const)] + in_specs
    return pl.pallas_call(
        functools.partial(_ffn_kernel, with_mixer_out=mixer_out is not None),
        grid=(s // FFN_TM,),
        in_specs=in_specs,
        out_specs=pl.BlockSpec((FFN_TM, D_MODEL), row),
        out_shape=jax.ShapeDtypeStruct((s, D_MODEL), F32),
        scratch_shapes=[pltpu.VMEM((FFN_TM, D_FF), BF16)],
        compiler_params=pltpu.CompilerParams(
            dimension_semantics=("parallel",), vmem_limit_bytes=VMEM_LIMIT),
        name="ffn_half",
    )(*operands)


def _rope_tables(s):
    pos = jnp.arange(s, dtype=F32)
    inv_freq = 1.0 / (ROPE_THETA ** (jnp.arange(0, HEAD_DIM, 2, dtype=F32) / HEAD_DIM))
    ang = pos[:, None] * inv_freq[None, :]
    cos, sin = jnp.cos(ang)[:, None, :], jnp.sin(ang)[:, None, :]
    quarter = lambda coef: jnp.asarray(coef, F32).reshape(1, 4, 1)
    cos_t = (cos * quarter([1, 1, 1, 1])).reshape(s, PAIR)
    sin_lo = (sin * quarter([-1, 0, -1, 0])).reshape(s, PAIR)
    sin_hi = (sin * quarter([0, 1, 0, 1])).reshape(s, PAIR)
    return cos_t, sin_lo, sin_hi


def _proj_kernel(x_ref, g_ref, w_ref, cos_ref, sinlo_ref, sinhi_ref, *out_refs,
                 n_rope, scale0, transposed, with_kmean):
    n_chunks = len(transposed)
    h = _rms(x_ref[...], g_ref[...]).astype(BF16)
    tm = h.shape[0]
    reps = D_MODEL // PAIR
    cos = jnp.concatenate([cos_ref[...]] * reps, axis=1)
    sin_lo = jnp.concatenate([sinlo_ref[...]] * reps, axis=1)
    sin_hi = jnp.concatenate([sinhi_ref[...]] * reps, axis=1)
    for c in range(n_chunks):
        y = jnp.dot(h, w_ref[:, c * D_MODEL:(c + 1) * D_MODEL], preferred_element_type=F32)
        if c < n_rope:
            half = HEAD_DIM // 2
            y = (y * cos + pltpu.roll(y, D_MODEL - half, 1) * sin_lo
                 + pltpu.roll(y, half, 1) * sin_hi)
            if c == 0 and with_kmean:
                km = jnp.mean(y.reshape(tm // MOBA_BLOCK, MOBA_BLOCK, D_MODEL), axis=1)
                out_refs[n_chunks][0] = km
        if c == 0 and scale0 != 1.0:
            y = y * scale0
        if transposed[c]:
            out_refs[c][...] = y.T.reshape(N_PAIRS, 1, PAIR, tm).astype(BF16)
        else:
            out_refs[c][...] = y.astype(BF16)


def _project(x, g, w, tables, *, n_rope, scale0, transposed, with_kmean):
    s = x.shape[0]
    tm = ATT_TK
    row = lambda i: (i, 0)
    out_shape, out_specs = [], []
    for tr in transposed:
        if tr:
            out_shape.append(jax.ShapeDtypeStruct((N_PAIRS, s // tm, PAIR, tm), BF16))
            out_specs.append(pl.BlockSpec((N_PAIRS, 1, PAIR, tm), lambda i: (0, i, 0, 0)))
        else:
            out_shape.append(jax.ShapeDtypeStruct((s, D_MODEL), BF16))
            out_specs.append(pl.BlockSpec((tm, D_MODEL), row))
    if with_kmean:
        nb = tm // MOBA_BLOCK
        out_shape.append(jax.ShapeDtypeStruct((s // tm, nb, D_MODEL), F32))
        out_specs.append(pl.BlockSpec((1, nb, D_MODEL), lambda i: (i, 0, 0)))
    return pl.pallas_call(
        functools.partial(_proj_kernel, n_rope=n_rope, scale0=scale0, transposed=transposed,
                          with_kmean=with_kmean),
        grid=(s // tm,),
        in_specs=[
            pl.BlockSpec((tm, D_MODEL), row),
            pl.BlockSpec((1, D_MODEL), lambda i: (0, 0)),
            pl.BlockSpec(w.shape, lambda i: (0, 0)),
            pl.BlockSpec((tm, PAIR), row),
            pl.BlockSpec((tm, PAIR), row),
            pl.BlockSpec((tm, PAIR), row),
        ],
        out_specs=out_specs,
        out_shape=out_shape,
        compiler_params=pltpu.CompilerParams(
            dimension_semantics=("parallel",), vmem_limit_bytes=VMEM_LIMIT),
        name="project",
    )(x, g.reshape(1, D_MODEL), w, *tables)


def _stacked_queries_t(qt_ref):
    qt = jnp.concatenate([qt_ref[r] for r in range(qt_ref.shape[0])], axis=1)
    feat = lax.broadcasted_iota(jnp.int32, qt.shape, 0)
    zero = jnp.zeros_like(qt)
    return jnp.concatenate([jnp.where(feat < HEAD_DIM, qt, zero),
                            jnp.where(feat >= HEAD_DIM, qt, zero)], axis=1)


def _chunk_visibility(key_offset, c):
    if key_offset is None:
        return "all"
    q0 = (c * ATT_CW) % ATT_TQ
    if key_offset > q0 + ATT_CW - 1:
        return "none"
    if key_offset + ATT_TK - 1 <= q0:
        return "all"
    return "part"


def _score_chunk(k_ref, row0, q_ref, s_ref, mx_ref, key_offset, c):
    vis = _chunk_visibility(key_offset, c)
    if vis == "none":
        return
    cols = slice(c * ATT_CW, (c + 1) * ATT_CW)
    st = jnp.dot(k_ref[pl.ds(row0, ATT_TK), :], q_ref[:, cols], preferred_element_type=F32)
    if vis == "part":
        key = lax.broadcasted_iota(jnp.int32, st.shape, 0) + key_offset
        qpos = lax.broadcasted_iota(jnp.int32, st.shape, 1) + (c * ATT_CW) % ATT_TQ
        st = jnp.where(key <= qpos, st, -jnp.inf)
    s_ref[:, cols] = st
    mx_ref[:, cols] = jnp.max(st, axis=0, keepdims=True)


def _consume_chunk(s_ref, mx_ref, vt, m_ref, acc_ref, key_offset, c, split_heads):
    if _chunk_visibility(key_offset, c) == "none":
        return
    cols = slice(c * ATT_CW, (c + 1) * ATT_CW)
    m_prev = m_ref[:, cols]
    m_new = jnp.maximum(m_prev, mx_ref[:, cols])
    alpha = jnp.exp2(m_prev - m_new)
    p = jnp.exp2(s_ref[:, cols] - m_new).astype(BF16)
    if split_heads:
        vt = vt[0:HEAD_DIM] if c * ATT_CW < ATT_TQ else vt[HEAD_DIM:]
    vt_sum = jnp.concatenate([vt, jnp.ones((SUM_ROWS, vt.shape[1]), BF16)], axis=0)
    acc_ref[:, cols] = alpha * acc_ref[:, cols] + jnp.dot(vt_sum, p, preferred_element_type=F32)
    m_ref[:, cols] = m_new


def _run_kv_blocks(i, score_chunk, consume_chunk):
    def fused(sc, cs):
        for c in range(2 * ATT_TQ // ATT_CW):
            if sc is not None:
                score_chunk(*sc, c)
            if cs is not None:
                consume_chunk(*cs, c)

    @pl.when(i == 0)
    def _():
        fused((0, 0, 0), None)
        fused((1, 1, ATT_TK), (0, 0, 0))
        fused(None, (1, 1, ATT_TK))

    @pl.when(i > 0)
    def _():
        fused((0, 0, None), None)

        def pair(p, carry):
            b = 2 * p
            fused((b + 1, 1, None), (b, 0, None))
            fused((b + 2, 0, None), (b + 1, 1, None))
            return carry

        lax.fori_loop(0, i - 1, pair, 0)
        b = 2 * i - 2
        fused((b + 1, 1, None), (b, 0, None))
        fused((b + 2, 0, 0), (b + 1, 1, None))
        fused((b + 3, 1, ATT_TK), (b + 2, 0, 0))
        fused(None, (b + 3, 1, ATT_TK))


def _init_stats(m_ref, acc_ref):
    m_ref[...] = jnp.full(m_ref.shape, -jnp.inf, F32)
    acc_ref[...] = jnp.zeros(acc_ref.shape, F32)


def _normalized(acc_ref, lo, hi):
    rows = acc_ref.shape[0] - SUM_ROWS
    return acc_ref[0:rows, lo:hi] / acc_ref[rows:rows + 1, lo:hi]


def _pipeline_scratch(value_rows):
    cols = 2 * ATT_TQ
    return [pltpu.VMEM((ATT_TK, cols), F32), pltpu.VMEM((ATT_TK, cols), F32),
            pltpu.VMEM((1, cols), F32), pltpu.VMEM((1, cols), F32),
            pltpu.VMEM((1, cols), F32), pltpu.VMEM((value_rows + SUM_ROWS, cols), F32)]


def _attn_specs(s):
    qt_spec = pl.BlockSpec((None, ATT_TQ // ATT_TK, PAIR, ATT_TK), lambda h, i: (h, i, 0, 0))
    k_spec = pl.BlockSpec((s, PAIR), lambda h, i: (0, h))
    vt_spec = pl.BlockSpec((None, s // ATT_TK, PAIR, ATT_TK), lambda h, i: (h, 0, 0, 0))
    o_spec = pl.BlockSpec((ATT_TQ, PAIR), lambda h, i: (i, h))
    return qt_spec, k_spec, vt_spec, o_spec


def _diff_attn_kernel(qt_ref, k_ref, vt_ref, lam_ref, subln_ref, o_ref,
                      qs_ref, s0_ref, s1_ref, mx0_ref, mx1_ref, m_ref, acc_ref):
    tq, tk = ATT_TQ, ATT_TK
    i = pl.program_id(1)
    qs_ref[...] = _stacked_queries_t(qt_ref)
    _init_stats(m_ref, acc_ref)

    def score_chunk(j, slot, key_offset, c):
        _score_chunk(k_ref, pl.multiple_of(j * tk, tk), qs_ref, (s0_ref, s1_ref)[slot],
                     (mx0_ref, mx1_ref)[slot], key_offset, c)

    def consume_chunk(j, slot, key_offset, c):
        _consume_chunk((s0_ref, s1_ref)[slot], (mx0_ref, mx1_ref)[slot], vt_ref[j], m_ref, acc_ref,
                       key_offset, c, split_heads=False)

    _run_kv_blocks(i, score_chunk, consume_chunk)

    lv = lam_ref[...]
    lam = (jnp.exp(jnp.sum(lv[0:1] * lv[1:2], axis=-1, keepdims=True))
           - jnp.exp(jnp.sum(lv[2:3] * lv[3:4], axis=-1, keepdims=True)) + DIFF_LAMBDA_INIT)
    ot = _normalized(acc_ref, 0, tq) - lam * _normalized(acc_ref, tq, 2 * tq)
    o_ref[...] = (_rms(ot.T, subln_ref[...]) * (1.0 - DIFF_LAMBDA_INIT)).astype(BF16)


def _diff_attention(qt, k, vt, lam_vecs, subln):
    s = k.shape[0]
    qt_spec, k_spec, vt_spec, o_spec = _attn_specs(s)
    return pl.pallas_call(
        _diff_attn_kernel,
        grid=(N_PAIRS, s // ATT_TQ),
        in_specs=[qt_spec, k_spec, vt_spec,
                  pl.BlockSpec((4, HEAD_DIM), lambda h, i: (0, 0)),
                  pl.BlockSpec((1, PAIR), lambda h, i: (0, 0))],
        out_specs=o_spec,
        out_shape=jax.ShapeDtypeStruct((s, D_MODEL), BF16),
        scratch_shapes=[pltpu.VMEM((PAIR, 2 * ATT_TQ), BF16)] + _pipeline_scratch(PAIR),
        compiler_params=pltpu.CompilerParams(
            dimension_semantics=("parallel", "arbitrary"), vmem_limit_bytes=VMEM_LIMIT),
        name="diff_attn",
    )(qt, k, vt, lam_vecs, subln.reshape(1, PAIR))


def _moba_bias_t(gate, qblk):
    n = lax.broadcasted_iota(jnp.int32, gate.shape, 0)
    nf = n.astype(F32)
    past = n < qblk
    g = jnp.where(past, gate, -jnp.inf)
    picked = jnp.zeros(gate.shape, F32)
    for _ in range(MOBA_TOPK):
        mx = jnp.max(g, axis=0, keepdims=True)
        first = jnp.min(jnp.where(g == mx, nf, float(PAIR)), axis=0, keepdims=True)
        hit = nf == first
        picked = jnp.where(hit, 1.0, picked)
        g = jnp.where(hit, -jnp.inf, g)
    sel_past = jnp.where(past, jnp.where(picked > 0.0, 0.0, MASK_NEG), MASK_NEG)
    return jnp.where(n == qblk, 0.0, sel_past)


def _moba_attn_kernel(qt_ref, k_ref, vt_ref, kmean_ref, o_ref,
                      qa_ref, kaug_ref, s0_ref, s1_ref, mx0_ref, mx1_ref, m_ref, acc_ref):
    tq, tk = ATT_TQ, ATT_TK
    i = pl.program_id(1)
    s_len = k_ref.shape[0]

    @pl.when(i == 0)
    def _():
        kaug_ref[:, 0:PAIR] = k_ref[...]
        blk = lax.broadcasted_iota(jnp.int32, (s_len, PAIR), 0) // MOBA_BLOCK
        lane = lax.broadcasted_iota(jnp.int32, (s_len, PAIR), 1)
        kaug_ref[:, PAIR:] = jnp.where(blk == lane, 1.0, 0.0).astype(BF16)

    qs = _stacked_queries_t(qt_ref)
    gate, rest = None, kmean_ref[...]
    for _ in range(3):
        piece = rest.astype(BF16)
        part = jnp.dot(piece, qs, preferred_element_type=F32)
        gate = part if gate is None else gate + part
        rest = rest - piece.astype(F32)
    n_blocks = gate.shape[0]
    col = lax.broadcasted_iota(jnp.int32, (1, 2 * tq), 1)
    qblk = (i * tq + jnp.where(col >= tq, col - tq, col)) // MOBA_BLOCK
    qa_ref[0:PAIR, :] = qs
    qa_ref[PAIR:PAIR + n_blocks, :] = _moba_bias_t(gate, qblk).astype(BF16)
    qa_ref[PAIR + n_blocks:, :] = jnp.zeros((PAIR - n_blocks, 2 * tq), BF16)
    _init_stats(m_ref, acc_ref)

    def score_chunk(j, slot, key_offset, c):
        _score_chunk(kaug_ref, pl.multiple_of(j * tk, tk), qa_ref, (s0_ref, s1_ref)[slot],
                     (mx0_ref, mx1_ref)[slot], key_offset, c)

    def consume_chunk(j, slot, key_offset, c):
        _consume_chunk((s0_ref, s1_ref)[slot], (mx0_ref, mx1_ref)[slot], vt_ref[j], m_ref, acc_ref,
                       key_offset, c, split_heads=True)

    _run_kv_blocks(i, score_chunk, consume_chunk)

    ot = jnp.concatenate([_normalized(acc_ref, 0, tq), _normalized(acc_ref, tq, 2 * tq)], axis=0)
    o_ref[...] = ot.T.astype(BF16)


def _moba_attention(qt, k, vt, kmean):
    s = k.shape[0]
    qt_spec, k_spec, vt_spec, o_spec = _attn_specs(s)
    return pl.pallas_call(
        _moba_attn_kernel,
        grid=(N_PAIRS, s // ATT_TQ),
        in_specs=[qt_spec, k_spec, vt_spec,
                  pl.BlockSpec((s // MOBA_BLOCK, PAIR), lambda h, i: (0, h))],
        out_specs=o_spec,
        out_shape=jax.ShapeDtypeStruct((s, D_MODEL), BF16),
        scratch_shapes=[pltpu.VMEM((2 * PAIR, 2 * ATT_TQ), BF16),
                        pltpu.VMEM((s, 2 * PAIR), BF16)] + _pipeline_scratch(HEAD_DIM),
        compiler_params=pltpu.CompilerParams(
            dimension_semantics=("arbitrary", "arbitrary"), vmem_limit_bytes=VMEM_LIMIT),
        name="moba_attn",
    )(qt, k, vt, kmean)


def kernel(x, ffn_norm, ffn_w_in, ffn_w_out, mix_norm, diff_w_qkv, diff_lambda, diff_subln,
           diff_w_o, kv_norm, kv_w, moba_w_q, moba_w_o):
    b, s, d = x.shape
    assert b == 1 and d == D_MODEL and s % ATT_TQ == 0
    assert (s // MOBA_BLOCK) % SUM_ROWS == 0 and s // MOBA_BLOCK < PAIR
    assert ffn_w_in.shape[0] == 2 and diff_w_qkv.shape[0] == 1 and moba_w_q.shape[0] == 1
    tables = _rope_tables(s)
    w_in, w_out = ffn_w_in.astype(BF16), ffn_w_out.astype(BF16)

    def ffn(xx, layer, slot, mixer_out=None):
        return _ffn_half(xx, ffn_norm, w_in, w_out, layer, slot, mixer_out)

    xs = x.reshape(s, d)

    xs = ffn(xs, 0, 0)
    qt, k, vt = _project(xs, mix_norm[0, 0], diff_w_qkv[0].astype(BF16), tables, n_rope=2,
                         scale0=Q_SCALE, transposed=(True, False, True), with_kmean=False)
    o = _diff_attention(qt, k, vt, diff_lambda[0], diff_subln[0])
    xs = ffn(xs, 0, 1, mixer_out=(o, diff_w_o[0].astype(BF16), mix_norm[0, 1]))

    k, vt, kmean = _project(xs, kv_norm, kv_w.astype(BF16), tables, n_rope=1, scale0=1.0,
                            transposed=(False, True), with_kmean=True)
    kmean = kmean.reshape(s // MOBA_BLOCK, d)
    xs = ffn(xs, 1, 0)
    (qt,) = _project(xs, mix_norm[1, 0], moba_w_q[0].astype(BF16), tables, n_rope=1,
                     scale0=Q_SCALE, transposed=(True,), with_kmean=False)
    o = _moba_attention(qt, k, vt, kmean)
    xs = ffn(xs, 1, 1, mixer_out=(o, moba_w_o[0].astype(BF16), mix_norm[1, 1]))
    return xs.reshape(b, s, d)
```

```python
import functools
import math

import jax
import jax.numpy as jnp
from jax import lax
from jax.experimental import pallas as pl
from jax.experimental.pallas import tpu as pltpu

F32 = jnp.float32
BF16 = jnp.bfloat16

D_MODEL = 1024
HEAD_DIM = 64
PAIR = 2 * HEAD_DIM
N_PAIRS = D_MODEL // PAIR
D_FF = 2816
ROPE_THETA = 10000.0
MOBA_BLOCK = 256
MOBA_TOPK = 3
FFN_RESIDUAL_SCALE = 0.5
NORM_EPS = 1e-6
DIFF_LAMBDA_INIT = 0.2
MASK_NEG = -1e30
Q_SCALE = HEAD_DIM ** -0.5 * math.log2(math.e)

FFN_TM = 512
FFN_CW = 256
ATT_TK = 512
ATT_TQ = 2 * ATT_TK
ATT_CW = 512
SUM_ROWS = 16
VMEM_LIMIT = 56 * 1024 * 1024


def _rms(x, g):
    return x * lax.rsqrt(jnp.mean(x * x, axis=-1, keepdims=True) + NORM_EPS) * g


def _ffn_kernel(*refs, with_mixer_out):
    if with_mixer_out:
        attn_ref, wmix_ref, gmix_ref, *refs = refs
    x_ref, gpre_ref, gpost_ref, win_ref, wout_ref, o_ref, act_ref = refs
    x = x_ref[...]
    if with_mixer_out:
        mixed = jnp.dot(attn_ref[...], wmix_ref[...], preferred_element_type=F32)
        x = x + _rms(mixed, gmix_ref[...])
    h = _rms(x, gpre_ref[...]).astype(BF16)
    for c in range(D_FF // FFN_CW):
        lo = c * FFN_CW
        gate = jnp.dot(h, win_ref[:, lo:lo + FFN_CW], preferred_element_type=F32)
        up = jnp.dot(h, win_ref[:, D_FF + lo:D_FF + lo + FFN_CW], preferred_element_type=F32)
        act_ref[:, lo:lo + FFN_CW] = (gate * (1.0 / (1.0 + jnp.exp(-gate))) * up).astype(BF16)
    y = jnp.dot(act_ref[...], wout_ref[...], preferred_element_type=F32)
    o_ref[...] = x + FFN_RESIDUAL_SCALE * _rms(y, gpost_ref[...])


def _ffn_half(x, norms, w_in, w_out, layer, slot, mixer_out=None):
    s = x.shape[0]
    row = lambda i: (i, 0)
    const = lambda i: (0, 0)
    weights = lambda i: (layer, slot, 0, 0)
    resident = pl.Buffered(1)
    g_pre, g_post = norms[layer, slot, 0], norms[layer, slot, 1]
    operands = [x, g_pre.reshape(1, D_MODEL), g_post.reshape(1, D_MODEL), w_in, w_out]
    in_specs = [
        pl.BlockSpec((FFN_TM, D_MODEL), row),
        pl.BlockSpec((1, D_MODEL), const),
        pl.BlockSpec((1, D_MODEL), const),
        pl.BlockSpec((None, None, D_MODEL, 2 * D_FF), weights, pipeline_mode=resident),
        pl.BlockSpec((None, None, D_FF, D_MODEL), weights, pipeline_mode=resident),
    ]
    if mixer_out is not None:
        attn, w_o, g_mix = mixer_out
        operands = [attn, w_o, g_mix.reshape(1, D_MODEL)] + operands
        in_specs = [pl.BlockSpec((FFN_TM, D_MODEL), row),
                    pl.BlockSpec((D_MODEL, D_MODEL), const, pipeline_mode=resident),
                    pl.BlockSpec((1, D_MODEL), const)] + in_specs
    return pl.pallas_call(
        functools.partial(_ffn_kernel, with_mixer_out=mixer_out is not None),
        grid=(s // FFN_TM,),
        in_specs=in_specs,
        out_specs=pl.BlockSpec((FFN_TM, D_MODEL), row),
        out_shape=jax.ShapeDtypeStruct((s, D_MODEL), F32),
        scratch_shapes=[pltpu.VMEM((FFN_TM, D_FF), BF16)],
        compiler_params=pltpu.CompilerParams(
            dimension_semantics=("parallel",), vmem_limit_bytes=VMEM_LIMIT),
        name="ffn_half",
    )(*operands)


def _rope_tables(s):
    pos = jnp.arange(s, dtype=F32)
    inv_freq = 1.0 / (ROPE_THETA ** (jnp.arange(0, HEAD_DIM, 2, dtype=F32) / HEAD_DIM))
    ang = pos[:, None] * inv_freq[None, :]
    cos, sin = jnp.cos(ang)[:, None, :], jnp.sin(ang)[:, None, :]
    quarter = lambda coef: jnp.asarray(coef, F32).reshape(1, 4, 1)
    cos_t = (cos * quarter([1, 1, 1, 1])).reshape(s, PAIR)
    sin_lo = (sin * quarter([-1, 0, -1, 0])).reshape(s, PAIR)
    sin_hi = (sin * quarter([0, 1, 0, 1])).reshape(s, PAIR)
    return cos_t, sin_lo, sin_hi


def _proj_kernel(x_ref, g_ref, w_ref, cos_ref, sinlo_ref, sinhi_ref, *out_refs,
                 n_rope, scale0, transposed, with_kmean):
    n_chunks = len(transposed)
    h = _rms(x_ref[...], g_ref[...]).astype(BF16)
    tm = h.shape[0]
    reps = D_MODEL // PAIR
    cos = jnp.concatenate([cos_ref[...]] * reps, axis=1)
    sin_lo = jnp.concatenate([sinlo_ref[...]] * reps, axis=1)
    sin_hi = jnp.concatenate([sinhi_ref[...]] * reps, axis=1)
    for c in range(n_chunks):
        y = jnp.dot(h, w_ref[:, c * D_MODEL:(c + 1) * D_MODEL], preferred_element_type=F32)
        if c < n_rope:
            half = HEAD_DIM // 2
            y = (y * cos + pltpu.roll(y, D_MODEL - half, 1) * sin_lo
                 + pltpu.roll(y, half, 1) * sin_hi)
            if c == 0 and with_kmean:
                km = jnp.mean(y.reshape(tm // MOBA_BLOCK, MOBA_BLOCK, D_MODEL), axis=1)
                out_refs[n_chunks][0] = km
        if c == 0 and scale0 != 1.0:
            y = y * scale0
        if transposed[c]:
            out_refs[c][...] = y.T.reshape(N_PAIRS, 1, PAIR, tm).astype(BF16)
        else:
            out_refs[c][...] = y.astype(BF16)


def _project(x, g, w, tables, *, n_rope, scale0, transposed, with_kmean):
    s = x.shape[0]
    tm = ATT_TK
    row = lambda i: (i, 0)
    out_shape, out_specs = [], []
    for tr in transposed:
        if tr:
            out_shape.append(jax.ShapeDtypeStruct((N_PAIRS, s // tm, PAIR, tm), BF16))
            out_specs.append(pl.BlockSpec((N_PAIRS, 1, PAIR, tm), lambda i: (0, i, 0, 0)))
        else:
            out_shape.append(jax.ShapeDtypeStruct((s, D_MODEL), BF16))
            out_specs.append(pl.BlockSpec((tm, D_MODEL), row))
    if with_kmean:
        nb = tm // MOBA_BLOCK
        out_shape.append(jax.ShapeDtypeStruct((s // tm, nb, D_MODEL), F32))
        out_specs.append(pl.BlockSpec((1, nb, D_MODEL), lambda i: (i, 0, 0)))
    return pl.pallas_call(
        functools.partial(_proj_kernel, n_rope=n_rope, scale0=scale0, transposed=transposed,
                          with_kmean=with_kmean),
        grid=(s // tm,),
        in_specs=[
            pl.BlockSpec((tm, D_MODEL), row),
            pl.BlockSpec((1, D_MODEL), lambda i: (0, 0)),
            pl.BlockSpec(w.shape, lambda i: (0, 0)),
            pl.BlockSpec((tm, PAIR), row),
            pl.BlockSpec((tm, PAIR), row),
            pl.BlockSpec((tm, PAIR), row),
        ],
        out_specs=out_specs,
        out_shape=out_shape,
        compiler_params=pltpu.CompilerParams(
            dimension_semantics=("parallel",), vmem_limit_bytes=VMEM_LIMIT),
        name="project",
    )(x, g.reshape(1, D_MODEL), w, *tables)


def _stacked_queries_t(qt_ref):
    qt = jnp.concatenate([qt_ref[r] for r in range(qt_ref.shape[0])], axis=1)
    feat = lax.broadcasted_iota(jnp.int32, qt.shape, 0)
    zero = jnp.zeros_like(qt)
    return jnp.concatenate([jnp.where(feat < HEAD_DIM, qt, zero),
                            jnp.where(feat >= HEAD_DIM, qt, zero)], axis=1)


def _chunk_visibility(key_offset, c):
    if key_offset is None:
        return "all"
    q0 = (c * ATT_CW) % ATT_TQ
    if key_offset > q0 + ATT_CW - 1:
        return "none"
    if key_offset + ATT_TK - 1 <= q0:
        return "all"
    return "part"


def _score_chunk(k_ref, row0, q_ref, s_ref, mx_ref, key_offset, c):
    vis = _chunk_visibility(key_offset, c)
    if vis == "none":
        return
    cols = slice(c * ATT_CW, (c + 1) * ATT_CW)
    st = jnp.dot(k_ref[pl.ds(row0, ATT_TK), :], q_ref[:, cols], preferred_element_type=F32)
    if vis == "part":
        key = lax.broadcasted_iota(jnp.int32, st.shape, 0) + key_offset
        qpos = lax.broadcasted_iota(jnp.int32, st.shape, 1) + (c * ATT_CW) % ATT_TQ
        st = jnp.where(key <= qpos, st, -jnp.inf)
    s_ref[:, cols] = st
    mx_ref[:, cols] = jnp.max(st, axis=0, keepdims=True)


def _consume_chunk(s_ref, mx_ref, vt, m_ref, acc_ref, key_offset, c, split_heads):
    if _chunk_visibility(key_offset, c) == "none":
        return
    cols = slice(c * ATT_CW, (c + 1) * ATT_CW)
    m_prev = m_ref[:, cols]
    m_new = jnp.maximum(m_prev, mx_ref[:, cols])
    alpha = jnp.exp2(m_prev - m_new)
    p = jnp.exp2(s_ref[:, cols] - m_new).astype(BF16)
    if split_heads:
        vt = vt[0:HEAD_DIM] if c * ATT_CW < ATT_TQ else vt[HEAD_DIM:]
    vt_sum = jnp.concatenate([vt, jnp.ones((SUM_ROWS, vt.shape[1]), BF16)], axis=0)
    acc_ref[:, cols] = alpha * acc_ref[:, cols] + jnp.dot(vt_sum, p, preferred_element_type=F32)
    m_ref[:, cols] = m_new


def _run_kv_blocks(i, score_chunk, consume_chunk):
    def fused(sc, cs):
        for c in range(2 * ATT_TQ // ATT_CW):
            if sc is not None:
                score_chunk(*sc, c)
            if cs is not None:
                consume_chunk(*cs, c)

    @pl.when(i == 0)
    def _():
        fused((0, 0, 0), None)
        fused((1, 1, ATT_TK), (0, 0, 0))
        fused(None, (1, 1, ATT_TK))

    @pl.when(i > 0)
    def _():
        fused((0, 0, None), None)

        def pair(p, carry):
            b = 2 * p
            fused((b + 1, 1, None), (b, 0, None))
            fused((b + 2, 0, None), (b + 1, 1, None))
            return carry

        lax.fori_loop(0, i - 1, pair, 0)
        b = 2 * i - 2
        fused((b + 1, 1, None), (b, 0, None))
        fused((b + 2, 0, 0), (b + 1, 1, None))
        fused((b + 3, 1, ATT_TK), (b + 2, 0, 0))
        fused(None, (b + 3, 1, ATT_TK))


def _init_stats(m_ref, acc_ref):
    m_ref[...] = jnp.full(m_ref.shape, -jnp.inf, F32)
    acc_ref[...] = jnp.zeros(acc_ref.shape, F32)


def _normalized(acc_ref, lo, hi):
    rows = acc_ref.shape[0] - SUM_ROWS
    return acc_ref[0:rows, lo:hi] / acc_ref[rows:rows + 1, lo:hi]


def _pipeline_scratch(value_rows):
    cols = 2 * ATT_TQ
    return [pltpu.VMEM((ATT_TK, cols), F32), pltpu.VMEM((ATT_TK, cols), F32),
            pltpu.VMEM((1, cols), F32), pltpu.VMEM((1, cols), F32),
            pltpu.VMEM((1, cols), F32), pltpu.VMEM((value_rows + SUM_ROWS, cols), F32)]


def _attn_specs(s):
    qt_spec = pl.BlockSpec((None, ATT_TQ // ATT_TK, PAIR, ATT_TK), lambda h, i: (h, i, 0, 0))
    k_spec = pl.BlockSpec((s, PAIR), lambda h, i: (0, h))
    vt_spec = pl.BlockSpec((None, s // ATT_TK, PAIR, ATT_TK), lambda h, i: (h, 0, 0, 0))
    o_spec = pl.BlockSpec((ATT_TQ, PAIR), lambda h, i: (i, h))
    return qt_spec, k_spec, vt_spec, o_spec


def _diff_attn_kernel(qt_ref, k_ref, vt_ref, lam_ref, subln_ref, o_ref,
                      qs_ref, s0_ref, s1_ref, mx0_ref, mx1_ref, m_ref, acc_ref):
    tq, tk = ATT_TQ, ATT_TK
    i = pl.program_id(1)
    qs_ref[...] = _stacked_queries_t(qt_ref)
    _init_stats(m_ref, acc_ref)

    def score_chunk(j, slot, key_offset, c):
        _score_chunk(k_ref, pl.multiple_of(j * tk, tk), qs_ref, (s0_ref, s1_ref)[slot],
                     (mx0_ref, mx1_ref)[slot], key_offset, c)

    def consume_chunk(j, slot, key_offset, c):
        _consume_chunk((s0_ref, s1_ref)[slot], (mx0_ref, mx1_ref)[slot], vt_ref[j], m_ref, acc_ref,
                       key_offset, c, split_heads=False)

    _run_kv_blocks(i, score_chunk, consume_chunk)

    lv = lam_ref[...]
    lam = (jnp.exp(jnp.sum(lv[0:1] * lv[1:2], axis=-1, keepdims=True))
           - jnp.exp(jnp.sum(lv[2:3] * lv[3:4], axis=-1, keepdims=True)) + DIFF_LAMBDA_INIT)
    ot = _normalized(acc_ref, 0, tq) - lam * _normalized(acc_ref, tq, 2 * tq)
    o_ref[...] = (_rms(ot.T, subln_ref[...]) * (1.0 - DIFF_LAMBDA_INIT)).astype(BF16)


def _diff_attention(qt, k, vt, lam_vecs, subln):
    s = k.shape[0]
    qt_spec, k_spec, vt_spec, o_spec = _attn_specs(s)
    return pl.pallas_call(
        _diff_attn_kernel,
        grid=(N_PAIRS, s // ATT_TQ),
        in_specs=[qt_spec, k_spec, vt_spec,
                  pl.BlockSpec((4, HEAD_DIM), lambda h, i: (0, 0)),
                  pl.BlockSpec((1, PAIR), lambda h, i: (0, 0))],
        out_specs=o_spec,
        out_shape=jax.ShapeDtypeStruct((s, D_MODEL), BF16),
        scratch_shapes=[pltpu.VMEM((PAIR, 2 * ATT_TQ), BF16)] + _pipeline_scratch(PAIR),
        compiler_params=pltpu.CompilerParams(
            dimension_semantics=("parallel", "arbitrary"), vmem_limit_bytes=VMEM_LIMIT),
        name="diff_attn",
    )(qt, k, vt, lam_vecs, subln.reshape(1, PAIR))


def _moba_bias_t(gate, qblk):
    n = lax.broadcasted_iota(jnp.int32, gate.shape, 0)
    nf = n.astype(F32)
    past = n < qblk
    g = jnp.where(past, gate, -jnp.inf)
    picked = jnp.zeros(gate.shape, F32)
    for _ in range(MOBA_TOPK):
        mx = jnp.max(g, axis=0, keepdims=True)
        first = jnp.min(jnp.where(g == mx, nf, float(PAIR)), axis=0, keepdims=True)
        hit = nf == first
        picked = jnp.where(hit, 1.0, picked)
        g = jnp.where(hit, -jnp.inf, g)
    sel_past = jnp.where(past, jnp.where(picked > 0.0, 0.0, MASK_NEG), MASK_NEG)
    return jnp.where(n == qblk, 0.0, sel_past)


def _moba_attn_kernel(qt_ref, k_ref, vt_ref, kmean_ref, o_ref,
                      qa_ref, kaug_ref, s0_ref, s1_ref, mx0_ref, mx1_ref, m_ref, acc_ref):
    tq, tk = ATT_TQ, ATT_TK
    i = pl.program_id(1)
    s_len = k_ref.shape[0]

    @pl.when(i == 0)
    def _():
        kaug_ref[:, 0:PAIR] = k_ref[...]
        blk = lax.broadcasted_iota(jnp.int32, (s_len, PAIR), 0) // MOBA_BLOCK
        lane = lax.broadcasted_iota(jnp.int32, (s_len, PAIR), 1)
        kaug_ref[:, PAIR:] = jnp.where(blk == lane, 1.0, 0.0).astype(BF16)

    qs = _stacked_queries_t(qt_ref)
    gate, rest = None, kmean_ref[...]
    for _ in range(3):
        piece = rest.astype(BF16)
        part = jnp.dot(piece, qs, preferred_element_type=F32)
        gate = part if gate is None else gate + part
        rest = rest - piece.astype(F32)
    n_blocks = gate.shape[0]
    col = lax.broadcasted_iota(jnp.int32, (1, 2 * tq), 1)
    qblk = (i * tq + jnp.where(col >= tq, col - tq, col)) // MOBA_BLOCK
    qa_ref[0:PAIR, :] = qs
    qa_ref[PAIR:PAIR + n_blocks, :] = _moba_bias_t(gate, qblk).astype(BF16)
    qa_ref[PAIR + n_blocks:, :] = jnp.zeros((PAIR - n_blocks, 2 * tq), BF16)
    _init_stats(m_ref, acc_ref)

    def score_chunk(j, slot, key_offset, c):
        _score_chunk(kaug_ref, pl.multiple_of(j * tk, tk), qa_ref, (s0_ref, s1_ref)[slot],
                     (mx0_ref, mx1_ref)[slot], key_offset, c)

    def consume_chunk(j, slot, key_offset, c):
        _consume_chunk((s0_ref, s1_ref)[slot], (mx0_ref, mx1_ref)[slot], vt_ref[j], m_ref, acc_ref,
                       key_offset, c, split_heads=True)

    _run_kv_blocks(i, score_chunk, consume_chunk)

    ot = jnp.concatenate([_normalized(acc_ref, 0, tq), _normalized(acc_ref, tq, 2 * tq)], axis=0)
    o_ref[...] = ot.T.astype(BF16)


def _moba_attention(qt, k, vt, kmean):
    s = k.shape[0]
    qt_spec, k_spec, vt_spec, o_spec = _attn_specs(s)
    return pl.pallas_call(
        _moba_attn_kernel,
        grid=(N_PAIRS, s // ATT_TQ),
        in_specs=[qt_spec, k_spec, vt_spec,
                  pl.BlockSpec((s // MOBA_BLOCK, PAIR), lambda h, i: (0, h))],
        out_specs=o_spec,
        out_shape=jax.ShapeDtypeStruct((s, D_MODEL), BF16),
        scratch_shapes=[pltpu.VMEM((2 * PAIR, 2 * ATT_TQ), BF16),
                        pltpu.VMEM((s, 2 * PAIR), BF16)] + _pipeline_scratch(HEAD_DIM),
        compiler_params=pltpu.CompilerParams(
            dimension_semantics=("arbitrary", "arbitrary"), vmem_limit_bytes=VMEM_LIMIT),
        name="moba_attn",
    )(qt, k, vt, kmean)


def kernel(x, ffn_norm, ffn_w_in, ffn_w_out, mix_norm, diff_w_qkv, diff_lambda, diff_subln,
           diff_w_o, kv_norm, kv_w, moba_w_q, moba_w_o):
    b, s, d = x.shape
    assert b == 1 and d == D_MODEL and s % ATT_TQ == 0
    assert (s // MOBA_BLOCK) % SUM_ROWS == 0 and s // MOBA_BLOCK < PAIR
    assert ffn_w_in.shape[0] == 2 and diff_w_qkv.shape[0] == 1 and moba_w_q.shape[0] == 1
    tables = _rope_tables(s)
    w_in, w_out = ffn_w_in.astype(BF16), ffn_w_out.astype(BF16)

    def ffn(xx, layer, slot, mixer_out=None):
        return _ffn_half(xx, ffn_norm, w_in, w_out, layer, slot, mixer_out)

    xs = x.reshape(s, d)

    xs = ffn(xs, 0, 0)
    qt, k, vt = _project(xs, mix_norm[0, 0], diff_w_qkv[0].astype(BF16), tables, n_rope=2,
                         scale0=Q_SCALE, transposed=(True, False, True), with_kmean=False)
    o = _diff_attention(qt, k, vt, diff_lambda[0], diff_subln[0])
    xs = ffn(xs, 0, 1, mixer_out=(o, diff_w_o[0].astype(BF16), mix_norm[0, 1]))

    k, vt, kmean = _project(xs, kv_norm, kv_w.astype(BF16), tables, n_rope=1, scale0=1.0,
                            transposed=(False, True), with_kmean=True)
    kmean = kmean.reshape(s // MOBA_BLOCK, d)
    xs = ffn(xs, 1, 0)
    (qt,) = _project(xs, mix_norm[1, 0], moba_w_q[0].astype(BF16), tables, n_rope=1,
                     scale0=Q_SCALE, transposed=(True,), with_kmean=False)
    o = _moba_attention(qt, k, vt, kmean)
    xs = ffn(xs, 1, 1, mixer_out=(o, moba_w_o[0].astype(BF16), mix_norm[1, 1]))
    return xs.reshape(b, s, d)
```

```python
import functools
import math

import jax
import jax.numpy as jnp
from jax import lax
from jax.experimental import pallas as pl
from jax.experimental.pallas import tpu as pltpu

F32 = jnp.float32
BF16 = jnp.bfloat16

D_MODEL = 1024
HEAD_DIM = 64
PAIR = 2 * HEAD_DIM
N_PAIRS = D_MODEL // PAIR
D_FF = 2816
ROPE_THETA = 10000.0
MOBA_BLOCK = 256
MOBA_TOPK = 3
FFN_RESIDUAL_SCALE = 0.5
NORM_EPS = 1e-6
DIFF_LAMBDA_INIT = 0.2
MASK_NEG = -1e30
Q_SCALE = HEAD_DIM ** -0.5 * math.log2(math.e)

FFN_TM = 512
FFN_CW = 256
ATT_TK = 512
ATT_TQ = 2 * ATT_TK
DIFF_CW = 1024
MOBA_CW = 256
SUM_ROWS = 16
VMEM_LIMIT = 56 * 1024 * 1024


def _rms(x, g):
    return x * lax.rsqrt(jnp.mean(x * x, axis=-1, keepdims=True) + NORM_EPS) * g


def _ffn_kernel(*refs, with_mixer_out):
    if with_mixer_out:
        attn_ref, wmix_ref, gmix_ref, *refs = refs
    x_ref, gpre_ref, gpost_ref, win_ref, wout_ref, o_ref, act_ref = refs
    x = x_ref[...]
    if with_mixer_out:
        mixed = jnp.dot(attn_ref[...], wmix_ref[...], preferred_element_type=F32)
        x = x + _rms(mixed, gmix_ref[...])
    h = _rms(x, gpre_ref[...]).astype(BF16)
    for c in range(D_FF // FFN_CW):
        lo = c * FFN_CW
        gate = jnp.dot(h, win_ref[:, lo:lo + FFN_CW], preferred_element_type=F32)
        up = jnp.dot(h, win_ref[:, D_FF + lo:D_FF + lo + FFN_CW], preferred_element_type=F32)
        act_ref[:, lo:lo + FFN_CW] = (gate * (1.0 / (1.0 + jnp.exp(-gate))) * up).astype(BF16)
    y = jnp.dot(act_ref[...], wout_ref[...], preferred_element_type=F32)
    o_ref[...] = x + FFN_RESIDUAL_SCALE * _rms(y, gpost_ref[...])


def _ffn_half(x, norms, w_in, w_out, layer, slot, mixer_out=None):
    s = x.shape[0]
    row = lambda i: (i, 0)
    const = lambda i: (0, 0)
    weights = lambda i: (layer, slot, 0, 0)
    resident = pl.Buffered(1)
    g_pre, g_post = norms[layer, slot, 0], norms[layer, slot, 1]
    operands = [x, g_pre.reshape(1, D_MODEL), g_post.reshape(1, D_MODEL), w_in, w_out]
    in_specs = [
        pl.BlockSpec((FFN_TM, D_MODEL), row),
        pl.BlockSpec((1, D_MODEL), const),
        pl.BlockSpec((1, D_MODEL), const),
        pl.BlockSpec((None, None, D_MODEL, 2 * D_FF), weights, pipeline_mode=resident),
        pl.BlockSpec((None, None, D_FF, D_MODEL), weights, pipeline_mode=resident),
    ]
    if mixer_out is not None:
        attn, w_o, g_mix = mixer_out
        operands = [attn, w_o, g_mix.reshape(1, D_MODEL)] + operands
        in_specs = [pl.BlockSpec((FFN_TM, D_MODEL), row),
                    pl.BlockSpec((D_MODEL, D_MODEL), const, pipeline_mode=resident),
                    pl.BlockSpec((1, D_MODEL), const)] + in_specs
    return pl.pallas_call(
        functools.partial(_ffn_kernel, with_mixer_out=mixer_out is not None),
        grid=(s // FFN_TM,),
        in_specs=in_specs,
        out_specs=pl.BlockSpec((FFN_TM, D_MODEL), row),
        out_shape=jax.ShapeDtypeStruct((s, D_MODEL), F32),
        scratch_shapes=[pltpu.VMEM((FFN_TM, D_FF), BF16)],
        compiler_params=pltpu.CompilerParams(
            dimension_semantics=("parallel",), vmem_limit_bytes=VMEM_LIMIT),
        name="ffn_half",
    )(*operands)


def _rope_tables(s):
    pos = jnp.arange(s, dtype=F32)
    inv_freq = 1.0 / (ROPE_THETA ** (jnp.arange(0, HEAD_DIM, 2, dtype=F32) / HEAD_DIM))
    ang = pos[:, None] * inv_freq[None, :]
    cos, sin = jnp.cos(ang)[:, None, :], jnp.sin(ang)[:, None, :]
    quarter = lambda coef: jnp.asarray(coef, F32).reshape(1, 4, 1)
    cos_t = (cos * quarter([1, 1, 1, 1])).reshape(s, PAIR)
    sin_lo = (sin * quarter([-1, 0, -1, 0])).reshape(s, PAIR)
    sin_hi = (sin * quarter([0, 1, 0, 1])).reshape(s, PAIR)
    return cos_t, sin_lo, sin_hi


def _proj_kernel(x_ref, g_ref, w_ref, cos_ref, sinlo_ref, sinhi_ref, *out_refs,
                 n_rope, scale0, transposed, with_kmean):
    n_chunks = len(transposed)
    h = _rms(x_ref[...], g_ref[...]).astype(BF16)
    tm = h.shape[0]
    reps = D_MODEL // PAIR
    cos = jnp.concatenate([cos_ref[...]] * reps, axis=1)
    sin_lo = jnp.concatenate([sinlo_ref[...]] * reps, axis=1)
    sin_hi = jnp.concatenate([sinhi_ref[...]] * reps, axis=1)
    for c in range(n_chunks):
        y = jnp.dot(h, w_ref[:, c * D_MODEL:(c + 1) * D_MODEL], preferred_element_type=F32)
        if c < n_rope:
            half = HEAD_DIM // 2
            y = (y * cos + pltpu.roll(y, D_MODEL - half, 1) * sin_lo
                 + pltpu.roll(y, half, 1) * sin_hi)
            if c == 0 and with_kmean:
                km = jnp.mean(y.reshape(tm // MOBA_BLOCK, MOBA_BLOCK, D_MODEL), axis=1)
                out_refs[n_chunks][0] = km
        if c == 0 and scale0 != 1.0:
            y = y * scale0
        if transposed[c]:
            out_refs[c][...] = y.T.reshape(N_PAIRS, 1, PAIR, tm).astype(BF16)
        else:
            out_refs[c][...] = y.astype(BF16)


def _project(x, g, w, tables, *, n_rope, scale0, transposed, with_kmean):
    s = x.shape[0]
    tm = ATT_TK
    row = lambda i: (i, 0)
    out_shape, out_specs = [], []
    for tr in transposed:
        if tr:
            out_shape.append(jax.ShapeDtypeStruct((N_PAIRS, s // tm, PAIR, tm), BF16))
            out_specs.append(pl.BlockSpec((N_PAIRS, 1, PAIR, tm), lambda i: (0, i, 0, 0)))
        else:
            out_shape.append(jax.ShapeDtypeStruct((s, D_MODEL), BF16))
            out_specs.append(pl.BlockSpec((tm, D_MODEL), row))
    if with_kmean:
        nb = tm // MOBA_BLOCK
        out_shape.append(jax.ShapeDtypeStruct((s // tm, nb, D_MODEL), F32))
        out_specs.append(pl.BlockSpec((1, nb, D_MODEL), lambda i: (i, 0, 0)))
    return pl.pallas_call(
        functools.partial(_proj_kernel, n_rope=n_rope, scale0=scale0, transposed=transposed,
                          with_kmean=with_kmean),
        grid=(s // tm,),
        in_specs=[
            pl.BlockSpec((tm, D_MODEL), row),
            pl.BlockSpec((1, D_MODEL), lambda i: (0, 0)),
            pl.BlockSpec(w.shape, lambda i: (0, 0)),
            pl.BlockSpec((tm, PAIR), row),
            pl.BlockSpec((tm, PAIR), row),
            pl.BlockSpec((tm, PAIR), row),
        ],
        out_specs=out_specs,
        out_shape=out_shape,
        compiler_params=pltpu.CompilerParams(
            dimension_semantics=("parallel",), vmem_limit_bytes=VMEM_LIMIT),
        name="project",
    )(x, g.reshape(1, D_MODEL), w, *tables)


def _stacked_queries_t(qt_ref):
    qt = jnp.concatenate([qt_ref[r] for r in range(qt_ref.shape[0])], axis=1)
    feat = lax.broadcasted_iota(jnp.int32, qt.shape, 0)
    zero = jnp.zeros_like(qt)
    return jnp.concatenate([jnp.where(feat < HEAD_DIM, qt, zero),
                            jnp.where(feat >= HEAD_DIM, qt, zero)], axis=1)


def _chunk_visibility(key_offset, cols):
    if key_offset is None:
        return "all"
    q_first, q_last = cols.start % ATT_TQ, (cols.stop - 1) % ATT_TQ
    if key_offset > q_last:
        return "none"
    if key_offset + ATT_TK - 1 <= q_first:
        return "all"
    return "part"


def _score_chunk(k_ref, row0, q_ref, s_ref, mx_ref, key_offset, cols):
    vis = _chunk_visibility(key_offset, cols)
    if vis == "none":
        return
    st = jnp.dot(k_ref[pl.ds(row0, ATT_TK), :], q_ref[:, cols], preferred_element_type=F32)
    if vis == "part":
        key = lax.broadcasted_iota(jnp.int32, st.shape, 0) + key_offset
        qpos = lax.broadcasted_iota(jnp.int32, st.shape, 1) + cols.start % ATT_TQ
        st = jnp.where(key <= qpos, st, -jnp.inf)
    s_ref[:, cols] = st
    mx_ref[:, cols] = jnp.max(st, axis=0, keepdims=True)


def _consume_chunk(s_ref, mx_ref, vt, m_ref, acc_ref, key_offset, cols, split_heads):
    if _chunk_visibility(key_offset, cols) == "none":
        return
    m_prev = m_ref[:, cols]
    m_new = jnp.maximum(m_prev, mx_ref[:, cols])
    alpha = jnp.exp2(m_prev - m_new)
    p = jnp.exp2(s_ref[:, cols] - m_new).astype(BF16)
    if split_heads:
        vt = vt[0:HEAD_DIM] if cols.start < ATT_TQ else vt[HEAD_DIM:]
    vt_sum = jnp.concatenate([vt, jnp.ones((SUM_ROWS, vt.shape[1]), BF16)], axis=0)
    acc_ref[:, cols] = alpha * acc_ref[:, cols] + jnp.dot(vt_sum, p, preferred_element_type=F32)
    m_ref[:, cols] = m_new


def _run_kv_blocks(i, score_chunk, consume_chunk, chunk_width):
    def fused(sc, cs):
        for c0 in range(0, 2 * ATT_TQ, chunk_width):
            cols = slice(c0, c0 + chunk_width)
            if sc is not None:
                score_chunk(*sc, cols)
            if cs is not None:
                consume_chunk(*cs, cols)

    @pl.when(i == 0)
    def _():
        fused((0, 0, 0), None)
        fused((1, 1, ATT_TK), (0, 0, 0))
        fused(None, (1, 1, ATT_TK))

    @pl.when(i > 0)
    def _():
        fused((0, 0, None), None)

        def pair(p, carry):
            b = 2 * p
            fused((b + 1, 1, None), (b, 0, None))
            fused((b + 2, 0, None), (b + 1, 1, None))
            return carry

        lax.fori_loop(0, i - 1, pair, 0)
        b = 2 * i - 2
        fused((b + 1, 1, None), (b, 0, None))
        fused((b + 2, 0, 0), (b + 1, 1, None))
        fused((b + 3, 1, ATT_TK), (b + 2, 0, 0))
        fused(None, (b + 3, 1, ATT_TK))


def _init_stats(m_ref, acc_ref):
    m_ref[...] = jnp.full(m_ref.shape, -jnp.inf, F32)
    acc_ref[...] = jnp.zeros(acc_ref.shape, F32)


def _normalized(acc_ref, lo, hi):
    rows = acc_ref.shape[0] - SUM_ROWS
    return acc_ref[0:rows, lo:hi] / acc_ref[rows:rows + 1, lo:hi]


def _pipeline_scratch(value_rows):
    cols = 2 * ATT_TQ
    return [pltpu.VMEM((ATT_TK, cols), F32), pltpu.VMEM((ATT_TK, cols), F32),
            pltpu.VMEM((1, cols), F32), pltpu.VMEM((1, cols), F32),
            pltpu.VMEM((1, cols), F32), pltpu.VMEM((value_rows + SUM_ROWS, cols), F32)]


def _attn_specs(s):
    qt_spec = pl.BlockSpec((None, ATT_TQ // ATT_TK, PAIR, ATT_TK), lambda h, i: (h, i, 0, 0))
    k_spec = pl.BlockSpec((s, PAIR), lambda h, i: (0, h))
    vt_spec = pl.BlockSpec((None, s // ATT_TK, PAIR, ATT_TK), lambda h, i: (h, 0, 0, 0))
    o_spec = pl.BlockSpec((ATT_TQ, PAIR), lambda h, i: (i, h))
    return qt_spec, k_spec, vt_spec, o_spec


def _diff_attn_kernel(qt_ref, k_ref, vt_ref, lam_ref, subln_ref, o_ref,
                      qs_ref, s0_ref, s1_ref, mx0_ref, mx1_ref, m_ref, acc_ref):
    tq, tk = ATT_TQ, ATT_TK
    i = pl.program_id(1)
    qs_ref[...] = _stacked_queries_t(qt_ref)
    _init_stats(m_ref, acc_ref)

    def score_chunk(j, slot, key_offset, cols):
        _score_chunk(k_ref, pl.multiple_of(j * tk, tk), qs_ref, (s0_ref, s1_ref)[slot],
                     (mx0_ref, mx1_ref)[slot], key_offset, cols)

    def consume_chunk(j, slot, key_offset, cols):
        _consume_chunk((s0_ref, s1_ref)[slot], (mx0_ref, mx1_ref)[slot], vt_ref[j], m_ref, acc_ref,
                       key_offset, cols, split_heads=False)

    _run_kv_blocks(i, score_chunk, consume_chunk, DIFF_CW)

    lv = lam_ref[...]
    lam = (jnp.exp(jnp.sum(lv[0:1] * lv[1:2], axis=-1, keepdims=True))
           - jnp.exp(jnp.sum(lv[2:3] * lv[3:4], axis=-1, keepdims=True)) + DIFF_LAMBDA_INIT)
    ot = _normalized(acc_ref, 0, tq) - lam * _normalized(acc_ref, tq, 2 * tq)
    o_ref[...] = (_rms(ot.T, subln_ref[...]) * (1.0 - DIFF_LAMBDA_INIT)).astype(BF16)


def _diff_attention(qt, k, vt, lam_vecs, subln):
    s = k.shape[0]
    qt_spec, k_spec, vt_spec, o_spec = _attn_specs(s)
    return pl.pallas_call(
        _diff_attn_kernel,
        grid=(N_PAIRS, s // ATT_TQ),
        in_specs=[qt_spec, k_spec, vt_spec,
                  pl.BlockSpec((4, HEAD_DIM), lambda h, i: (0, 0)),
                  pl.BlockSpec((1, PAIR), lambda h, i: (0, 0))],
        out_specs=o_spec,
        out_shape=jax.ShapeDtypeStruct((s, D_MODEL), BF16),
        scratch_shapes=[pltpu.VMEM((PAIR, 2 * ATT_TQ), BF16)] + _pipeline_scratch(PAIR),
        compiler_params=pltpu.CompilerParams(
            dimension_semantics=("parallel", "arbitrary"), vmem_limit_bytes=VMEM_LIMIT),
        name="diff_attn",
    )(qt, k, vt, lam_vecs, subln.reshape(1, PAIR))


def _moba_bias_t(gate, qblk):
    n = lax.broadcasted_iota(jnp.int32, gate.shape, 0)
    nf = n.astype(F32)
    past = n < qblk
    g = jnp.where(past, gate, -jnp.inf)
    picked = jnp.zeros(gate.shape, F32)
    for _ in range(MOBA_TOPK):
        mx = jnp.max(g, axis=0, keepdims=True)
        first = jnp.min(jnp.where(g == mx, nf, float(PAIR)), axis=0, keepdims=True)
        hit = nf == first
        picked = jnp.where(hit, 1.0, picked)
        g = jnp.where(hit, -jnp.inf, g)
    sel_past = jnp.where(past, jnp.where(picked > 0.0, 0.0, MASK_NEG), MASK_NEG)
    return jnp.where(n == qblk, 0.0, sel_past)


def _moba_attn_kernel(qt_ref, k_ref, vt_ref, kmean_ref, o_ref,
                      qa_ref, kaug_ref, s0_ref, s1_ref, mx0_ref, mx1_ref, m_ref, acc_ref):
    tq, tk = ATT_TQ, ATT_TK
    i = pl.program_id(1)
    s_len = k_ref.shape[0]

    @pl.when(i == 0)
    def _():
        kaug_ref[:, 0:PAIR] = k_ref[...]
        blk = lax.broadcasted_iota(jnp.int32, (s_len, PAIR), 0) // MOBA_BLOCK
        lane = lax.broadcasted_iota(jnp.int32, (s_len, PAIR), 1)
        kaug_ref[:, PAIR:] = jnp.where(blk == lane, 1.0, 0.0).astype(BF16)

    qs = _stacked_queries_t(qt_ref)
    gate, rest = None, kmean_ref[...]
    for _ in range(3):
        piece = rest.astype(BF16)
        part = jnp.dot(piece, qs, preferred_element_type=F32)
        gate = part if gate is None else gate + part
        rest = rest - piece.astype(F32)
    n_blocks = gate.shape[0]
    col = lax.broadcasted_iota(jnp.int32, (1, 2 * tq), 1)
    qblk = (i * tq + jnp.where(col >= tq, col - tq, col)) // MOBA_BLOCK
    qa_ref[0:PAIR, :] = qs
    qa_ref[PAIR:PAIR + n_blocks, :] = _moba_bias_t(gate, qblk).astype(BF16)
    qa_ref[PAIR + n_blocks:, :] = jnp.zeros((PAIR - n_blocks, 2 * tq), BF16)
    _init_stats(m_ref, acc_ref)

    def score_chunk(j, slot, key_offset, cols):
        _score_chunk(kaug_ref, pl.multiple_of(j * tk, tk), qa_ref, (s0_ref, s1_ref)[slot],
                     (mx0_ref, mx1_ref)[slot], key_offset, cols)

    def consume_chunk(j, slot, key_offset, cols):
        _consume_chunk((s0_ref, s1_ref)[slot], (mx0_ref, mx1_ref)[slot], vt_ref[j], m_ref, acc_ref,
                       key_offset, cols, split_heads=True)

    _run_kv_blocks(i, score_chunk, consume_chunk, MOBA_CW)

    ot = jnp.concatenate([_normalized(acc_ref, 0, tq), _normalized(acc_ref, tq, 2 * tq)], axis=0)
    o_ref[...] = ot.T.astype(BF16)


def _moba_attention(qt, k, vt, kmean):
    s = k.shape[0]
    qt_spec, k_spec, vt_spec, o_spec = _attn_specs(s)
    return pl.pallas_call(
        _moba_attn_kernel,
        grid=(N_PAIRS, s // ATT_TQ),
        in_specs=[qt_spec, k_spec, vt_spec,
                  pl.BlockSpec((s // MOBA_BLOCK, PAIR), lambda h, i: (0, h))],
        out_specs=o_spec,
        out_shape=jax.ShapeDtypeStruct((s, D_MODEL), BF16),
        scratch_shapes=[pltpu.VMEM((2 * PAIR, 2 * ATT_TQ), BF16),
                        pltpu.VMEM((s, 2 * PAIR), BF16)] + _pipeline_scratch(HEAD_DIM),
        compiler_params=pltpu.CompilerParams(
            dimension_semantics=("arbitrary", "arbitrary"), vmem_limit_bytes=VMEM_LIMIT),
        name="moba_attn",
    )(qt, k, vt, kmean)


def kernel(x, ffn_norm, ffn_w_in, ffn_w_out, mix_norm, diff_w_qkv, diff_lambda, diff_subln,
           diff_w_o, kv_norm, kv_w, moba_w_q, moba_w_o):
    b, s, d = x.shape
    assert b == 1 and d == D_MODEL and s % ATT_TQ == 0
    assert (s // MOBA_BLOCK) % SUM_ROWS == 0 and s // MOBA_BLOCK < PAIR
    assert ffn_w_in.shape[0] == 2 and diff_w_qkv.shape[0] == 1 and moba_w_q.shape[0] == 1
    tables = _rope_tables(s)
    w_in, w_out = ffn_w_in.astype(BF16), ffn_w_out.astype(BF16)

    def ffn(xx, layer, slot, mixer_out=None):
        return _ffn_half(xx, ffn_norm, w_in, w_out, layer, slot, mixer_out)

    xs = x.reshape(s, d)

    xs = ffn(xs, 0, 0)
    qt, k, vt = _project(xs, mix_norm[0, 0], diff_w_qkv[0].astype(BF16), tables, n_rope=2,
                         scale0=Q_SCALE, transposed=(True, False, True), with_kmean=False)
    o = _diff_attention(qt, k, vt, diff_lambda[0], diff_subln[0])
    xs = ffn(xs, 0, 1, mixer_out=(o, diff_w_o[0].astype(BF16), mix_norm[0, 1]))

    k, vt, kmean = _project(xs, kv_norm, kv_w.astype(BF16), tables, n_rope=1, scale0=1.0,
                            transposed=(False, True), with_kmean=True)
    kmean = kmean.reshape(s // MOBA_BLOCK, d)
    xs = ffn(xs, 1, 0)
    (qt,) = _project(xs, mix_norm[1, 0], moba_w_q[0].astype(BF16), tables, n_rope=1,
                     scale0=Q_SCALE, transposed=(True,), with_kmean=False)
    o = _moba_attention(qt, k, vt, kmean)
    xs = ffn(xs, 1, 1, mixer_out=(o, moba_w_o[0].astype(BF16), mix_norm[1, 1]))
    return xs.reshape(b, s, d)
```

```python
import functools
import math

import jax
import jax.numpy as jnp
from jax import lax
from jax.experimental import pallas as pl
from jax.experimental.pallas import tpu as pltpu

F32 = jnp.float32
BF16 = jnp.bfloat16

D_MODEL = 1024
HEAD_DIM = 64
PAIR = 2 * HEAD_DIM
N_PAIRS = D_MODEL // PAIR
D_FF = 2816
ROPE_THETA = 10000.0
MOBA_BLOCK = 256
MOBA_TOPK = 3
FFN_RESIDUAL_SCALE = 0.5
NORM_EPS = 1e-6
DIFF_LAMBDA_INIT = 0.2
MASK_NEG = -1e30
Q_SCALE = HEAD_DIM ** -0.5 * math.log2(math.e)

FFN_TM = 512
FFN_CW = 256
ATT_TK = 512
ATT_TQ = 2 * ATT_TK
DIFF_CW = 512
MOBA_CW = 256
SUM_ROWS = 16
VMEM_LIMIT = 56 * 1024 * 1024


def _rms(x, g):
    return x * lax.rsqrt(jnp.mean(x * x, axis=-1, keepdims=True) + NORM_EPS) * g


def _ffn_kernel(*refs, with_mixer_out):
    if with_mixer_out:
        attn_ref, wmix_ref, gmix_ref, *refs = refs
    x_ref, gpre_ref, gpost_ref, win_ref, wout_ref, o_ref, act_ref = refs
    x = x_ref[...]
    if with_mixer_out:
        mixed = jnp.dot(attn_ref[...], wmix_ref[...], preferred_element_type=F32)
        x = x + _rms(mixed, gmix_ref[...])
    h = _rms(x, gpre_ref[...]).astype(BF16)
    for c in range(D_FF // FFN_CW):
        lo = c * FFN_CW
        gate = jnp.dot(h, win_ref[:, lo:lo + FFN_CW], preferred_element_type=F32)
        up = jnp.dot(h, win_ref[:, D_FF + lo:D_FF + lo + FFN_CW], preferred_element_type=F32)
        act_ref[:, lo:lo + FFN_CW] = (gate * (1.0 / (1.0 + jnp.exp(-gate))) * up).astype(BF16)
    y = jnp.dot(act_ref[...], wout_ref[...], preferred_element_type=F32)
    o_ref[...] = x + FFN_RESIDUAL_SCALE * _rms(y, gpost_ref[...])


def _ffn_half(x, norms, w_in, w_out, layer, slot, mixer_out=None):
    s = x.shape[0]
    row = lambda i: (i, 0)
    const = lambda i: (0, 0)
    weights = lambda i: (layer, slot, 0, 0)
    resident = pl.Buffered(1)
    g_pre, g_post = norms[layer, slot, 0], norms[layer, slot, 1]
    operands = [x, g_pre.reshape(1, D_MODEL), g_post.reshape(1, D_MODEL), w_in, w_out]
    in_specs = [
        pl.BlockSpec((FFN_TM, D_MODEL), row),
        pl.BlockSpec((1, D_MODEL), const),
        pl.BlockSpec((1, D_MODEL), const),
        pl.BlockSpec((None, None, D_MODEL, 2 * D_FF), weights, pipeline_mode=resident),
        pl.BlockSpec((None, None, D_FF, D_MODEL), weights, pipeline_mode=resident),
    ]
    if mixer_out is not None:
        attn, w_o, g_mix = mixer_out
        operands = [attn, w_o, g_mix.reshape(1, D_MODEL)] + operands
        in_specs = [pl.BlockSpec((FFN_TM, D_MODEL), row),
                    pl.BlockSpec((D_MODEL, D_MODEL), const, pipeline_mode=resident),
                    pl.BlockSpec((1, D_MODEL), const)] + in_specs
    return pl.pallas_call(
        functools.partial(_ffn_kernel, with_mixer_out=mixer_out is not None),
        grid=(s // FFN_TM,),
        in_specs=in_specs,
        out_specs=pl.BlockSpec((FFN_TM, D_MODEL), row),
        out_shape=jax.ShapeDtypeStruct((s, D_MODEL), F32),
        scratch_shapes=[pltpu.VMEM((FFN_TM, D_FF), BF16)],
        compiler_params=pltpu.CompilerParams(
            dimension_semantics=("parallel",), vmem_limit_bytes=VMEM_LIMIT),
        name="ffn_half",
    )(*operands)


def _rope_tables(s):
    pos = jnp.arange(s, dtype=F32)
    inv_freq = 1.0 / (ROPE_THETA ** (jnp.arange(0, HEAD_DIM, 2, dtype=F32) / HEAD_DIM))
    ang = pos[:, None] * inv_freq[None, :]
    cos, sin = jnp.cos(ang)[:, None, :], jnp.sin(ang)[:, None, :]
    quarter = lambda coef: jnp.asarray(coef, F32).reshape(1, 4, 1)
    cos_t = (cos * quarter([1, 1, 1, 1])).reshape(s, PAIR)
    sin_lo = (sin * quarter([-1, 0, -1, 0])).reshape(s, PAIR)
    sin_hi = (sin * quarter([0, 1, 0, 1])).reshape(s, PAIR)
    return cos_t, sin_lo, sin_hi


def _proj_kernel(x_ref, g_ref, w_ref, cos_ref, sinlo_ref, sinhi_ref, *out_refs,
                 n_rope, scale0, transposed, with_kmean):
    n_chunks = len(transposed)
    h = _rms(x_ref[...], g_ref[...]).astype(BF16)
    tm = h.shape[0]
    reps = D_MODEL // PAIR
    cos = jnp.concatenate([cos_ref[...]] * reps, axis=1)
    sin_lo = jnp.concatenate([sinlo_ref[...]] * reps, axis=1)
    sin_hi = jnp.concatenate([sinhi_ref[...]] * reps, axis=1)
    for c in range(n_chunks):
        y = jnp.dot(h, w_ref[:, c * D_MODEL:(c + 1) * D_MODEL], preferred_element_type=F32)
        if c < n_rope:
            half = HEAD_DIM // 2
            y = (y * cos + pltpu.roll(y, D_MODEL - half, 1) * sin_lo
                 + pltpu.roll(y, half, 1) * sin_hi)
            if c == 0 and with_kmean:
                km = jnp.mean(y.reshape(tm // MOBA_BLOCK, MOBA_BLOCK, D_MODEL), axis=1)
                out_refs[n_chunks][0] = km
        if c == 0 and scale0 != 1.0:
            y = y * scale0
        if transposed[c]:
            out_refs[c][...] = y.T.reshape(N_PAIRS, 1, PAIR, tm).astype(BF16)
        else:
            out_refs[c][...] = y.astype(BF16)


def _project(x, g, w, tables, *, n_rope, scale0, transposed, with_kmean):
    s = x.shape[0]
    tm = ATT_TK
    row = lambda i: (i, 0)
    out_shape, out_specs = [], []
    for tr in transposed:
        if tr:
            out_shape.append(jax.ShapeDtypeStruct((N_PAIRS, s // tm, PAIR, tm), BF16))
            out_specs.append(pl.BlockSpec((N_PAIRS, 1, PAIR, tm), lambda i: (0, i, 0, 0)))
        else:
            out_shape.append(jax.ShapeDtypeStruct((s, D_MODEL), BF16))
            out_specs.append(pl.BlockSpec((tm, D_MODEL), row))
    if with_kmean:
        nb = tm // MOBA_BLOCK
        out_shape.append(jax.ShapeDtypeStruct((s // tm, nb, D_MODEL), F32))
        out_specs.append(pl.BlockSpec((1, nb, D_MODEL), lambda i: (i, 0, 0)))
    return pl.pallas_call(
        functools.partial(_proj_kernel, n_rope=n_rope, scale0=scale0, transposed=transposed,
                          with_kmean=with_kmean),
        grid=(s // tm,),
        in_specs=[
            pl.BlockSpec((tm, D_MODEL), row),
            pl.BlockSpec((1, D_MODEL), lambda i: (0, 0)),
            pl.BlockSpec(w.shape, lambda i: (0, 0)),
            pl.BlockSpec((tm, PAIR), row),
            pl.BlockSpec((tm, PAIR), row),
            pl.BlockSpec((tm, PAIR), row),
        ],
        out_specs=out_specs,
        out_shape=out_shape,
        compiler_params=pltpu.CompilerParams(
            dimension_semantics=("parallel",), vmem_limit_bytes=VMEM_LIMIT),
        name="project",
    )(x, g.reshape(1, D_MODEL), w, *tables)


def _stacked_queries_t(qt_ref):
    qt = jnp.concatenate([qt_ref[r] for r in range(qt_ref.shape[0])], axis=1)
    feat = lax.broadcasted_iota(jnp.int32, qt.shape, 0)
    zero = jnp.zeros_like(qt)
    return jnp.concatenate([jnp.where(feat < HEAD_DIM, qt, zero),
                            jnp.where(feat >= HEAD_DIM, qt, zero)], axis=1)


def _chunk_visibility(key_offset, cols):
    if key_offset is None:
        return "all"
    q_first, q_last = cols.start % ATT_TQ, (cols.stop - 1) % ATT_TQ
    if key_offset > q_last:
        return "none"
    if key_offset + ATT_TK - 1 <= q_first:
        return "all"
    return "part"


def _score_chunk(k_ref, row0, q_ref, s_ref, mx_ref, key_offset, cols):
    vis = _chunk_visibility(key_offset, cols)
    if vis == "none":
        return
    st = jnp.dot(k_ref[pl.ds(row0, ATT_TK), :], q_ref[:, cols], preferred_element_type=F32)
    if vis == "part":
        key = lax.broadcasted_iota(jnp.int32, st.shape, 0) + key_offset
        qpos = lax.broadcasted_iota(jnp.int32, st.shape, 1) + cols.start % ATT_TQ
        st = jnp.where(key <= qpos, st, -jnp.inf)
    s_ref[:, cols] = st
    mx_ref[:, cols] = jnp.max(st, axis=0, keepdims=True)


def _consume_chunk(s_ref, mx_ref, vt, m_ref, acc_ref, key_offset, cols, split_heads):
    if _chunk_visibility(key_offset, cols) == "none":
        return
    m_prev = m_ref[:, cols]
    m_new = jnp.maximum(m_prev, mx_ref[:, cols])
    alpha = jnp.exp2(m_prev - m_new)
    p = jnp.exp2(s_ref[:, cols] - m_new).astype(BF16)
    if split_heads:
        vt = vt[0:HEAD_DIM] if cols.start < ATT_TQ else vt[HEAD_DIM:]
    vt_sum = jnp.concatenate([vt, jnp.ones((SUM_ROWS, vt.shape[1]), BF16)], axis=0)
    acc_ref[:, cols] = alpha * acc_ref[:, cols] + jnp.dot(vt_sum, p, preferred_element_type=F32)
    m_ref[:, cols] = m_new


def _run_kv_blocks(i, score_chunk, consume_chunk, chunk_width):
    def fused(sc, cs):
        for c0 in range(0, 2 * ATT_TQ, chunk_width):
            cols = slice(c0, c0 + chunk_width)
            if sc is not None:
                score_chunk(*sc, cols)
            if cs is not None:
                consume_chunk(*cs, cols)

    @pl.when(i == 0)
    def _():
        fused((0, 0, 0), None)
        fused((1, 1, ATT_TK), (0, 0, 0))
        fused(None, (1, 1, ATT_TK))

    @pl.when(i > 0)
    def _():
        fused((0, 0, None), None)

        def pair(p, carry):
            b = 2 * p
            fused((b + 1, 1, None), (b, 0, None))
            fused((b + 2, 0, None), (b + 1, 1, None))
            return carry

        lax.fori_loop(0, i - 1, pair, 0)
        b = 2 * i - 2
        fused((b + 1, 1, None), (b, 0, None))
        fused((b + 2, 0, 0), (b + 1, 1, None))
        fused((b + 3, 1, ATT_TK), (b + 2, 0, 0))
        fused(None, (b + 3, 1, ATT_TK))


def _init_stats(m_ref, acc_ref):
    m_ref[...] = jnp.full(m_ref.shape, -jnp.inf, F32)
    acc_ref[...] = jnp.zeros(acc_ref.shape, F32)


def _normalized(acc_ref, lo, hi):
    rows = acc_ref.shape[0] - SUM_ROWS
    return acc_ref[0:rows, lo:hi] / acc_ref[rows:rows + 1, lo:hi]


def _pipeline_scratch(value_rows):
    cols = 2 * ATT_TQ
    return [pltpu.VMEM((ATT_TK, cols), F32), pltpu.VMEM((ATT_TK, cols), F32),
            pltpu.VMEM((1, cols), F32), pltpu.VMEM((1, cols), F32),
            pltpu.VMEM((1, cols), F32), pltpu.VMEM((value_rows + SUM_ROWS, cols), F32)]


def _attn_specs(s):
    qt_spec = pl.BlockSpec((None, ATT_TQ // ATT_TK, PAIR, ATT_TK), lambda h, i: (h, i, 0, 0))
    k_spec = pl.BlockSpec((s, PAIR), lambda h, i: (0, h))
    vt_spec = pl.BlockSpec((None, s // ATT_TK, PAIR, ATT_TK), lambda h, i: (h, 0, 0, 0))
    o_spec = pl.BlockSpec((ATT_TQ, PAIR), lambda h, i: (i, h))
    return qt_spec, k_spec, vt_spec, o_spec


def _diff_attn_kernel(qt_ref, k_ref, vt_ref, lam_ref, subln_ref, o_ref,
                      qs_ref, s0_ref, s1_ref, mx0_ref, mx1_ref, m_ref, acc_ref):
    tq, tk = ATT_TQ, ATT_TK
    i = pl.program_id(1)
    qs_ref[...] = _stacked_queries_t(qt_ref)
    _init_stats(m_ref, acc_ref)

    def score_chunk(j, slot, key_offset, cols):
        _score_chunk(k_ref, pl.multiple_of(j * tk, tk), qs_ref, (s0_ref, s1_ref)[slot],
                     (mx0_ref, mx1_ref)[slot], key_offset, cols)

    def consume_chunk(j, slot, key_offset, cols):
        _consume_chunk((s0_ref, s1_ref)[slot], (mx0_ref, mx1_ref)[slot], vt_ref[j], m_ref, acc_ref,
                       key_offset, cols, split_heads=False)

    _run_kv_blocks(i, score_chunk, consume_chunk, DIFF_CW)

    lv = lam_ref[...]
    lam = (jnp.exp(jnp.sum(lv[0:1] * lv[1:2], axis=-1, keepdims=True))
           - jnp.exp(jnp.sum(lv[2:3] * lv[3:4], axis=-1, keepdims=True)) + DIFF_LAMBDA_INIT)
    ot = _normalized(acc_ref, 0, tq) - lam * _normalized(acc_ref, tq, 2 * tq)
    o_ref[...] = (_rms(ot.T, subln_ref[...]) * (1.0 - DIFF_LAMBDA_INIT)).astype(BF16)


def _diff_attention(qt, k, vt, lam_vecs, subln):
    s = k.shape[0]
    qt_spec, k_spec, vt_spec, o_spec = _attn_specs(s)
    return pl.pallas_call(
        _diff_attn_kernel,
        grid=(N_PAIRS, s // ATT_TQ),
        in_specs=[qt_spec, k_spec, vt_spec,
                  pl.BlockSpec((4, HEAD_DIM), lambda h, i: (0, 0)),
                  pl.BlockSpec((1, PAIR), lambda h, i: (0, 0))],
        out_specs=o_spec,
        out_shape=jax.ShapeDtypeStruct((s, D_MODEL), BF16),
        scratch_shapes=[pltpu.VMEM((PAIR, 2 * ATT_TQ), BF16)] + _pipeline_scratch(PAIR),
        compiler_params=pltpu.CompilerParams(
            dimension_semantics=("parallel", "arbitrary"), vmem_limit_bytes=VMEM_LIMIT),
        name="diff_attn",
    )(qt, k, vt, lam_vecs, subln.reshape(1, PAIR))


def _moba_bias_t(gate, qblk):
    n = lax.broadcasted_iota(jnp.int32, gate.shape, 0)
    nf = n.astype(F32)
    past = n < qblk
    g = jnp.where(past, gate, -jnp.inf)
    picked = jnp.zeros(gate.shape, F32)
    for _ in range(MOBA_TOPK):
        mx = jnp.max(g, axis=0, keepdims=True)
        first = jnp.min(jnp.where(g == mx, nf, float(PAIR)), axis=0, keepdims=True)
        hit = nf == first
        picked = jnp.where(hit, 1.0, picked)
        g = jnp.where(hit, -jnp.inf, g)
    sel_past = jnp.where(past, jnp.where(picked > 0.0, 0.0, MASK_NEG), MASK_NEG)
    return jnp.where(n == qblk, 0.0, sel_past)


def _moba_attn_kernel(qt_ref, k_ref, vt_ref, kmean_ref, o_ref,
                      qa_ref, kaug_ref, s0_ref, s1_ref, mx0_ref, mx1_ref, m_ref, acc_ref):
    tq, tk = ATT_TQ, ATT_TK
    i = pl.program_id(1)
    s_len = k_ref.shape[0]

    @pl.when(i == 0)
    def _():
        kaug_ref[:, 0:PAIR] = k_ref[...]
        blk = lax.broadcasted_iota(jnp.int32, (s_len, PAIR), 0) // MOBA_BLOCK
        lane = lax.broadcasted_iota(jnp.int32, (s_len, PAIR), 1)
        kaug_ref[:, PAIR:] = jnp.where(blk == lane, 1.0, 0.0).astype(BF16)

    qs = _stacked_queries_t(qt_ref)
    gate, rest = None, kmean_ref[...]
    for _ in range(3):
        piece = rest.astype(BF16)
        part = jnp.dot(piece, qs, preferred_element_type=F32)
        gate = part if gate is None else gate + part
        rest = rest - piece.astype(F32)
    n_blocks = gate.shape[0]
    col = lax.broadcasted_iota(jnp.int32, (1, 2 * tq), 1)
    qblk = (i * tq + jnp.where(col >= tq, col - tq, col)) // MOBA_BLOCK
    qa_ref[0:PAIR, :] = qs
    qa_ref[PAIR:PAIR + n_blocks, :] = _moba_bias_t(gate, qblk).astype(BF16)
    qa_ref[PAIR + n_blocks:, :] = jnp.zeros((PAIR - n_blocks, 2 * tq), BF16)
    _init_stats(m_ref, acc_ref)

    def score_chunk(j, slot, key_offset, cols):
        _score_chunk(kaug_ref, pl.multiple_of(j * tk, tk), qa_ref, (s0_ref, s1_ref)[slot],
                     (mx0_ref, mx1_ref)[slot], key_offset, cols)

    def consume_chunk(j, slot, key_offset, cols):
        _consume_chunk((s0_ref, s1_ref)[slot], (mx0_ref, mx1_ref)[slot], vt_ref[j], m_ref, acc_ref,
                       key_offset, cols, split_heads=True)

    _run_kv_blocks(i, score_chunk, consume_chunk, MOBA_CW)

    ot = jnp.concatenate([_normalized(acc_ref, 0, tq), _normalized(acc_ref, tq, 2 * tq)], axis=0)
    o_ref[...] = ot.T.astype(BF16)


def _moba_attention(qt, k, vt, kmean):
    s = k.shape[0]
    qt_spec, k_spec, vt_spec, o_spec = _attn_specs(s)
    return pl.pallas_call(
        _moba_attn_kernel,
        grid=(N_PAIRS, s // ATT_TQ),
        in_specs=[qt_spec, k_spec, vt_spec,
                  pl.BlockSpec((s // MOBA_BLOCK, PAIR), lambda h, i: (0, h))],
        out_specs=o_spec,
        out_shape=jax.ShapeDtypeStruct((s, D_MODEL), BF16),
        scratch_shapes=[pltpu.VMEM((2 * PAIR, 2 * ATT_TQ), BF16),
                        pltpu.VMEM((s, 2 * PAIR), BF16)] + _pipeline_scratch(HEAD_DIM),
        compiler_params=pltpu.CompilerParams(
            dimension_semantics=("arbitrary", "arbitrary"), vmem_limit_bytes=VMEM_LIMIT),
        name="moba_attn",
    )(qt, k, vt, kmean)


def kernel(x, ffn_norm, ffn_w_in, ffn_w_out, mix_norm, diff_w_qkv, diff_lambda, diff_subln,
           diff_w_o, kv_norm, kv_w, moba_w_q, moba_w_o):
    b, s, d = x.shape
    assert b == 1 and d == D_MODEL and s % ATT_TQ == 0
    assert (s // MOBA_BLOCK) % SUM_ROWS == 0 and s // MOBA_BLOCK < PAIR
    assert ffn_w_in.shape[0] == 2 and diff_w_qkv.shape[0] == 1 and moba_w_q.shape[0] == 1
    tables = _rope_tables(s)
    w_in, w_out = ffn_w_in.astype(BF16), ffn_w_out.astype(BF16)

    def ffn(xx, layer, slot, mixer_out=None):
        return _ffn_half(xx, ffn_norm, w_in, w_out, layer, slot, mixer_out)

    xs = x.reshape(s, d)

    xs = ffn(xs, 0, 0)
    qt, k, vt = _project(xs, mix_norm[0, 0], diff_w_qkv[0].astype(BF16), tables, n_rope=2,
                         scale0=Q_SCALE, transposed=(True, False, True), with_kmean=False)
    o = _diff_attention(qt, k, vt, diff_lambda[0], diff_subln[0])
    xs = ffn(xs, 0, 1, mixer_out=(o, diff_w_o[0].astype(BF16), mix_norm[0, 1]))

    k, vt, kmean = _project(xs, kv_norm, kv_w.astype(BF16), tables, n_rope=1, scale0=1.0,
                            transposed=(False, True), with_kmean=True)
    kmean = kmean.reshape(s // MOBA_BLOCK, d)
    xs = ffn(xs, 1, 0)
    (qt,) = _project(xs, mix_norm[1, 0], moba_w_q[0].astype(BF16), tables, n_rope=1,
                     scale0=Q_SCALE, transposed=(True,), with_kmean=False)
    o = _moba_attention(qt, k, vt, kmean)
    xs = ffn(xs, 1, 1, mixer_out=(o, moba_w_o[0].astype(BF16), mix_norm[1, 1]))
    return xs.reshape(b, s, d)
```

```python
import functools
import math

import jax
import jax.numpy as jnp
from jax import lax
from jax.experimental import pallas as pl
from jax.experimental.pallas import tpu as pltpu

F32 = jnp.float32
BF16 = jnp.bfloat16

D_MODEL = 1024
HEAD_DIM = 64
PAIR = 2 * HEAD_DIM
N_PAIRS = D_MODEL // PAIR
D_FF = 2816
ROPE_THETA = 10000.0
MOBA_BLOCK = 256
MOBA_TOPK = 3
FFN_RESIDUAL_SCALE = 0.5
NORM_EPS = 1e-6
DIFF_LAMBDA_INIT = 0.2
MASK_NEG = -1e30
Q_SCALE = HEAD_DIM ** -0.5 * math.log2(math.e)

FFN_TM = 512
FFN_CW = 256
ATT_TK = 512
ATT_TQ = 2 * ATT_TK
DIFF_CW = 512
MOBA_CW = 256
SUM_ROWS = 16
VMEM_LIMIT = 56 * 1024 * 1024


def _rms(x, g):
    return x * lax.rsqrt(jnp.mean(x * x, axis=-1, keepdims=True) + NORM_EPS) * g


def _ffn_kernel(*refs, with_mixer_out):
    if with_mixer_out:
        attn_ref, wmix_ref, gmix_ref, *refs = refs
    x_ref, gpre_ref, gpost_ref, win_ref, wout_ref, o_ref, act_ref = refs
    x = x_ref[...]
    if with_mixer_out:
        mixed = jnp.dot(attn_ref[...], wmix_ref[...], preferred_element_type=F32)
        x = x + _rms(mixed, gmix_ref[...])
    h = _rms(x, gpre_ref[...]).astype(BF16)
    for c in range(D_FF // FFN_CW):
        lo = c * FFN_CW
        gate = jnp.dot(h, win_ref[:, lo:lo + FFN_CW], preferred_element_type=F32)
        up = jnp.dot(h, win_ref[:, D_FF + lo:D_FF + lo + FFN_CW], preferred_element_type=F32)
        act_ref[:, lo:lo + FFN_CW] = (gate * (1.0 / (1.0 + jnp.exp(-gate))) * up).astype(BF16)
    y = jnp.dot(act_ref[...], wout_ref[...], preferred_element_type=F32)
    o_ref[...] = x + FFN_RESIDUAL_SCALE * _rms(y, gpost_ref[...])


def _ffn_half(x, norms, w_in, w_out, layer, slot, mixer_out=None):
    s = x.shape[0]
    row = lambda i: (i, 0)
    const = lambda i: (0, 0)
    weights = lambda i: (layer, slot, 0, 0)
    resident = pl.Buffered(1)
    g_pre, g_post = norms[layer, slot, 0], norms[layer, slot, 1]
    operands = [x, g_pre.reshape(1, D_MODEL), g_post.reshape(1, D_MODEL), w_in, w_out]
    in_specs = [
        pl.BlockSpec((FFN_TM, D_MODEL), row),
        pl.BlockSpec((1, D_MODEL), const),
        pl.BlockSpec((1, D_MODEL), const),
        pl.BlockSpec((None, None, D_MODEL, 2 * D_FF), weights, pipeline_mode=resident),
        pl.BlockSpec((None, None, D_FF, D_MODEL), weights, pipeline_mode=resident),
    ]
    if mixer_out is not None:
        attn, w_o, g_mix = mixer_out
        operands = [attn, w_o, g_mix.reshape(1, D_MODEL)] + operands
        in_specs = [pl.BlockSpec((FFN_TM, D_MODEL), row),
                    pl.BlockSpec((D_MODEL, D_MODEL), const, pipeline_mode=resident),
                    pl.BlockSpec((1, D_MODEL), const)] + in_specs
    return pl.pallas_call(
        functools.partial(_ffn_kernel, with_mixer_out=mixer_out is not None),
        grid=(s // FFN_TM,),
        in_specs=in_specs,
        out_specs=pl.BlockSpec((FFN_TM, D_MODEL), row),
        out_shape=jax.ShapeDtypeStruct((s, D_MODEL), F32),
        scratch_shapes=[pltpu.VMEM((FFN_TM, D_FF), BF16)],
        compiler_params=pltpu.CompilerParams(
            dimension_semantics=("parallel",), vmem_limit_bytes=VMEM_LIMIT),
        name="ffn_half",
    )(*operands)


def _rope_tables(s):
    pos = jnp.arange(s, dtype=F32)
    inv_freq = 1.0 / (ROPE_THETA ** (jnp.arange(0, HEAD_DIM, 2, dtype=F32) / HEAD_DIM))
    ang = pos[:, None] * inv_freq[None, :]
    cos, sin = jnp.cos(ang)[:, None, :], jnp.sin(ang)[:, None, :]
    quarter = lambda coef: jnp.asarray(coef, F32).reshape(1, 4, 1)
    cos_t = (cos * quarter([1, 1, 1, 1])).reshape(s, PAIR)
    sin_lo = (sin * quarter([-1, 0, -1, 0])).reshape(s, PAIR)
    sin_hi = (sin * quarter([0, 1, 0, 1])).reshape(s, PAIR)
    return cos_t, sin_lo, sin_hi


def _proj_kernel(x_ref, g_ref, w_ref, cos_ref, sinlo_ref, sinhi_ref, *out_refs,
                 n_rope, scale0, transposed, with_kmean):
    n_chunks = len(transposed)
    h = _rms(x_ref[...], g_ref[...]).astype(BF16)
    tm = h.shape[0]
    reps = D_MODEL // PAIR
    cos = jnp.concatenate([cos_ref[...]] * reps, axis=1)
    sin_lo = jnp.concatenate([sinlo_ref[...]] * reps, axis=1)
    sin_hi = jnp.concatenate([sinhi_ref[...]] * reps, axis=1)
    for c in range(n_chunks):
        y = jnp.dot(h, w_ref[:, c * D_MODEL:(c + 1) * D_MODEL], preferred_element_type=F32)
        if c < n_rope:
            half = HEAD_DIM // 2
            y = (y * cos + pltpu.roll(y, D_MODEL - half, 1) * sin_lo
                 + pltpu.roll(y, half, 1) * sin_hi)
            if c == 0 and with_kmean:
                km = jnp.mean(y.reshape(tm // MOBA_BLOCK, MOBA_BLOCK, D_MODEL), axis=1)
                out_refs[n_chunks][0] = km
        if c == 0 and scale0 != 1.0:
            y = y * scale0
        if transposed[c]:
            out_refs[c][...] = y.T.reshape(N_PAIRS, 1, PAIR, tm).astype(BF16)
        else:
            out_refs[c][...] = y.astype(BF16)


def _project(x, g, w, tables, *, n_rope, scale0, transposed, with_kmean):
    s = x.shape[0]
    tm = ATT_TK
    row = lambda i: (i, 0)
    out_shape, out_specs = [], []
    for tr in transposed:
        if tr:
            out_shape.append(jax.ShapeDtypeStruct((N_PAIRS, s // tm, PAIR, tm), BF16))
            out_specs.append(pl.BlockSpec((N_PAIRS, 1, PAIR, tm), lambda i: (0, i, 0, 0)))
        else:
            out_shape.append(jax.ShapeDtypeStruct((s, D_MODEL), BF16))
            out_specs.append(pl.BlockSpec((tm, D_MODEL), row))
    if with_kmean:
        nb = tm // MOBA_BLOCK
        out_shape.append(jax.ShapeDtypeStruct((s // tm, nb, D_MODEL), F32))
        out_specs.append(pl.BlockSpec((1, nb, D_MODEL), lambda i: (i, 0, 0)))
    return pl.pallas_call(
        functools.partial(_proj_kernel, n_rope=n_rope, scale0=scale0, transposed=transposed,
                          with_kmean=with_kmean),
        grid=(s // tm,),
        in_specs=[
            pl.BlockSpec((tm, D_MODEL), row),
            pl.BlockSpec((1, D_MODEL), lambda i: (0, 0)),
            pl.BlockSpec(w.shape, lambda i: (0, 0)),
            pl.BlockSpec((tm, PAIR), row),
            pl.BlockSpec((tm, PAIR), row),
            pl.BlockSpec((tm, PAIR), row),
        ],
        out_specs=out_specs,
        out_shape=out_shape,
        compiler_params=pltpu.CompilerParams(
            dimension_semantics=("parallel",), vmem_limit_bytes=VMEM_LIMIT),
        name="project",
    )(x, g.reshape(1, D_MODEL), w, *tables)


def _stacked_queries_t(qt_ref):
    qt = jnp.concatenate([qt_ref[r] for r in range(qt_ref.shape[0])], axis=1)
    feat = lax.broadcasted_iota(jnp.int32, qt.shape, 0)
    zero = jnp.zeros_like(qt)
    return jnp.concatenate([jnp.where(feat < HEAD_DIM, qt, zero),
                            jnp.where(feat >= HEAD_DIM, qt, zero)], axis=1)


def _chunk_visibility(key_offset, cols):
    if key_offset is None:
        return "all"
    q_first, q_last = cols.start % ATT_TQ, (cols.stop - 1) % ATT_TQ
    if key_offset > q_last:
        return "none"
    if key_offset + ATT_TK - 1 <= q_first:
        return "all"
    return "part"


def _score_chunk(k_ref, row0, q_ref, s_ref, mx_ref, key_offset, cols):
    vis = _chunk_visibility(key_offset, cols)
    if vis == "none":
        return
    st = jnp.dot(k_ref[pl.ds(row0, ATT_TK), :], q_ref[:, cols], preferred_element_type=F32)
    if vis == "part":
        key = lax.broadcasted_iota(jnp.int32, st.shape, 0) + key_offset
        qpos = lax.broadcasted_iota(jnp.int32, st.shape, 1) + cols.start % ATT_TQ
        st = jnp.where(key <= qpos, st, -jnp.inf)
    s_ref[:, cols] = st
    mx_ref[:, cols] = jnp.max(st, axis=0, keepdims=True)


def _consume_chunk(s_ref, mx_ref, vt, m_ref, acc_ref, key_offset, cols, split_heads):
    if _chunk_visibility(key_offset, cols) == "none":
        return
    m_prev = m_ref[:, cols]
    m_new = jnp.maximum(m_prev, mx_ref[:, cols])
    alpha = jnp.exp2(m_prev - m_new)
    p = jnp.exp2(s_ref[:, cols] - m_new).astype(BF16)
    if split_heads:
        vt = vt[0:HEAD_DIM] if cols.start < ATT_TQ else vt[HEAD_DIM:]
    vt_sum = jnp.concatenate([vt, jnp.ones((SUM_ROWS, vt.shape[1]), BF16)], axis=0)
    acc_ref[:, cols] = alpha * acc_ref[:, cols] + jnp.dot(vt_sum, p, preferred_element_type=F32)
    m_ref[:, cols] = m_new


def _run_kv_blocks(i, score_chunk, consume_chunk, chunk_width):
    def fused(sc, cs):
        for c0 in range(0, 2 * ATT_TQ, chunk_width):
            cols = slice(c0, c0 + chunk_width)
            if sc is not None:
                score_chunk(*sc, cols)
            if cs is not None:
                consume_chunk(*cs, cols)

    @pl.when(i == 0)
    def _():
        fused((0, 0, 0), None)
        fused((1, 1, ATT_TK), (0, 0, 0))
        fused(None, (1, 1, ATT_TK))

    @pl.when(i > 0)
    def _():
        fused((0, 0, None), None)

        def pair(b):
            fused((b + 1, 1, None), (b, 0, None))
            fused((b + 2, 0, None), (b + 1, 1, None))

        def two_pairs(q, carry):
            pair(4 * q)
            pair(4 * q + 2)
            return carry

        n_double = lax.shift_right_logical(i - 1, 1)
        lax.fori_loop(0, n_double, two_pairs, 0)

        @pl.when(lax.rem(i - 1, 2) == 1)
        def _():
            pair(4 * n_double)

        b = 2 * i - 2
        fused((b + 1, 1, None), (b, 0, None))
        fused((b + 2, 0, 0), (b + 1, 1, None))
        fused((b + 3, 1, ATT_TK), (b + 2, 0, 0))
        fused(None, (b + 3, 1, ATT_TK))


def _init_stats(m_ref, acc_ref):
    m_ref[...] = jnp.full(m_ref.shape, -jnp.inf, F32)
    acc_ref[...] = jnp.zeros(acc_ref.shape, F32)


def _normalized(acc_ref, lo, hi):
    rows = acc_ref.shape[0] - SUM_ROWS
    return acc_ref[0:rows, lo:hi] / acc_ref[rows:rows + 1, lo:hi]


def _pipeline_scratch(value_rows):
    cols = 2 * ATT_TQ
    return [pltpu.VMEM((ATT_TK, cols), F32), pltpu.VMEM((ATT_TK, cols), F32),
            pltpu.VMEM((1, cols), F32), pltpu.VMEM((1, cols), F32),
            pltpu.VMEM((1, cols), F32), pltpu.VMEM((value_rows + SUM_ROWS, cols), F32)]


def _attn_specs(s):
    qt_spec = pl.BlockSpec((None, ATT_TQ // ATT_TK, PAIR, ATT_TK), lambda h, i: (h, i, 0, 0))
    k_spec = pl.BlockSpec((s, PAIR), lambda h, i: (0, h))
    vt_spec = pl.BlockSpec((None, s // ATT_TK, PAIR, ATT_TK), lambda h, i: (h, 0, 0, 0))
    o_spec = pl.BlockSpec((ATT_TQ, PAIR), lambda h, i: (i, h))
    return qt_spec, k_spec, vt_spec, o_spec


def _diff_attn_kernel(qt_ref, k_ref, vt_ref, lam_ref, subln_ref, o_ref,
                      qs_ref, s0_ref, s1_ref, mx0_ref, mx1_ref, m_ref, acc_ref):
    tq, tk = ATT_TQ, ATT_TK
    i = pl.program_id(1)
    qs_ref[...] = _stacked_queries_t(qt_ref)
    _init_stats(m_ref, acc_ref)

    def score_chunk(j, slot, key_offset, cols):
        _score_chunk(k_ref, pl.multiple_of(j * tk, tk), qs_ref, (s0_ref, s1_ref)[slot],
                     (mx0_ref, mx1_ref)[slot], key_offset, cols)

    def consume_chunk(j, slot, key_offset, cols):
        _consume_chunk((s0_ref, s1_ref)[slot], (mx0_ref, mx1_ref)[slot], vt_ref[j], m_ref, acc_ref,
                       key_offset, cols, split_heads=False)

    _run_kv_blocks(i, score_chunk, consume_chunk, DIFF_CW)

    lv = lam_ref[...]
    lam = (jnp.exp(jnp.sum(lv[0:1] * lv[1:2], axis=-1, keepdims=True))
           - jnp.exp(jnp.sum(lv[2:3] * lv[3:4], axis=-1, keepdims=True)) + DIFF_LAMBDA_INIT)
    ot = _normalized(acc_ref, 0, tq) - lam * _normalized(acc_ref, tq, 2 * tq)
    o_ref[...] = (_rms(ot.T, subln_ref[...]) * (1.0 - DIFF_LAMBDA_INIT)).astype(BF16)


def _diff_attention(qt, k, vt, lam_vecs, subln):
    s = k.shape[0]
    qt_spec, k_spec, vt_spec, o_spec = _attn_specs(s)
    return pl.pallas_call(
        _diff_attn_kernel,
        grid=(N_PAIRS, s // ATT_TQ),
        in_specs=[qt_spec, k_spec, vt_spec,
                  pl.BlockSpec((4, HEAD_DIM), lambda h, i: (0, 0)),
                  pl.BlockSpec((1, PAIR), lambda h, i: (0, 0))],
        out_specs=o_spec,
        out_shape=jax.ShapeDtypeStruct((s, D_MODEL), BF16),
        scratch_shapes=[pltpu.VMEM((PAIR, 2 * ATT_TQ), BF16)] + _pipeline_scratch(PAIR),
        compiler_params=pltpu.CompilerParams(
            dimension_semantics=("parallel", "arbitrary"), vmem_limit_bytes=VMEM_LIMIT),
        name="diff_attn",
    )(qt, k, vt, lam_vecs, subln.reshape(1, PAIR))


def _moba_bias_t(gate, qblk):
    n = lax.broadcasted_iota(jnp.int32, gate.shape, 0)
    nf = n.astype(F32)
    past = n < qblk
    g = jnp.where(past, gate, -jnp.inf)
    picked = jnp.zeros(gate.shape, F32)
    for _ in range(MOBA_TOPK):
        mx = jnp.max(g, axis=0, keepdims=True)
        first = jnp.min(jnp.where(g == mx, nf, float(PAIR)), axis=0, keepdims=True)
        hit = nf == first
        picked = jnp.where(hit, 1.0, picked)
        g = jnp.where(hit, -jnp.inf, g)
    sel_past = jnp.where(past, jnp.where(picked > 0.0, 0.0, MASK_NEG), MASK_NEG)
    return jnp.where(n == qblk, 0.0, sel_past)


def _moba_attn_kernel(qt_ref, k_ref, vt_ref, kmean_ref, o_ref,
                      qa_ref, kaug_ref, s0_ref, s1_ref, mx0_ref, mx1_ref, m_ref, acc_ref):
    tq, tk = ATT_TQ, ATT_TK
    i = pl.program_id(1)
    s_len = k_ref.shape[0]

    @pl.when(i == 0)
    def _():
        kaug_ref[:, 0:PAIR] = k_ref[...]
        blk = lax.broadcasted_iota(jnp.int32, (s_len, PAIR), 0) // MOBA_BLOCK
        lane = lax.broadcasted_iota(jnp.int32, (s_len, PAIR), 1)
        kaug_ref[:, PAIR:] = jnp.where(blk == lane, 1.0, 0.0).astype(BF16)

    qs = _stacked_queries_t(qt_ref)
    gate, rest = None, kmean_ref[...]
    for _ in range(3):
        piece = rest.astype(BF16)
        part = jnp.dot(piece, qs, preferred_element_type=F32)
        gate = part if gate is None else gate + part
        rest = rest - piece.astype(F32)
    n_blocks = gate.shape[0]
    col = lax.broadcasted_iota(jnp.int32, (1, 2 * tq), 1)
    qblk = (i * tq + jnp.where(col >= tq, col - tq, col)) // MOBA_BLOCK
    qa_ref[0:PAIR, :] = qs
    qa_ref[PAIR:PAIR + n_blocks, :] = _moba_bias_t(gate, qblk).astype(BF16)
    qa_ref[PAIR + n_blocks:, :] = jnp.zeros((PAIR - n_blocks, 2 * tq), BF16)
    _init_stats(m_ref, acc_ref)

    def score_chunk(j, slot, key_offset, cols):
        _score_chunk(kaug_ref, pl.multiple_of(j * tk, tk), qa_ref, (s0_ref, s1_ref)[slot],
                     (mx0_ref, mx1_ref)[slot], key_offset, cols)

    def consume_chunk(j, slot, key_offset, cols):
        _consume_chunk((s0_ref, s1_ref)[slot], (mx0_ref, mx1_ref)[slot], vt_ref[j], m_ref, acc_ref,
                       key_offset, cols, split_heads=True)

    _run_kv_blocks(i, score_chunk, consume_chunk, MOBA_CW)

    ot = jnp.concatenate([_normalized(acc_ref, 0, tq), _normalized(acc_ref, tq, 2 * tq)], axis=0)
    o_ref[...] = ot.T.astype(BF16)


def _moba_attention(qt, k, vt, kmean):
    s = k.shape[0]
    qt_spec, k_spec, vt_spec, o_spec = _attn_specs(s)
    return pl.pallas_call(
        _moba_attn_kernel,
        grid=(N_PAIRS, s // ATT_TQ),
        in_specs=[qt_spec, k_spec, vt_spec,
                  pl.BlockSpec((s // MOBA_BLOCK, PAIR), lambda h, i: (0, h))],
        out_specs=o_spec,
        out_shape=jax.ShapeDtypeStruct((s, D_MODEL), BF16),
        scratch_shapes=[pltpu.VMEM((2 * PAIR, 2 * ATT_TQ), BF16),
                        pltpu.VMEM((s, 2 * PAIR), BF16)] + _pipeline_scratch(HEAD_DIM),
        compiler_params=pltpu.CompilerParams(
            dimension_semantics=("arbitrary", "arbitrary"), vmem_limit_bytes=VMEM_LIMIT),
        name="moba_attn",
    )(qt, k, vt, kmean)


def kernel(x, ffn_norm, ffn_w_in, ffn_w_out, mix_norm, diff_w_qkv, diff_lambda, diff_subln,
           diff_w_o, kv_norm, kv_w, moba_w_q, moba_w_o):
    b, s, d = x.shape
    assert b == 1 and d == D_MODEL and s % ATT_TQ == 0
    assert (s // MOBA_BLOCK) % SUM_ROWS == 0 and s // MOBA_BLOCK < PAIR
    assert ffn_w_in.shape[0] == 2 and diff_w_qkv.shape[0] == 1 and moba_w_q.shape[0] == 1
    tables = _rope_tables(s)
    w_in, w_out = ffn_w_in.astype(BF16), ffn_w_out.astype(BF16)

    def ffn(xx, layer, slot, mixer_out=None):
        return _ffn_half(xx, ffn_norm, w_in, w_out, layer, slot, mixer_out)

    xs = x.reshape(s, d)

    xs = ffn(xs, 0, 0)
    qt, k, vt = _project(xs, mix_norm[0, 0], diff_w_qkv[0].astype(BF16), tables, n_rope=2,
                         scale0=Q_SCALE, transposed=(True, False, True), with_kmean=False)
    o = _diff_attention(qt, k, vt, diff_lambda[0], diff_subln[0])
    xs = ffn(xs, 0, 1, mixer_out=(o, diff_w_o[0].astype(BF16), mix_norm[0, 1]))

    k, vt, kmean = _project(xs, kv_norm, kv_w.astype(BF16), tables, n_rope=1, scale0=1.0,
                            transposed=(False, True), with_kmean=True)
    kmean = kmean.reshape(s // MOBA_BLOCK, d)
    xs = ffn(xs, 1, 0)
    (qt,) = _project(xs, mix_norm[1, 0], moba_w_q[0].astype(BF16), tables, n_rope=1,
                     scale0=Q_SCALE, transposed=(True,), with_kmean=False)
    o = _moba_attention(qt, k, vt, kmean)
    xs = ffn(xs, 1, 1, mixer_out=(o, moba_w_o[0].astype(BF16), mix_norm[1, 1]))
    return xs.reshape(b, s, d)
```

```python
import functools
import math

import jax
import jax.numpy as jnp
from jax import lax
from jax.experimental import pallas as pl
from jax.experimental.pallas import tpu as pltpu

F32 = jnp.float32
BF16 = jnp.bfloat16

D_MODEL = 1024
HEAD_DIM = 64
PAIR = 2 * HEAD_DIM
N_PAIRS = D_MODEL // PAIR
D_FF = 2816
ROPE_THETA = 10000.0
MOBA_BLOCK = 256
MOBA_TOPK = 3
FFN_RESIDUAL_SCALE = 0.5
NORM_EPS = 1e-6
DIFF_LAMBDA_INIT = 0.2
MASK_NEG = -1e30
M_FLOOR = -1e29
Q_SCALE = HEAD_DIM ** -0.5 * math.log2(math.e)

FFN_TM = 512
FFN_CW = 256
ATT_TK = 512
ATT_TQ = 2 * ATT_TK
DIFF_CW = 512
MOBA_CW = 256
SUM_ROWS = 16
VMEM_LIMIT = 56 * 1024 * 1024


def _rms(x, g):
    return x * lax.rsqrt(jnp.mean(x * x, axis=-1, keepdims=True) + NORM_EPS) * g


def _ffn_kernel(*refs, with_mixer_out):
    if with_mixer_out:
        attn_ref, wmix_ref, gmix_ref, *refs = refs
    x_ref, gpre_ref, gpost_ref, win_ref, wout_ref, o_ref, act_ref = refs
    x = x_ref[...]
    if with_mixer_out:
        mixed = jnp.dot(attn_ref[...], wmix_ref[...], preferred_element_type=F32)
        x = x + _rms(mixed, gmix_ref[...])
    h = _rms(x, gpre_ref[...]).astype(BF16)
    for c in range(D_FF // FFN_CW):
        lo = c * FFN_CW
        gate = jnp.dot(h, win_ref[:, lo:lo + FFN_CW], preferred_element_type=F32)
        up = jnp.dot(h, win_ref[:, D_FF + lo:D_FF + lo + FFN_CW], preferred_element_type=F32)
        act_ref[:, lo:lo + FFN_CW] = (gate * (1.0 / (1.0 + jnp.exp(-gate))) * up).astype(BF16)
    y = jnp.dot(act_ref[...], wout_ref[...], preferred_element_type=F32)
    o_ref[...] = x + FFN_RESIDUAL_SCALE * _rms(y, gpost_ref[...])


def _ffn_half(x, norms, w_in, w_out, layer, slot, mixer_out=None):
    s = x.shape[0]
    row = lambda i: (i, 0)
    const = lambda i: (0, 0)
    weights = lambda i: (layer, slot, 0, 0)
    resident = pl.Buffered(1)
    g_pre, g_post = norms[layer, slot, 0], norms[layer, slot, 1]
    operands = [x, g_pre.reshape(1, D_MODEL), g_post.reshape(1, D_MODEL), w_in, w_out]
    in_specs = [
        pl.BlockSpec((FFN_TM, D_MODEL), row),
        pl.BlockSpec((1, D_MODEL), const),
        pl.BlockSpec((1, D_MODEL), const),
        pl.BlockSpec((None, None, D_MODEL, 2 * D_FF), weights, pipeline_mode=resident),
        pl.BlockSpec((None, None, D_FF, D_MODEL), weights, pipeline_mode=resident),
    ]
    if mixer_out is not None:
        attn, w_o, g_mix = mixer_out
        operands = [attn, w_o, g_mix.reshape(1, D_MODEL)] + operands
        in_specs = [pl.BlockSpec((FFN_TM, D_MODEL), row),
                    pl.BlockSpec((D_MODEL, D_MODEL), const, pipeline_mode=resident),
                    pl.BlockSpec((1, D_MODEL), const)] + in_specs
    return pl.pallas_call(
        functools.partial(_ffn_kernel, with_mixer_out=mixer_out is not None),
        grid=(s // FFN_TM,),
        in_specs=in_specs,
        out_specs=pl.BlockSpec((FFN_TM, D_MODEL), row),
        out_shape=jax.ShapeDtypeStruct((s, D_MODEL), F32),
        scratch_shapes=[pltpu.VMEM((FFN_TM, D_FF), BF16)],
        compiler_params=pltpu.CompilerParams(
            dimension_semantics=("parallel",), vmem_limit_bytes=VMEM_LIMIT),
        name="ffn_half",
    )(*operands)


def _rope_tables(s):
    pos = jnp.arange(s, dtype=F32)
    inv_freq = 1.0 / (ROPE_THETA ** (jnp.arange(0, HEAD_DIM, 2, dtype=F32) / HEAD_DIM))
    ang = pos[:, None] * inv_freq[None, :]
    cos, sin = jnp.cos(ang)[:, None, :], jnp.sin(ang)[:, None, :]
    quarter = lambda coef: jnp.asarray(coef, F32).reshape(1, 4, 1)
    cos_t = (cos * quarter([1, 1, 1, 1])).reshape(s, PAIR)
    sin_lo = (sin * quarter([-1, 0, -1, 0])).reshape(s, PAIR)
    sin_hi = (sin * quarter([0, 1, 0, 1])).reshape(s, PAIR)
    return cos_t, sin_lo, sin_hi


def _proj_kernel(x_ref, g_ref, w_ref, cos_ref, sinlo_ref, sinhi_ref, *out_refs,
                 n_rope, scale0, transposed, with_kmean):
    n_chunks = len(transposed)
    h = _rms(x_ref[...], g_ref[...]).astype(BF16)
    tm = h.shape[0]
    reps = D_MODEL // PAIR
    cos = jnp.concatenate([cos_ref[...]] * reps, axis=1)
    sin_lo = jnp.concatenate([sinlo_ref[...]] * reps, axis=1)
    sin_hi = jnp.concatenate([sinhi_ref[...]] * reps, axis=1)
    for c in range(n_chunks):
        y = jnp.dot(h, w_ref[:, c * D_MODEL:(c + 1) * D_MODEL], preferred_element_type=F32)
        if c < n_rope:
            half = HEAD_DIM // 2
            y = (y * cos + pltpu.roll(y, D_MODEL - half, 1) * sin_lo
                 + pltpu.roll(y, half, 1) * sin_hi)
            if c == 0 and with_kmean:
                km = jnp.mean(y.reshape(tm // MOBA_BLOCK, MOBA_BLOCK, D_MODEL), axis=1)
                out_refs[n_chunks][0] = km
        if c == 0 and scale0 != 1.0:
            y = y * scale0
        if transposed[c]:
            out_refs[c][...] = y.T.reshape(N_PAIRS, 1, PAIR, tm).astype(BF16)
        else:
            out_refs[c][...] = y.astype(BF16)


def _project(x, g, w, tables, *, n_rope, scale0, transposed, with_kmean):
    s = x.shape[0]
    tm = ATT_TK
    row = lambda i: (i, 0)
    out_shape, out_specs = [], []
    for tr in transposed:
        if tr:
            out_shape.append(jax.ShapeDtypeStruct((N_PAIRS, s // tm, PAIR, tm), BF16))
            out_specs.append(pl.BlockSpec((N_PAIRS, 1, PAIR, tm), lambda i: (0, i, 0, 0)))
        else:
            out_shape.append(jax.ShapeDtypeStruct((s, D_MODEL), BF16))
            out_specs.append(pl.BlockSpec((tm, D_MODEL), row))
    if with_kmean:
        nb = tm // MOBA_BLOCK
        out_shape.append(jax.ShapeDtypeStruct((s // tm, nb, D_MODEL), F32))
        out_specs.append(pl.BlockSpec((1, nb, D_MODEL), lambda i: (i, 0, 0)))
    return pl.pallas_call(
        functools.partial(_proj_kernel, n_rope=n_rope, scale0=scale0, transposed=transposed,
                          with_kmean=with_kmean),
        grid=(s // tm,),
        in_specs=[
            pl.BlockSpec((tm, D_MODEL), row),
            pl.BlockSpec((1, D_MODEL), lambda i: (0, 0)),
            pl.BlockSpec(w.shape, lambda i: (0, 0)),
            pl.BlockSpec((tm, PAIR), row),
            pl.BlockSpec((tm, PAIR), row),
            pl.BlockSpec((tm, PAIR), row),
        ],
        out_specs=out_specs,
        out_shape=out_shape,
        compiler_params=pltpu.CompilerParams(
            dimension_semantics=("parallel",), vmem_limit_bytes=VMEM_LIMIT),
        name="project",
    )(x, g.reshape(1, D_MODEL), w, *tables)


def _stacked_queries_t(qt_ref):
    qt = jnp.concatenate([qt_ref[r] for r in range(qt_ref.shape[0])], axis=1)
    feat = lax.broadcasted_iota(jnp.int32, qt.shape, 0)
    zero = jnp.zeros_like(qt)
    return jnp.concatenate([jnp.where(feat < HEAD_DIM, qt, zero),
                            jnp.where(feat >= HEAD_DIM, qt, zero)], axis=1)


def _chunk_visibility(key_offset, cols):
    if key_offset is None:
        return "all"
    q_first, q_last = cols.start % ATT_TQ, (cols.stop - 1) % ATT_TQ
    if key_offset > q_last:
        return "none"
    if key_offset + ATT_TK - 1 <= q_first:
        return "all"
    return "part"


def _block_bias_rows(bias, cols):
    bias_ref, blk0 = bias
    return bias_ref[pl.ds(blk0, 1), cols], bias_ref[pl.ds(blk0 + 1, 1), cols]


def _score_chunk(k_ref, row0, q_ref, s_ref, mx_ref, key_offset, cols, bias=None):
    vis = _chunk_visibility(key_offset, cols)
    if vis == "none":
        return
    st = jnp.dot(k_ref[pl.ds(row0, ATT_TK), :], q_ref[:, cols], preferred_element_type=F32)
    if vis == "part":
        key = lax.broadcasted_iota(jnp.int32, st.shape, 0) + key_offset
        qpos = lax.broadcasted_iota(jnp.int32, st.shape, 1) + cols.start % ATT_TQ
        st = jnp.where(key <= qpos, st, -jnp.inf)
    s_ref[:, cols] = st
    if bias is None:
        mx_ref[:, cols] = jnp.max(st, axis=0, keepdims=True)
    else:
        b_top, b_bot = _block_bias_rows(bias, cols)
        mx_ref[:, cols] = jnp.maximum(
            jnp.max(st[0:MOBA_BLOCK], axis=0, keepdims=True) + b_top,
            jnp.max(st[MOBA_BLOCK:], axis=0, keepdims=True) + b_bot)


def _consume_chunk(s_ref, mx_ref, vt, m_ref, acc_ref, key_offset, cols, split_heads, bias=None):
    if _chunk_visibility(key_offset, cols) == "none":
        return
    m_prev = m_ref[:, cols]
    m_new = jnp.maximum(m_prev, mx_ref[:, cols])
    alpha = jnp.exp2(m_prev - m_new)
    if bias is None:
        p = jnp.exp2(s_ref[:, cols] - m_new).astype(BF16)
    else:
        b_top, b_bot = _block_bias_rows(bias, cols)
        p = jnp.concatenate([jnp.exp2(s_ref[0:MOBA_BLOCK, cols] - (m_new - b_top)),
                             jnp.exp2(s_ref[MOBA_BLOCK:, cols] - (m_new - b_bot))],
                            axis=0).astype(BF16)
    if split_heads:
        vt = vt[0:HEAD_DIM] if cols.start < ATT_TQ else vt[HEAD_DIM:]
    vt_sum = jnp.concatenate([vt, jnp.ones((SUM_ROWS, vt.shape[1]), BF16)], axis=0)
    acc_ref[:, cols] = alpha * acc_ref[:, cols] + jnp.dot(vt_sum, p, preferred_element_type=F32)
    m_ref[:, cols] = m_new


def _run_kv_blocks(i, score_chunk, consume_chunk, chunk_width):
    def fused(sc, cs):
        for c0 in range(0, 2 * ATT_TQ, chunk_width):
            cols = slice(c0, c0 + chunk_width)
            if sc is not None:
                score_chunk(*sc, cols)
            if cs is not None:
                consume_chunk(*cs, cols)

    @pl.when(i == 0)
    def _():
        fused((0, 0, 0), None)
        fused((1, 1, ATT_TK), (0, 0, 0))
        fused(None, (1, 1, ATT_TK))

    @pl.when(i > 0)
    def _():
        fused((0, 0, None), None)

        def pair(b):
            fused((b + 1, 1, None), (b, 0, None))
            fused((b + 2, 0, None), (b + 1, 1, None))

        def two_pairs(q, carry):
            pair(4 * q)
            pair(4 * q + 2)
            return carry

        n_double = lax.shift_right_logical(i - 1, 1)
        lax.fori_loop(0, n_double, two_pairs, 0)

        @pl.when(lax.rem(i - 1, 2) == 1)
        def _():
            pair(4 * n_double)

        b = 2 * i - 2
        fused((b + 1, 1, None), (b, 0, None))
        fused((b + 2, 0, 0), (b + 1, 1, None))
        fused((b + 3, 1, ATT_TK), (b + 2, 0, 0))
        fused(None, (b + 3, 1, ATT_TK))


def _init_stats(m_ref, acc_ref, m_init):
    m_ref[...] = jnp.full(m_ref.shape, m_init, F32)
    acc_ref[...] = jnp.zeros(acc_ref.shape, F32)


def _normalized(acc_ref, lo, hi):
    rows = acc_ref.shape[0] - SUM_ROWS
    return acc_ref[0:rows, lo:hi] / acc_ref[rows:rows + 1, lo:hi]


def _pipeline_scratch(value_rows):
    cols = 2 * ATT_TQ
    return [pltpu.VMEM((ATT_TK, cols), F32), pltpu.VMEM((ATT_TK, cols), F32),
            pltpu.VMEM((1, cols), F32), pltpu.VMEM((1, cols), F32),
            pltpu.VMEM((1, cols), F32), pltpu.VMEM((value_rows + SUM_ROWS, cols), F32)]


def _attn_specs(s):
    qt_spec = pl.BlockSpec((None, ATT_TQ // ATT_TK, PAIR, ATT_TK), lambda h, i: (h, i, 0, 0))
    k_spec = pl.BlockSpec((s, PAIR), lambda h, i: (0, h))
    vt_spec = pl.BlockSpec((None, s // ATT_TK, PAIR, ATT_TK), lambda h, i: (h, 0, 0, 0))
    o_spec = pl.BlockSpec((ATT_TQ, PAIR), lambda h, i: (i, h))
    return qt_spec, k_spec, vt_spec, o_spec


def _diff_attn_kernel(qt_ref, k_ref, vt_ref, lam_ref, subln_ref, o_ref,
                      qs_ref, s0_ref, s1_ref, mx0_ref, mx1_ref, m_ref, acc_ref):
    tq, tk = ATT_TQ, ATT_TK
    i = pl.program_id(1)
    qs_ref[...] = _stacked_queries_t(qt_ref)
    _init_stats(m_ref, acc_ref, -jnp.inf)

    def score_chunk(j, slot, key_offset, cols):
        _score_chunk(k_ref, pl.multiple_of(j * tk, tk), qs_ref, (s0_ref, s1_ref)[slot],
                     (mx0_ref, mx1_ref)[slot], key_offset, cols)

    def consume_chunk(j, slot, key_offset, cols):
        _consume_chunk((s0_ref, s1_ref)[slot], (mx0_ref, mx1_ref)[slot], vt_ref[j], m_ref, acc_ref,
                       key_offset, cols, split_heads=False)

    _run_kv_blocks(i, score_chunk, consume_chunk, DIFF_CW)

    lv = lam_ref[...]
    lam = (jnp.exp(jnp.sum(lv[0:1] * lv[1:2], axis=-1, keepdims=True))
           - jnp.exp(jnp.sum(lv[2:3] * lv[3:4], axis=-1, keepdims=True)) + DIFF_LAMBDA_INIT)
    ot = _normalized(acc_ref, 0, tq) - lam * _normalized(acc_ref, tq, 2 * tq)
    o_ref[...] = (_rms(ot.T, subln_ref[...]) * (1.0 - DIFF_LAMBDA_INIT)).astype(BF16)


def _diff_attention(qt, k, vt, lam_vecs, subln):
    s = k.shape[0]
    qt_spec, k_spec, vt_spec, o_spec = _attn_specs(s)
    return pl.pallas_call(
        _diff_attn_kernel,
        grid=(N_PAIRS, s // ATT_TQ),
        in_specs=[qt_spec, k_spec, vt_spec,
                  pl.BlockSpec((4, HEAD_DIM), lambda h, i: (0, 0)),
                  pl.BlockSpec((1, PAIR), lambda h, i: (0, 0))],
        out_specs=o_spec,
        out_shape=jax.ShapeDtypeStruct((s, D_MODEL), BF16),
        scratch_shapes=[pltpu.VMEM((PAIR, 2 * ATT_TQ), BF16)] + _pipeline_scratch(PAIR),
        compiler_params=pltpu.CompilerParams(
            dimension_semantics=("parallel", "arbitrary"), vmem_limit_bytes=VMEM_LIMIT),
        name="diff_attn",
    )(qt, k, vt, lam_vecs, subln.reshape(1, PAIR))


def _moba_bias_t(gate, qblk):
    n = lax.broadcasted_iota(jnp.int32, gate.shape, 0)
    nf = n.astype(F32)
    past = n < qblk
    g = jnp.where(past, gate, -jnp.inf)
    picked = jnp.zeros(gate.shape, F32)
    for _ in range(MOBA_TOPK):
        mx = jnp.max(g, axis=0, keepdims=True)
        first = jnp.min(jnp.where(g == mx, nf, float(PAIR)), axis=0, keepdims=True)
        hit = nf == first
        picked = jnp.where(hit, 1.0, picked)
        g = jnp.where(hit, -jnp.inf, g)
    sel_past = jnp.where(past, jnp.where(picked > 0.0, 0.0, MASK_NEG), MASK_NEG)
    return jnp.where(n == qblk, 0.0, sel_past)


def _moba_attn_kernel(qt_ref, k_ref, vt_ref, kmean_ref, o_ref,
                      qs_ref, bias_ref, s0_ref, s1_ref, mx0_ref, mx1_ref, m_ref, acc_ref):
    tq, tk = ATT_TQ, ATT_TK
    i = pl.program_id(1)

    qs = _stacked_queries_t(qt_ref)
    gate, rest = None, kmean_ref[...]
    for _ in range(3):
        piece = rest.astype(BF16)
        part = jnp.dot(piece, qs, preferred_element_type=F32)
        gate = part if gate is None else gate + part
        rest = rest - piece.astype(F32)
    col = lax.broadcasted_iota(jnp.int32, (1, 2 * tq), 1)
    qblk = (i * tq + jnp.where(col >= tq, col - tq, col)) // MOBA_BLOCK
    qs_ref[...] = qs
    bias_ref[...] = _moba_bias_t(gate, qblk)
    _init_stats(m_ref, acc_ref, M_FLOOR)

    def block_bias(j):
        return bias_ref, j * (tk // MOBA_BLOCK)

    def score_chunk(j, slot, key_offset, cols):
        _score_chunk(k_ref, pl.multiple_of(j * tk, tk), qs_ref, (s0_ref, s1_ref)[slot],
                     (mx0_ref, mx1_ref)[slot], key_offset, cols, bias=block_bias(j))

    def consume_chunk(j, slot, key_offset, cols):
        _consume_chunk((s0_ref, s1_ref)[slot], (mx0_ref, mx1_ref)[slot], vt_ref[j], m_ref, acc_ref,
                       key_offset, cols, split_heads=True, bias=block_bias(j))

    _run_kv_blocks(i, score_chunk, consume_chunk, MOBA_CW)

    ot = jnp.concatenate([_normalized(acc_ref, 0, tq), _normalized(acc_ref, tq, 2 * tq)], axis=0)
    o_ref[...] = ot.T.astype(BF16)


def _moba_attention(qt, k, vt, kmean):
    s = k.shape[0]
    qt_spec, k_spec, vt_spec, o_spec = _attn_specs(s)
    return pl.pallas_call(
        _moba_attn_kernel,
        grid=(N_PAIRS, s // ATT_TQ),
        in_specs=[qt_spec, k_spec, vt_spec,
                  pl.BlockSpec((s // MOBA_BLOCK, PAIR), lambda h, i: (0, h))],
        out_specs=o_spec,
        out_shape=jax.ShapeDtypeStruct((s, D_MODEL), BF16),
        scratch_shapes=[pltpu.VMEM((PAIR, 2 * ATT_TQ), BF16),
                        pltpu.VMEM((s // MOBA_BLOCK, 2 * ATT_TQ), F32)] + _pipeline_scratch(HEAD_DIM),
        compiler_params=pltpu.CompilerParams(
            dimension_semantics=("parallel", "arbitrary"), vmem_limit_bytes=VMEM_LIMIT),
        name="moba_attn",
    )(qt, k, vt, kmean)


def kernel(x, ffn_norm, ffn_w_in, ffn_w_out, mix_norm, diff_w_qkv, diff_lambda, diff_subln,
           diff_w_o, kv_norm, kv_w, moba_w_q, moba_w_o):
    b, s, d = x.shape
    assert b == 1 and d == D_MODEL and s % ATT_TQ == 0
    assert ATT_TK == 2 * MOBA_BLOCK and (s // MOBA_BLOCK) % 8 == 0
    assert ffn_w_in.shape[0] == 2 and diff_w_qkv.shape[0] == 1 and moba_w_q.shape[0] == 1
    tables = _rope_tables(s)
    w_in, w_out = ffn_w_in.astype(BF16), ffn_w_out.astype(BF16)

    def ffn(xx, layer, slot, mixer_out=None):
        return _ffn_half(xx, ffn_norm, w_in, w_out, layer, slot, mixer_out)

    xs = x.reshape(s, d)

    xs = ffn(xs, 0, 0)
    qt, k, vt = _project(xs, mix_norm[0, 0], diff_w_qkv[0].astype(BF16), tables, n_rope=2,
                         scale0=Q_SCALE, transposed=(True, False, True), with_kmean=False)
    o = _diff_attention(qt, k, vt, diff_lambda[0], diff_subln[0])
    xs = ffn(xs, 0, 1, mixer_out=(o, diff_w_o[0].astype(BF16), mix_norm[0, 1]))

    k, vt, kmean = _project(xs, kv_norm, kv_w.astype(BF16), tables, n_rope=1, scale0=1.0,
                            transposed=(False, True), with_kmean=True)
    kmean = kmean.reshape(s // MOBA_BLOCK, d)
    xs = ffn(xs, 1, 0)
    (qt,) = _project(xs, mix_norm[1, 0], moba_w_q[0].astype(BF16), tables, n_rope=1,
                     scale0=Q_SCALE, transposed=(True,), with_kmean=False)
    o = _moba_attention(qt, k, vt, kmean)
    xs = ffn(xs, 1, 1, mixer_out=(o, moba_w_o[0].astype(BF16), mix_norm[1, 1]))
    return xs.reshape(b, s, d)
```

```python
import functools
import math

import jax
import jax.numpy as jnp
from jax import lax
from jax.experimental import pallas as pl
from jax.experimental.pallas import tpu as pltpu

F32 = jnp.float32
BF16 = jnp.bfloat16

D_MODEL = 1024
HEAD_DIM = 64
PAIR = 2 * HEAD_DIM
N_PAIRS = D_MODEL // PAIR
D_FF = 2816
ROPE_THETA = 10000.0
MOBA_BLOCK = 256
MOBA_TOPK = 3
FFN_RESIDUAL_SCALE = 0.5
NORM_EPS = 1e-6
DIFF_LAMBDA_INIT = 0.2
MASK_NEG = -1e30
M_FLOOR = -1e29
Q_SCALE = HEAD_DIM ** -0.5 * math.log2(math.e)

FFN_TM = 512
FFN_CW = 256
ATT_TK = 512
ATT_TQ = 2 * ATT_TK
DIFF_CW = 512
MOBA_CW = 512
SUM_ROWS = 16
VMEM_LIMIT = 56 * 1024 * 1024


def _rms(x, g):
    return x * lax.rsqrt(jnp.mean(x * x, axis=-1, keepdims=True) + NORM_EPS) * g


def _ffn_kernel(*refs, with_mixer_out):
    if with_mixer_out:
        attn_ref, wmix_ref, gmix_ref, *refs = refs
    x_ref, gpre_ref, gpost_ref, win_ref, wout_ref, o_ref, act_ref = refs
    x = x_ref[...]
    if with_mixer_out:
        mixed = jnp.dot(attn_ref[...], wmix_ref[...], preferred_element_type=F32)
        x = x + _rms(mixed, gmix_ref[...])
    h = _rms(x, gpre_ref[...]).astype(BF16)
    for c in range(D_FF // FFN_CW):
        lo = c * FFN_CW
        gate = jnp.dot(h, win_ref[:, lo:lo + FFN_CW], preferred_element_type=F32)
        up = jnp.dot(h, win_ref[:, D_FF + lo:D_FF + lo + FFN_CW], preferred_element_type=F32)
        act_ref[:, lo:lo + FFN_CW] = (gate * (1.0 / (1.0 + jnp.exp(-gate))) * up).astype(BF16)
    y = jnp.dot(act_ref[...], wout_ref[...], preferred_element_type=F32)
    o_ref[...] = x + FFN_RESIDUAL_SCALE * _rms(y, gpost_ref[...])


def _ffn_half(x, norms, w_in, w_out, layer, slot, mixer_out=None):
    s = x.shape[0]
    row = lambda i: (i, 0)
    const = lambda i: (0, 0)
    weights = lambda i: (layer, slot, 0, 0)
    resident = pl.Buffered(1)
    g_pre, g_post = norms[layer, slot, 0], norms[layer, slot, 1]
    operands = [x, g_pre.reshape(1, D_MODEL), g_post.reshape(1, D_MODEL), w_in, w_out]
    in_specs = [
        pl.BlockSpec((FFN_TM, D_MODEL), row),
        pl.BlockSpec((1, D_MODEL), const),
        pl.BlockSpec((1, D_MODEL), const),
        pl.BlockSpec((None, None, D_MODEL, 2 * D_FF), weights, pipeline_mode=resident),
        pl.BlockSpec((None, None, D_FF, D_MODEL), weights, pipeline_mode=resident),
    ]
    if mixer_out is not None:
        attn, w_o, g_mix = mixer_out
        operands = [attn, w_o, g_mix.reshape(1, D_MODEL)] + operands
        in_specs = [pl.BlockSpec((FFN_TM, D_MODEL), row),
                    pl.BlockSpec((D_MODEL, D_MODEL), const, pipeline_mode=resident),
                    pl.BlockSpec((1, D_MODEL), const)] + in_specs
    return pl.pallas_call(
        functools.partial(_ffn_kernel, with_mixer_out=mixer_out is not None),
        grid=(s // FFN_TM,),
        in_specs=in_specs,
        out_specs=pl.BlockSpec((FFN_TM, D_MODEL), row),
        out_shape=jax.ShapeDtypeStruct((s, D_MODEL), F32),
        scratch_shapes=[pltpu.VMEM((FFN_TM, D_FF), BF16)],
        compiler_params=pltpu.CompilerParams(
            dimension_semantics=("parallel",), vmem_limit_bytes=VMEM_LIMIT),
        name="ffn_half",
    )(*operands)


def _rope_tables(s):
    pos = jnp.arange(s, dtype=F32)
    inv_freq = 1.0 / (ROPE_THETA ** (jnp.arange(0, HEAD_DIM, 2, dtype=F32) / HEAD_DIM))
    ang = pos[:, None] * inv_freq[None, :]
    cos, sin = jnp.cos(ang)[:, None, :], jnp.sin(ang)[:, None, :]
    quarter = lambda coef: jnp.asarray(coef, F32).reshape(1, 4, 1)
    cos_t = (cos * quarter([1, 1, 1, 1])).reshape(s, PAIR)
    sin_lo = (sin * quarter([-1, 0, -1, 0])).reshape(s, PAIR)
    sin_hi = (sin * quarter([0, 1, 0, 1])).reshape(s, PAIR)
    return cos_t, sin_lo, sin_hi


def _proj_kernel(x_ref, g_ref, w_ref, cos_ref, sinlo_ref, sinhi_ref, *out_refs,
                 n_rope, scale0, transposed, with_kmean):
    n_chunks = len(transposed)
    h = _rms(x_ref[...], g_ref[...]).astype(BF16)
    tm = h.shape[0]
    reps = D_MODEL // PAIR
    cos = jnp.concatenate([cos_ref[...]] * reps, axis=1)
    sin_lo = jnp.concatenate([sinlo_ref[...]] * reps, axis=1)
    sin_hi = jnp.concatenate([sinhi_ref[...]] * reps, axis=1)
    for c in range(n_chunks):
        y = jnp.dot(h, w_ref[:, c * D_MODEL:(c + 1) * D_MODEL], preferred_element_type=F32)
        if c < n_rope:
            half = HEAD_DIM // 2
            y = (y * cos + pltpu.roll(y, D_MODEL - half, 1) * sin_lo
                 + pltpu.roll(y, half, 1) * sin_hi)
            if c == 0 and with_kmean:
                km = jnp.mean(y.reshape(tm // MOBA_BLOCK, MOBA_BLOCK, D_MODEL), axis=1)
                out_refs[n_chunks][0] = km
        if c == 0 and scale0 != 1.0:
            y = y * scale0
        if transposed[c]:
            out_refs[c][...] = y.T.reshape(N_PAIRS, 1, PAIR, tm).astype(BF16)
        else:
            out_refs[c][...] = y.astype(BF16)


def _project(x, g, w, tables, *, n_rope, scale0, transposed, with_kmean):
    s = x.shape[0]
    tm = ATT_TK
    row = lambda i: (i, 0)
    out_shape, out_specs = [], []
    for tr in transposed:
        if tr:
            out_shape.append(jax.ShapeDtypeStruct((N_PAIRS, s // tm, PAIR, tm), BF16))
            out_specs.append(pl.BlockSpec((N_PAIRS, 1, PAIR, tm), lambda i: (0, i, 0, 0)))
        else:
            out_shape.append(jax.ShapeDtypeStruct((s, D_MODEL), BF16))
            out_specs.append(pl.BlockSpec((tm, D_MODEL), row))
    if with_kmean:
        nb = tm // MOBA_BLOCK
        out_shape.append(jax.ShapeDtypeStruct((s // tm, nb, D_MODEL), F32))
        out_specs.append(pl.BlockSpec((1, nb, D_MODEL), lambda i: (i, 0, 0)))
    return pl.pallas_call(
        functools.partial(_proj_kernel, n_rope=n_rope, scale0=scale0, transposed=transposed,
                          with_kmean=with_kmean),
        grid=(s // tm,),
        in_specs=[
            pl.BlockSpec((tm, D_MODEL), row),
            pl.BlockSpec((1, D_MODEL), lambda i: (0, 0)),
            pl.BlockSpec(w.shape, lambda i: (0, 0)),
            pl.BlockSpec((tm, PAIR), row),
            pl.BlockSpec((tm, PAIR), row),
            pl.BlockSpec((tm, PAIR), row),
        ],
        out_specs=out_specs,
        out_shape=out_shape,
        compiler_params=pltpu.CompilerParams(
            dimension_semantics=("parallel",), vmem_limit_bytes=VMEM_LIMIT),
        name="project",
    )(x, g.reshape(1, D_MODEL), w, *tables)


def _stacked_queries_t(qt_ref):
    qt = jnp.concatenate([qt_ref[r] for r in range(qt_ref.shape[0])], axis=1)
    feat = lax.broadcasted_iota(jnp.int32, qt.shape, 0)
    zero = jnp.zeros_like(qt)
    return jnp.concatenate([jnp.where(feat < HEAD_DIM, qt, zero),
                            jnp.where(feat >= HEAD_DIM, qt, zero)], axis=1)


def _chunk_visibility(key_offset, cols):
    if key_offset is None:
        return "all"
    q_first, q_last = cols.start % ATT_TQ, (cols.stop - 1) % ATT_TQ
    if key_offset > q_last:
        return "none"
    if key_offset + ATT_TK - 1 <= q_first:
        return "all"
    return "part"


def _block_bias_rows(bias, cols):
    bias_ref, blk0 = bias
    return bias_ref[pl.ds(blk0, 1), cols], bias_ref[pl.ds(blk0 + 1, 1), cols]


def _score_chunk(k_ref, row0, q_ref, s_ref, mx_ref, key_offset, cols, bias=None):
    vis = _chunk_visibility(key_offset, cols)
    if vis == "none":
        return
    st = jnp.dot(k_ref[pl.ds(row0, ATT_TK), :], q_ref[:, cols], preferred_element_type=F32)
    if vis == "part":
        key = lax.broadcasted_iota(jnp.int32, st.shape, 0) + key_offset
        qpos = lax.broadcasted_iota(jnp.int32, st.shape, 1) + cols.start % ATT_TQ
        st = jnp.where(key <= qpos, st, -jnp.inf)
    s_ref[:, cols] = st
    if bias is None:
        mx_ref[:, cols] = jnp.max(st, axis=0, keepdims=True)
    else:
        b_top, b_bot = _block_bias_rows(bias, cols)
        mx_ref[:, cols] = jnp.maximum(
            jnp.max(st[0:MOBA_BLOCK], axis=0, keepdims=True) + b_top,
            jnp.max(st[MOBA_BLOCK:], axis=0, keepdims=True) + b_bot)


def _consume_chunk(s_ref, mx_ref, vt, m_ref, acc_ref, key_offset, cols, split_heads, bias=None):
    if _chunk_visibility(key_offset, cols) == "none":
        return
    m_prev = m_ref[:, cols]
    m_new = jnp.maximum(m_prev, mx_ref[:, cols])
    alpha = jnp.exp2(m_prev - m_new)
    if bias is None:
        p = jnp.exp2(s_ref[:, cols] - m_new).astype(BF16)
    else:
        b_top, b_bot = _block_bias_rows(bias, cols)
        p = jnp.concatenate([jnp.exp2(s_ref[0:MOBA_BLOCK, cols] - (m_new - b_top)),
                             jnp.exp2(s_ref[MOBA_BLOCK:, cols] - (m_new - b_bot))],
                            axis=0).astype(BF16)
    if split_heads:
        vt = vt[0:HEAD_DIM] if cols.start < ATT_TQ else vt[HEAD_DIM:]
    vt_sum = jnp.concatenate([vt, jnp.ones((SUM_ROWS, vt.shape[1]), BF16)], axis=0)
    acc_ref[:, cols] = alpha * acc_ref[:, cols] + jnp.dot(vt_sum, p, preferred_element_type=F32)
    m_ref[:, cols] = m_new


def _run_kv_blocks(i, score_chunk, consume_chunk, chunk_width):
    def fused(sc, cs):
        for c0 in range(0, 2 * ATT_TQ, chunk_width):
            cols = slice(c0, c0 + chunk_width)
            if sc is not None:
                score_chunk(*sc, cols)
            if cs is not None:
                consume_chunk(*cs, cols)

    @pl.when(i == 0)
    def _():
        fused((0, 0, 0), None)
        fused((1, 1, ATT_TK), (0, 0, 0))
        fused(None, (1, 1, ATT_TK))

    @pl.when(i > 0)
    def _():
        fused((0, 0, None), None)

        def pair(b):
            fused((b + 1, 1, None), (b, 0, None))
            fused((b + 2, 0, None), (b + 1, 1, None))

        def two_pairs(q, carry):
            pair(4 * q)
            pair(4 * q + 2)
            return carry

        n_double = lax.shift_right_logical(i - 1, 1)
        lax.fori_loop(0, n_double, two_pairs, 0)

        @pl.when(lax.rem(i - 1, 2) == 1)
        def _():
            pair(4 * n_double)

        b = 2 * i - 2
        fused((b + 1, 1, None), (b, 0, None))
        fused((b + 2, 0, 0), (b + 1, 1, None))
        fused((b + 3, 1, ATT_TK), (b + 2, 0, 0))
        fused(None, (b + 3, 1, ATT_TK))


def _init_stats(m_ref, acc_ref, m_init):
    m_ref[...] = jnp.full(m_ref.shape, m_init, F32)
    acc_ref[...] = jnp.zeros(acc_ref.shape, F32)


def _normalized(acc_ref, lo, hi):
    rows = acc_ref.shape[0] - SUM_ROWS
    return acc_ref[0:rows, lo:hi] / acc_ref[rows:rows + 1, lo:hi]


def _pipeline_scratch(value_rows):
    cols = 2 * ATT_TQ
    return [pltpu.VMEM((ATT_TK, cols), F32), pltpu.VMEM((ATT_TK, cols), F32),
            pltpu.VMEM((1, cols), F32), pltpu.VMEM((1, cols), F32),
            pltpu.VMEM((1, cols), F32), pltpu.VMEM((value_rows + SUM_ROWS, cols), F32)]


def _attn_specs(s):
    qt_spec = pl.BlockSpec((None, ATT_TQ // ATT_TK, PAIR, ATT_TK), lambda h, i: (h, i, 0, 0))
    k_spec = pl.BlockSpec((s, PAIR), lambda h, i: (0, h))
    vt_spec = pl.BlockSpec((None, s // ATT_TK, PAIR, ATT_TK), lambda h, i: (h, 0, 0, 0))
    o_spec = pl.BlockSpec((ATT_TQ, PAIR), lambda h, i: (i, h))
    return qt_spec, k_spec, vt_spec, o_spec


def _diff_attn_kernel(qt_ref, k_ref, vt_ref, lam_ref, subln_ref, o_ref,
                      qs_ref, s0_ref, s1_ref, mx0_ref, mx1_ref, m_ref, acc_ref):
    tq, tk = ATT_TQ, ATT_TK
    i = pl.program_id(1)
    qs_ref[...] = _stacked_queries_t(qt_ref)
    _init_stats(m_ref, acc_ref, -jnp.inf)

    def score_chunk(j, slot, key_offset, cols):
        _score_chunk(k_ref, pl.multiple_of(j * tk, tk), qs_ref, (s0_ref, s1_ref)[slot],
                     (mx0_ref, mx1_ref)[slot], key_offset, cols)

    def consume_chunk(j, slot, key_offset, cols):
        _consume_chunk((s0_ref, s1_ref)[slot], (mx0_ref, mx1_ref)[slot], vt_ref[j], m_ref, acc_ref,
                       key_offset, cols, split_heads=False)

    _run_kv_blocks(i, score_chunk, consume_chunk, DIFF_CW)

    lv = lam_ref[...]
    lam = (jnp.exp(jnp.sum(lv[0:1] * lv[1:2], axis=-1, keepdims=True))
           - jnp.exp(jnp.sum(lv[2:3] * lv[3:4], axis=-1, keepdims=True)) + DIFF_LAMBDA_INIT)
    ot = _normalized(acc_ref, 0, tq) - lam * _normalized(acc_ref, tq, 2 * tq)
    o_ref[...] = (_rms(ot.T, subln_ref[...]) * (1.0 - DIFF_LAMBDA_INIT)).astype(BF16)


def _diff_attention(qt, k, vt, lam_vecs, subln):
    s = k.shape[0]
    qt_spec, k_spec, vt_spec, o_spec = _attn_specs(s)
    return pl.pallas_call(
        _diff_attn_kernel,
        grid=(N_PAIRS, s // ATT_TQ),
        in_specs=[qt_spec, k_spec, vt_spec,
                  pl.BlockSpec((4, HEAD_DIM), lambda h, i: (0, 0)),
                  pl.BlockSpec((1, PAIR), lambda h, i: (0, 0))],
        out_specs=o_spec,
        out_shape=jax.ShapeDtypeStruct((s, D_MODEL), BF16),
        scratch_shapes=[pltpu.VMEM((PAIR, 2 * ATT_TQ), BF16)] + _pipeline_scratch(PAIR),
        compiler_params=pltpu.CompilerParams(
            dimension_semantics=("parallel", "arbitrary"), vmem_limit_bytes=VMEM_LIMIT),
        name="diff_attn",
    )(qt, k, vt, lam_vecs, subln.reshape(1, PAIR))


def _moba_bias_t(gate, qblk):
    n = lax.broadcasted_iota(jnp.int32, gate.shape, 0)
    nf = n.astype(F32)
    past = n < qblk
    g = jnp.where(past, gate, -jnp.inf)
    picked = jnp.zeros(gate.shape, F32)
    for _ in range(MOBA_TOPK):
        mx = jnp.max(g, axis=0, keepdims=True)
        first = jnp.min(jnp.where(g == mx, nf, float(PAIR)), axis=0, keepdims=True)
        hit = nf == first
        picked = jnp.where(hit, 1.0, picked)
        g = jnp.where(hit, -jnp.inf, g)
    sel_past = jnp.where(past, jnp.where(picked > 0.0, 0.0, MASK_NEG), MASK_NEG)
    return jnp.where(n == qblk, 0.0, sel_past)


def _moba_attn_kernel(qt_ref, k_ref, vt_ref, kmean_ref, o_ref,
                      qs_ref, bias_ref, s0_ref, s1_ref, mx0_ref, mx1_ref, m_ref, acc_ref):
    tq, tk = ATT_TQ, ATT_TK
    i = pl.program_id(1)

    qs = _stacked_queries_t(qt_ref)
    gate, rest = None, kmean_ref[...]
    for _ in range(3):
        piece = rest.astype(BF16)
        part = jnp.dot(piece, qs, preferred_element_type=F32)
        gate = part if gate is None else gate + part
        rest = rest - piece.astype(F32)
    col = lax.broadcasted_iota(jnp.int32, (1, 2 * tq), 1)
    qblk = (i * tq + jnp.where(col >= tq, col - tq, col)) // MOBA_BLOCK
    qs_ref[...] = qs
    bias_ref[...] = _moba_bias_t(gate, qblk)
    _init_stats(m_ref, acc_ref, M_FLOOR)

    def block_bias(j):
        return bias_ref, j * (tk // MOBA_BLOCK)

    def score_chunk(j, slot, key_offset, cols):
        _score_chunk(k_ref, pl.multiple_of(j * tk, tk), qs_ref, (s0_ref, s1_ref)[slot],
                     (mx0_ref, mx1_ref)[slot], key_offset, cols, bias=block_bias(j))

    def consume_chunk(j, slot, key_offset, cols):
        _consume_chunk((s0_ref, s1_ref)[slot], (mx0_ref, mx1_ref)[slot], vt_ref[j], m_ref, acc_ref,
                       key_offset, cols, split_heads=True, bias=block_bias(j))

    _run_kv_blocks(i, score_chunk, consume_chunk, MOBA_CW)

    ot = jnp.concatenate([_normalized(acc_ref, 0, tq), _normalized(acc_ref, tq, 2 * tq)], axis=0)
    o_ref[...] = ot.T.astype(BF16)


def _moba_attention(qt, k, vt, kmean):
    s = k.shape[0]
    qt_spec, k_spec, vt_spec, o_spec = _attn_specs(s)
    return pl.pallas_call(
        _moba_attn_kernel,
        grid=(N_PAIRS, s // ATT_TQ),
        in_specs=[qt_spec, k_spec, vt_spec,
                  pl.BlockSpec((s // MOBA_BLOCK, PAIR), lambda h, i: (0, h))],
        out_specs=o_spec,
        out_shape=jax.ShapeDtypeStruct((s, D_MODEL), BF16),
        scratch_shapes=[pltpu.VMEM((PAIR, 2 * ATT_TQ), BF16),
                        pltpu.VMEM((s // MOBA_BLOCK, 2 * ATT_TQ), F32)] + _pipeline_scratch(HEAD_DIM),
        compiler_params=pltpu.CompilerParams(
            dimension_semantics=("parallel", "arbitrary"), vmem_limit_bytes=VMEM_LIMIT),
        name="moba_attn",
    )(qt, k, vt, kmean)


def kernel(x, ffn_norm, ffn_w_in, ffn_w_out, mix_norm, diff_w_qkv, diff_lambda, diff_subln,
           diff_w_o, kv_norm, kv_w, moba_w_q, moba_w_o):
    b, s, d = x.shape
    assert b == 1 and d == D_MODEL and s % ATT_TQ == 0
    assert ATT_TK == 2 * MOBA_BLOCK and (s // MOBA_BLOCK) % 8 == 0
    assert ffn_w_in.shape[0] == 2 and diff_w_qkv.shape[0] == 1 and moba_w_q.shape[0] == 1
    tables = _rope_tables(s)
    w_in, w_out = ffn_w_in.astype(BF16), ffn_w_out.astype(BF16)

    def ffn(xx, layer, slot, mixer_out=None):
        return _ffn_half(xx, ffn_norm, w_in, w_out, layer, slot, mixer_out)

    xs = x.reshape(s, d)

    xs = ffn(xs, 0, 0)
    qt, k, vt = _project(xs, mix_norm[0, 0], diff_w_qkv[0].astype(BF16), tables, n_rope=2,
                         scale0=Q_SCALE, transposed=(True, False, True), with_kmean=False)
    o = _diff_attention(qt, k, vt, diff_lambda[0], diff_subln[0])
    xs = ffn(xs, 0, 1, mixer_out=(o, diff_w_o[0].astype(BF16), mix_norm[0, 1]))

    k, vt, kmean = _project(xs, kv_norm, kv_w.astype(BF16), tables, n_rope=1, scale0=1.0,
                            transposed=(False, True), with_kmean=True)
    kmean = kmean.reshape(s // MOBA_BLOCK, d)
    xs = ffn(xs, 1, 0)
    (qt,) = _project(xs, mix_norm[1, 0], moba_w_q[0].astype(BF16), tables, n_rope=1,
                     scale0=Q_SCALE, transposed=(True,), with_kmean=False)
    o = _moba_attention(qt, k, vt, kmean)
    xs = ffn(xs, 1, 1, mixer_out=(o, moba_w_o[0].astype(BF16), mix_norm[1, 1]))
    return xs.reshape(b, s, d)
```

```python
import functools
import math

import jax
import jax.numpy as jnp
from jax import lax
from jax.experimental import pallas as pl
from jax.experimental.pallas import tpu as pltpu

F32 = jnp.float32
BF16 = jnp.bfloat16

D_MODEL = 1024
HEAD_DIM = 64
PAIR = 2 * HEAD_DIM
N_PAIRS = D_MODEL // PAIR
D_FF = 2816
ROPE_THETA = 10000.0
MOBA_BLOCK = 256
MOBA_TOPK = 3
FFN_RESIDUAL_SCALE = 0.5
NORM_EPS = 1e-6
DIFF_LAMBDA_INIT = 0.2
MASK_NEG = -1e30
M_FLOOR = -1e29
Q_SCALE = HEAD_DIM ** -0.5 * math.log2(math.e)

FFN_TM = 512
FFN_CW = 256
ATT_TK = 512
ATT_TQ = 2 * ATT_TK
DIFF_CW = 512
MOBA_CW = 256
SUM_ROWS = 16
VMEM_LIMIT = 56 * 1024 * 1024


def _rms(x, g):
    return x * lax.rsqrt(jnp.mean(x * x, axis=-1, keepdims=True) + NORM_EPS) * g


def _ffn_kernel(*refs, with_mixer_out):
    if with_mixer_out:
        attn_ref, wmix_ref, gmix_ref, *refs = refs
    x_ref, gpre_ref, gpost_ref, win_ref, wout_ref, o_ref, act_ref = refs
    x = x_ref[...]
    if with_mixer_out:
        mixed = jnp.dot(attn_ref[...], wmix_ref[...], preferred_element_type=F32)
        x = x + _rms(mixed, gmix_ref[...])
    h = _rms(x, gpre_ref[...]).astype(BF16)
    for c in range(D_FF // FFN_CW):
        lo = c * FFN_CW
        w_gate = win_ref[:, lo:lo + FFN_CW].astype(BF16)
        w_up = win_ref[:, D_FF + lo:D_FF + lo + FFN_CW].astype(BF16)
        gate = jnp.dot(h, w_gate, preferred_element_type=F32)
        up = jnp.dot(h, w_up, preferred_element_type=F32)
        act_ref[:, lo:lo + FFN_CW] = (gate * (1.0 / (1.0 + jnp.exp(-gate))) * up).astype(BF16)
    act = act_ref[...]
    y = jnp.concatenate(
        [jnp.dot(act, wout_ref[:, n:n + FFN_CW].astype(BF16), preferred_element_type=F32)
         for n in range(0, D_MODEL, FFN_CW)], axis=1)
    o_ref[...] = x + FFN_RESIDUAL_SCALE * _rms(y, gpost_ref[...])


def _ffn_half(x, norms, w_in, w_out, layer, slot, mixer_out=None):
    s = x.shape[0]
    row = lambda i: (i, 0)
    const = lambda i: (0, 0)
    weights = lambda i: (layer, slot, 0, 0)
    resident = pl.Buffered(1)
    g_pre, g_post = norms[layer, slot, 0], norms[layer, slot, 1]
    operands = [x, g_pre.reshape(1, D_MODEL), g_post.reshape(1, D_MODEL), w_in, w_out]
    in_specs = [
        pl.BlockSpec((FFN_TM, D_MODEL), row),
        pl.BlockSpec((1, D_MODEL), const),
        pl.BlockSpec((1, D_MODEL), const),
        pl.BlockSpec((None, None, D_MODEL, 2 * D_FF), weights, pipeline_mode=resident),
        pl.BlockSpec((None, None, D_FF, D_MODEL), weights, pipeline_mode=resident),
    ]
    if mixer_out is not None:
        attn, w_o, g_mix = mixer_out
        operands = [attn, w_o, g_mix.reshape(1, D_MODEL)] + operands
        in_specs = [pl.BlockSpec((FFN_TM, D_MODEL), row),
                    pl.BlockSpec((D_MODEL, D_MODEL), const, pipeline_mode=resident),
                    pl.BlockSpec((1, D_MODEL), const)] + in_specs
    return pl.pallas_call(
        functools.partial(_ffn_kernel, with_mixer_out=mixer_out is not None),
        grid=(s // FFN_TM,),
        in_specs=in_specs,
        out_specs=pl.BlockSpec((FFN_TM, D_MODEL), row),
        out_shape=jax.ShapeDtypeStruct((s, D_MODEL), F32),
        scratch_shapes=[pltpu.VMEM((FFN_TM, D_FF), BF16)],
        compiler_params=pltpu.CompilerParams(
            dimension_semantics=("parallel",), vmem_limit_bytes=VMEM_LIMIT),
        name="ffn_half",
    )(*operands)


def _rope_tables(s):
    pos = jnp.arange(s, dtype=F32)
    inv_freq = 1.0 / (ROPE_THETA ** (jnp.arange(0, HEAD_DIM, 2, dtype=F32) / HEAD_DIM))
    ang = pos[:, None] * inv_freq[None, :]
    cos, sin = jnp.cos(ang)[:, None, :], jnp.sin(ang)[:, None, :]
    quarter = lambda coef: jnp.asarray(coef, F32).reshape(1, 4, 1)
    cos_t = (cos * quarter([1, 1, 1, 1])).reshape(s, PAIR)
    sin_lo = (sin * quarter([-1, 0, -1, 0])).reshape(s, PAIR)
    sin_hi = (sin * quarter([0, 1, 0, 1])).reshape(s, PAIR)
    return cos_t, sin_lo, sin_hi


def _proj_kernel(x_ref, g_ref, w_ref, cos_ref, sinlo_ref, sinhi_ref, *out_refs,
                 n_rope, scale0, transposed, with_kmean):
    n_chunks = len(transposed)
    h = _rms(x_ref[...], g_ref[...]).astype(BF16)
    tm = h.shape[0]
    reps = D_MODEL // PAIR
    cos = jnp.concatenate([cos_ref[...]] * reps, axis=1)
    sin_lo = jnp.concatenate([sinlo_ref[...]] * reps, axis=1)
    sin_hi = jnp.concatenate([sinhi_ref[...]] * reps, axis=1)
    for c in range(n_chunks):
        y = jnp.dot(h, w_ref[:, c * D_MODEL:(c + 1) * D_MODEL], preferred_element_type=F32)
        if c < n_rope:
            half = HEAD_DIM // 2
            y = (y * cos + pltpu.roll(y, D_MODEL - half, 1) * sin_lo
                 + pltpu.roll(y, half, 1) * sin_hi)
            if c == 0 and with_kmean:
                km = jnp.mean(y.reshape(tm // MOBA_BLOCK, MOBA_BLOCK, D_MODEL), axis=1)
                out_refs[n_chunks][0] = km
        if c == 0 and scale0 != 1.0:
            y = y * scale0
        if transposed[c]:
            out_refs[c][...] = y.T.reshape(N_PAIRS, 1, PAIR, tm).astype(BF16)
        else:
            out_refs[c][...] = y.astype(BF16)


def _project(x, g, w, tables, *, n_rope, scale0, transposed, with_kmean):
    s = x.shape[0]
    tm = ATT_TK
    row = lambda i: (i, 0)
    out_shape, out_specs = [], []
    for tr in transposed:
        if tr:
            out_shape.append(jax.ShapeDtypeStruct((N_PAIRS, s // tm, PAIR, tm), BF16))
            out_specs.append(pl.BlockSpec((N_PAIRS, 1, PAIR, tm), lambda i: (0, i, 0, 0)))
        else:
            out_shape.append(jax.ShapeDtypeStruct((s, D_MODEL), BF16))
            out_specs.append(pl.BlockSpec((tm, D_MODEL), row))
    if with_kmean:
        nb = tm // MOBA_BLOCK
        out_shape.append(jax.ShapeDtypeStruct((s // tm, nb, D_MODEL), F32))
        out_specs.append(pl.BlockSpec((1, nb, D_MODEL), lambda i: (i, 0, 0)))
    return pl.pallas_call(
        functools.partial(_proj_kernel, n_rope=n_rope, scale0=scale0, transposed=transposed,
                          with_kmean=with_kmean),
        grid=(s // tm,),
        in_specs=[
            pl.BlockSpec((tm, D_MODEL), row),
            pl.BlockSpec((1, D_MODEL), lambda i: (0, 0)),
            pl.BlockSpec(w.shape, lambda i: (0, 0)),
            pl.BlockSpec((tm, PAIR), row),
            pl.BlockSpec((tm, PAIR), row),
            pl.BlockSpec((tm, PAIR), row),
        ],
        out_specs=out_specs,
        out_shape=out_shape,
        compiler_params=pltpu.CompilerParams(
            dimension_semantics=("parallel",), vmem_limit_bytes=VMEM_LIMIT),
        name="project",
    )(x, g.reshape(1, D_MODEL), w, *tables)


def _stacked_queries_t(qt_ref):
    qt = jnp.concatenate([qt_ref[r] for r in range(qt_ref.shape[0])], axis=1)
    feat = lax.broadcasted_iota(jnp.int32, qt.shape, 0)
    zero = jnp.zeros_like(qt)
    return jnp.concatenate([jnp.where(feat < HEAD_DIM, qt, zero),
                            jnp.where(feat >= HEAD_DIM, qt, zero)], axis=1)


def _chunk_visibility(key_offset, cols):
    if key_offset is None:
        return "all"
    q_first, q_last = cols.start % ATT_TQ, (cols.stop - 1) % ATT_TQ
    if key_offset > q_last:
        return "none"
    if key_offset + ATT_TK - 1 <= q_first:
        return "all"
    return "part"


def _block_bias_rows(bias, cols):
    bias_ref, blk0 = bias
    return bias_ref[pl.ds(blk0, 1), cols], bias_ref[pl.ds(blk0 + 1, 1), cols]


def _score_chunk(k_ref, row0, q_ref, s_ref, mx_ref, key_offset, cols, bias=None):
    vis = _chunk_visibility(key_offset, cols)
    if vis == "none":
        return
    st = jnp.dot(k_ref[pl.ds(row0, ATT_TK), :], q_ref[:, cols], preferred_element_type=F32)
    if vis == "part":
        key = lax.broadcasted_iota(jnp.int32, st.shape, 0) + key_offset
        qpos = lax.broadcasted_iota(jnp.int32, st.shape, 1) + cols.start % ATT_TQ
        st = jnp.where(key <= qpos, st, -jnp.inf)
    s_ref[:, cols] = st
    if bias is None:
        mx_ref[:, cols] = jnp.max(st, axis=0, keepdims=True)
    else:
        b_top, b_bot = _block_bias_rows(bias, cols)
        mx_ref[:, cols] = jnp.maximum(
            jnp.max(st[0:MOBA_BLOCK], axis=0, keepdims=True) + b_top,
            jnp.max(st[MOBA_BLOCK:], axis=0, keepdims=True) + b_bot)


def _consume_chunk(s_ref, mx_ref, vt, m_ref, acc_ref, key_offset, cols, split_heads, bias=None):
    if _chunk_visibility(key_offset, cols) == "none":
        return
    m_prev = m_ref[:, cols]
    m_new = jnp.maximum(m_prev, mx_ref[:, cols])
    alpha = jnp.exp2(m_prev - m_new)
    if bias is None:
        p = jnp.exp2(s_ref[:, cols] - m_new).astype(BF16)
    else:
        b_top, b_bot = _block_bias_rows(bias, cols)
        p = jnp.concatenate([jnp.exp2(s_ref[0:MOBA_BLOCK, cols] - (m_new - b_top)),
                             jnp.exp2(s_ref[MOBA_BLOCK:, cols] - (m_new - b_bot))],
                            axis=0).astype(BF16)
    if split_heads:
        vt = vt[0:HEAD_DIM] if cols.start < ATT_TQ else vt[HEAD_DIM:]
    vt_sum = jnp.concatenate([vt, jnp.ones((SUM_ROWS, vt.shape[1]), BF16)], axis=0)
    acc_ref[:, cols] = alpha * acc_ref[:, cols] + jnp.dot(vt_sum, p, preferred_element_type=F32)
    m_ref[:, cols] = m_new


def _run_kv_blocks(i, score_chunk, consume_chunk, chunk_width):
    def fused(sc, cs):
        for c0 in range(0, 2 * ATT_TQ, chunk_width):
            cols = slice(c0, c0 + chunk_width)
            if sc is not None:
                score_chunk(*sc, cols)
            if cs is not None:
                consume_chunk(*cs, cols)

    @pl.when(i == 0)
    def _():
        fused((0, 0, 0), None)
        fused((1, 1, ATT_TK), (0, 0, 0))
        fused(None, (1, 1, ATT_TK))

    @pl.when(i > 0)
    def _():
        fused((0, 0, None), None)

        def pair(b):
            fused((b + 1, 1, None), (b, 0, None))
            fused((b + 2, 0, None), (b + 1, 1, None))

        def two_pairs(q, carry):
            pair(4 * q)
            pair(4 * q + 2)
            return carry

        n_double = lax.shift_right_logical(i - 1, 1)
        lax.fori_loop(0, n_double, two_pairs, 0)

        @pl.when(lax.rem(i - 1, 2) == 1)
        def _():
            pair(4 * n_double)

        b = 2 * i - 2
        fused((b + 1, 1, None), (b, 0, None))
        fused((b + 2, 0, 0), (b + 1, 1, None))
        fused((b + 3, 1, ATT_TK), (b + 2, 0, 0))
        fused(None, (b + 3, 1, ATT_TK))


def _init_stats(m_ref, acc_ref, m_init):
    m_ref[...] = jnp.full(m_ref.shape, m_init, F32)
    acc_ref[...] = jnp.zeros(acc_ref.shape, F32)


def _normalized(acc_ref, lo, hi):
    rows = acc_ref.shape[0] - SUM_ROWS
    return acc_ref[0:rows, lo:hi] / acc_ref[rows:rows + 1, lo:hi]


def _pipeline_scratch(value_rows):
    cols = 2 * ATT_TQ
    return [pltpu.VMEM((ATT_TK, cols), F32), pltpu.VMEM((ATT_TK, cols), F32),
            pltpu.VMEM((1, cols), F32), pltpu.VMEM((1, cols), F32),
            pltpu.VMEM((1, cols), F32), pltpu.VMEM((value_rows + SUM_ROWS, cols), F32)]


def _attn_specs(s):
    qt_spec = pl.BlockSpec((None, ATT_TQ // ATT_TK, PAIR, ATT_TK), lambda h, i: (h, i, 0, 0))
    k_spec = pl.BlockSpec((s, PAIR), lambda h, i: (0, h))
    vt_spec = pl.BlockSpec((None, s // ATT_TK, PAIR, ATT_TK), lambda h, i: (h, 0, 0, 0))
    o_spec = pl.BlockSpec((ATT_TQ, PAIR), lambda h, i: (i, h))
    return qt_spec, k_spec, vt_spec, o_spec


def _diff_attn_kernel(qt_ref, k_ref, vt_ref, lam_ref, subln_ref, o_ref,
                      qs_ref, s0_ref, s1_ref, mx0_ref, mx1_ref, m_ref, acc_ref):
    tq, tk = ATT_TQ, ATT_TK
    i = pl.program_id(1)
    qs_ref[...] = _stacked_queries_t(qt_ref)
    _init_stats(m_ref, acc_ref, -jnp.inf)

    def score_chunk(j, slot, key_offset, cols):
        _score_chunk(k_ref, pl.multiple_of(j * tk, tk), qs_ref, (s0_ref, s1_ref)[slot],
                     (mx0_ref, mx1_ref)[slot], key_offset, cols)

    def consume_chunk(j, slot, key_offset, cols):
        _consume_chunk((s0_ref, s1_ref)[slot], (mx0_ref, mx1_ref)[slot], vt_ref[j], m_ref, acc_ref,
                       key_offset, cols, split_heads=False)

    _run_kv_blocks(i, score_chunk, consume_chunk, DIFF_CW)

    lv = lam_ref[...]
    lam = (jnp.exp(jnp.sum(lv[0:1] * lv[1:2], axis=-1, keepdims=True))
           - jnp.exp(jnp.sum(lv[2:3] * lv[3:4], axis=-1, keepdims=True)) + DIFF_LAMBDA_INIT)
    ot = _normalized(acc_ref, 0, tq) - lam * _normalized(acc_ref, tq, 2 * tq)
    o_ref[...] = (_rms(ot.T, subln_ref[...]) * (1.0 - DIFF_LAMBDA_INIT)).astype(BF16)


def _diff_attention(qt, k, vt, lam_vecs, subln):
    s = k.shape[0]
    qt_spec, k_spec, vt_spec, o_spec = _attn_specs(s)
    return pl.pallas_call(
        _diff_attn_kernel,
        grid=(N_PAIRS, s // ATT_TQ),
        in_specs=[qt_spec, k_spec, vt_spec,
                  pl.BlockSpec((4, HEAD_DIM), lambda h, i: (0, 0)),
                  pl.BlockSpec((1, PAIR), lambda h, i: (0, 0))],
        out_specs=o_spec,
        out_shape=jax.ShapeDtypeStruct((s, D_MODEL), BF16),
        scratch_shapes=[pltpu.VMEM((PAIR, 2 * ATT_TQ), BF16)] + _pipeline_scratch(PAIR),
        compiler_params=pltpu.CompilerParams(
            dimension_semantics=("parallel", "arbitrary"), vmem_limit_bytes=VMEM_LIMIT),
        name="diff_attn",
    )(qt, k, vt, lam_vecs, subln.reshape(1, PAIR))


def _moba_bias_t(gate, qblk):
    n = lax.broadcasted_iota(jnp.int32, gate.shape, 0)
    nf = n.astype(F32)
    past = n < qblk
    g = jnp.where(past, gate, -jnp.inf)
    picked = jnp.zeros(gate.shape, F32)
    for _ in range(MOBA_TOPK):
        mx = jnp.max(g, axis=0, keepdims=True)
        first = jnp.min(jnp.where(g == mx, nf, float(PAIR)), axis=0, keepdims=True)
        hit = nf == first
        picked = jnp.where(hit, 1.0, picked)
        g = jnp.where(hit, -jnp.inf, g)
    sel_past = jnp.where(past, jnp.where(picked > 0.0, 0.0, MASK_NEG), MASK_NEG)
    return jnp.where(n == qblk, 0.0, sel_past)


def _moba_attn_kernel(qt_ref, k_ref, vt_ref, kmean_ref, o_ref,
                      qs_ref, bias_ref, s0_ref, s1_ref, mx0_ref, mx1_ref, m_ref, acc_ref):
    tq, tk = ATT_TQ, ATT_TK
    i = pl.program_id(1)

    qs = _stacked_queries_t(qt_ref)
    gate, rest = None, kmean_ref[...]
    for _ in range(3):
        piece = rest.astype(BF16)
        part = jnp.dot(piece, qs, preferred_element_type=F32)
        gate = part if gate is None else gate + part
        rest = rest - piece.astype(F32)
    col = lax.broadcasted_iota(jnp.int32, (1, 2 * tq), 1)
    qblk = (i * tq + jnp.where(col >= tq, col - tq, col)) // MOBA_BLOCK
    qs_ref[...] = qs
    bias_ref[...] = _moba_bias_t(gate, qblk)
    _init_stats(m_ref, acc_ref, M_FLOOR)

    def block_bias(j):
        return bias_ref, j * (tk // MOBA_BLOCK)

    def score_chunk(j, slot, key_offset, cols):
        _score_chunk(k_ref, pl.multiple_of(j * tk, tk), qs_ref, (s0_ref, s1_ref)[slot],
                     (mx0_ref, mx1_ref)[slot], key_offset, cols, bias=block_bias(j))

    def consume_chunk(j, slot, key_offset, cols):
        _consume_chunk((s0_ref, s1_ref)[slot], (mx0_ref, mx1_ref)[slot], vt_ref[j], m_ref, acc_ref,
                       key_offset, cols, split_heads=True, bias=block_bias(j))

    _run_kv_blocks(i, score_chunk, consume_chunk, MOBA_CW)

    ot = jnp.concatenate([_normalized(acc_ref, 0, tq), _normalized(acc_ref, tq, 2 * tq)], axis=0)
    o_ref[...] = ot.T.astype(BF16)


def _moba_attention(qt, k, vt, kmean):
    s = k.shape[0]
    qt_spec, k_spec, vt_spec, o_spec = _attn_specs(s)
    return pl.pallas_call(
        _moba_attn_kernel,
        grid=(N_PAIRS, s // ATT_TQ),
        in_specs=[qt_spec, k_spec, vt_spec,
                  pl.BlockSpec((s // MOBA_BLOCK, PAIR), lambda h, i: (0, h))],
        out_specs=o_spec,
        out_shape=jax.ShapeDtypeStruct((s, D_MODEL), BF16),
        scratch_shapes=[pltpu.VMEM((PAIR, 2 * ATT_TQ), BF16),
                        pltpu.VMEM((s // MOBA_BLOCK, 2 * ATT_TQ), F32)] + _pipeline_scratch(HEAD_DIM),
        compiler_params=pltpu.CompilerParams(
            dimension_semantics=("parallel", "arbitrary"), vmem_limit_bytes=VMEM_LIMIT),
        name="moba_attn",
    )(qt, k, vt, kmean)


def kernel(x, ffn_norm, ffn_w_in, ffn_w_out, mix_norm, diff_w_qkv, diff_lambda, diff_subln,
           diff_w_o, kv_norm, kv_w, moba_w_q, moba_w_o):
    b, s, d = x.shape
    assert b == 1 and d == D_MODEL and s % ATT_TQ == 0
    assert ATT_TK == 2 * MOBA_BLOCK and (s // MOBA_BLOCK) % 8 == 0
    assert ffn_w_in.shape[0] == 2 and diff_w_qkv.shape[0] == 1 and moba_w_q.shape[0] == 1
    tables = _rope_tables(s)
    w_in, w_out = ffn_w_in, ffn_w_out

    def ffn(xx, layer, slot, mixer_out=None):
        return _ffn_half(xx, ffn_norm, w_in, w_out, layer, slot, mixer_out)

    xs = x.reshape(s, d)

    xs = ffn(xs, 0, 0)
    qt, k, vt = _project(xs, mix_norm[0, 0], diff_w_qkv[0].astype(BF16), tables, n_rope=2,
                         scale0=Q_SCALE, transposed=(True, False, True), with_kmean=False)
    o = _diff_attention(qt, k, vt, diff_lambda[0], diff_subln[0])
    xs = ffn(xs, 0, 1, mixer_out=(o, diff_w_o[0].astype(BF16), mix_norm[0, 1]))

    k, vt, kmean = _project(xs, kv_norm, kv_w.astype(BF16), tables, n_rope=1, scale0=1.0,
                            transposed=(False, True), with_kmean=True)
    kmean = kmean.reshape(s // MOBA_BLOCK, d)
    xs = ffn(xs, 1, 0)
    (qt,) = _project(xs, mix_norm[1, 0], moba_w_q[0].astype(BF16), tables, n_rope=1,
                     scale0=Q_SCALE, transposed=(True,), with_kmean=False)
    o = _moba_attention(qt, k, vt, kmean)
    xs = ffn(xs, 1, 1, mixer_out=(o, moba_w_o[0].astype(BF16), mix_norm[1, 1]))
    return xs.reshape(b, s, d)
```

```python
import functools
import math

import jax
import jax.numpy as jnp
from jax import lax
from jax.experimental import pallas as pl
from jax.experimental.pallas import tpu as pltpu

F32 = jnp.float32
BF16 = jnp.bfloat16

D_MODEL = 1024
HEAD_DIM = 64
PAIR = 2 * HEAD_DIM
N_PAIRS = D_MODEL // PAIR
D_FF = 2816
ROPE_THETA = 10000.0
MOBA_BLOCK = 256
MOBA_TOPK = 3
FFN_RESIDUAL_SCALE = 0.5
NORM_EPS = 1e-6
DIFF_LAMBDA_INIT = 0.2
MASK_NEG = -1e30
M_FLOOR = -1e29
Q_SCALE = HEAD_DIM ** -0.5 * math.log2(math.e)

FFN_TM = 512
FFN_CW = 256
PROJ_CW = 256
ATT_TK = 512
ATT_TQ = 2 * ATT_TK
DIFF_CW = 512
MOBA_CW = 256
SUM_ROWS = 16
VMEM_LIMIT = 56 * 1024 * 1024


def _rms(x, g):
    return x * lax.rsqrt(jnp.mean(x * x, axis=-1, keepdims=True) + NORM_EPS) * g


def _ffn_kernel(*refs, with_mixer_out):
    if with_mixer_out:
        attn_ref, wmix_ref, gmix_ref, *refs = refs
    x_ref, gpre_ref, gpost_ref, win_ref, wout_ref, o_ref, act_ref = refs
    x = x_ref[...]
    if with_mixer_out:
        mixed = jnp.dot(attn_ref[...], wmix_ref[...], preferred_element_type=F32)
        x = x + _rms(mixed, gmix_ref[...])
    h = _rms(x, gpre_ref[...]).astype(BF16)
    for c in range(D_FF // FFN_CW):
        lo = c * FFN_CW
        w_gate = win_ref[:, lo:lo + FFN_CW].astype(BF16)
        w_up = win_ref[:, D_FF + lo:D_FF + lo + FFN_CW].astype(BF16)
        gate = jnp.dot(h, w_gate, preferred_element_type=F32)
        up = jnp.dot(h, w_up, preferred_element_type=F32)
        act_ref[:, lo:lo + FFN_CW] = (gate * (1.0 / (1.0 + jnp.exp(-gate))) * up).astype(BF16)
    act = act_ref[...]
    y = jnp.concatenate(
        [jnp.dot(act, wout_ref[:, n:n + FFN_CW].astype(BF16), preferred_element_type=F32)
         for n in range(0, D_MODEL, FFN_CW)], axis=1)
    o_ref[...] = x + FFN_RESIDUAL_SCALE * _rms(y, gpost_ref[...])


def _ffn_half(x, norms, w_in, w_out, layer, slot, mixer_out=None):
    s = x.shape[0]
    row = lambda i: (i, 0)
    const = lambda i: (0, 0)
    weights = lambda i: (layer, slot, 0, 0)
    resident = pl.Buffered(1)
    g_pre, g_post = norms[layer, slot, 0], norms[layer, slot, 1]
    operands = [x, g_pre.reshape(1, D_MODEL), g_post.reshape(1, D_MODEL), w_in, w_out]
    in_specs = [
        pl.BlockSpec((FFN_TM, D_MODEL), row),
        pl.BlockSpec((1, D_MODEL), const),
        pl.BlockSpec((1, D_MODEL), const),
        pl.BlockSpec((None, None, D_MODEL, 2 * D_FF), weights, pipeline_mode=resident),
        pl.BlockSpec((None, None, D_FF, D_MODEL), weights, pipeline_mode=resident),
    ]
    if mixer_out is not None:
        attn, w_o, g_mix = mixer_out
        operands = [attn, w_o, g_mix.reshape(1, D_MODEL)] + operands
        in_specs = [pl.BlockSpec((FFN_TM, D_MODEL), row),
                    pl.BlockSpec((D_MODEL, D_MODEL), const, pipeline_mode=resident),
                    pl.BlockSpec((1, D_MODEL), const)] + in_specs
    return pl.pallas_call(
        functools.partial(_ffn_kernel, with_mixer_out=mixer_out is not None),
        grid=(s // FFN_TM,),
        in_specs=in_specs,
        out_specs=pl.BlockSpec((FFN_TM, D_MODEL), row),
        out_shape=jax.ShapeDtypeStruct((s, D_MODEL), F32),
        scratch_shapes=[pltpu.VMEM((FFN_TM, D_FF), BF16)],
        compiler_params=pltpu.CompilerParams(
            dimension_semantics=("parallel",), vmem_limit_bytes=VMEM_LIMIT),
        name="ffn_half",
    )(*operands)


def _rope_tables(s):
    pos = jnp.arange(s, dtype=F32)
    inv_freq = 1.0 / (ROPE_THETA ** (jnp.arange(0, HEAD_DIM, 2, dtype=F32) / HEAD_DIM))
    ang = pos[:, None] * inv_freq[None, :]
    cos, sin = jnp.cos(ang)[:, None, :], jnp.sin(ang)[:, None, :]
    quarter = lambda coef: jnp.asarray(coef, F32).reshape(1, 4, 1)
    cos_t = (cos * quarter([1, 1, 1, 1])).reshape(s, PAIR)
    sin_lo = (sin * quarter([-1, 0, -1, 0])).reshape(s, PAIR)
    sin_hi = (sin * quarter([0, 1, 0, 1])).reshape(s, PAIR)
    return cos_t, sin_lo, sin_hi


def _proj_kernel(x_ref, g_ref, w_ref, cos_ref, sinlo_ref, sinhi_ref, *out_refs,
                 n_rope, scale0, transposed, with_kmean):
    n_chunks = len(transposed)
    h = _rms(x_ref[...], g_ref[...]).astype(BF16)
    tm = h.shape[0]
    reps = PROJ_CW // PAIR
    cos = jnp.concatenate([cos_ref[...]] * reps, axis=1)
    sin_lo = jnp.concatenate([sinlo_ref[...]] * reps, axis=1)
    sin_hi = jnp.concatenate([sinhi_ref[...]] * reps, axis=1)
    for c in range(n_chunks):
        for lo in range(0, D_MODEL, PROJ_CW):
            cols = slice(lo, lo + PROJ_CW)
            y = jnp.dot(h, w_ref[:, c * D_MODEL + lo:c * D_MODEL + lo + PROJ_CW],
                        preferred_element_type=F32)
            if c < n_rope:
                half = HEAD_DIM // 2
                y = (y * cos + pltpu.roll(y, PROJ_CW - half, 1) * sin_lo
                     + pltpu.roll(y, half, 1) * sin_hi)
                if c == 0 and with_kmean:
                    km = jnp.mean(y.reshape(tm // MOBA_BLOCK, MOBA_BLOCK, PROJ_CW), axis=1)
                    out_refs[n_chunks][0, :, cols] = km
            if c == 0 and scale0 != 1.0:
                y = y * scale0
            if transposed[c]:
                pairs = slice(lo // PAIR, (lo + PROJ_CW) // PAIR)
                out_refs[c][pairs] = y.T.reshape(reps, 1, PAIR, tm).astype(BF16)
            else:
                out_refs[c][:, cols] = y.astype(BF16)


def _project(x, g, w, tables, *, n_rope, scale0, transposed, with_kmean):
    s = x.shape[0]
    tm = ATT_TK
    row = lambda i: (i, 0)
    out_shape, out_specs = [], []
    for tr in transposed:
        if tr:
            out_shape.append(jax.ShapeDtypeStruct((N_PAIRS, s // tm, PAIR, tm), BF16))
            out_specs.append(pl.BlockSpec((N_PAIRS, 1, PAIR, tm), lambda i: (0, i, 0, 0)))
        else:
            out_shape.append(jax.ShapeDtypeStruct((s, D_MODEL), BF16))
            out_specs.append(pl.BlockSpec((tm, D_MODEL), row))
    if with_kmean:
        nb = tm // MOBA_BLOCK
        out_shape.append(jax.ShapeDtypeStruct((s // tm, nb, D_MODEL), F32))
        out_specs.append(pl.BlockSpec((1, nb, D_MODEL), lambda i: (i, 0, 0)))
    return pl.pallas_call(
        functools.partial(_proj_kernel, n_rope=n_rope, scale0=scale0, transposed=transposed,
                          with_kmean=with_kmean),
        grid=(s // tm,),
        in_specs=[
            pl.BlockSpec((tm, D_MODEL), row),
            pl.BlockSpec((1, D_MODEL), lambda i: (0, 0)),
            pl.BlockSpec(w.shape, lambda i: (0, 0)),
            pl.BlockSpec((tm, PAIR), row),
            pl.BlockSpec((tm, PAIR), row),
            pl.BlockSpec((tm, PAIR), row),
        ],
        out_specs=out_specs,
        out_shape=out_shape,
        compiler_params=pltpu.CompilerParams(
            dimension_semantics=("parallel",), vmem_limit_bytes=VMEM_LIMIT),
        name="project",
    )(x, g.reshape(1, D_MODEL), w, *tables)


def _stacked_queries_t(qt_ref):
    qt = jnp.concatenate([qt_ref[r] for r in range(qt_ref.shape[0])], axis=1)
    feat = lax.broadcasted_iota(jnp.int32, qt.shape, 0)
    zero = jnp.zeros_like(qt)
    return jnp.concatenate([jnp.where(feat < HEAD_DIM, qt, zero),
                            jnp.where(feat >= HEAD_DIM, qt, zero)], axis=1)


def _chunk_visibility(key_offset, cols):
    if key_offset is None:
        return "all"
    q_first, q_last = cols.start % ATT_TQ, (cols.stop - 1) % ATT_TQ
    if key_offset > q_last:
        return "none"
    if key_offset + ATT_TK - 1 <= q_first:
        return "all"
    return "part"


def _block_bias_rows(bias, cols):
    bias_ref, blk0 = bias
    return bias_ref[pl.ds(blk0, 1), cols], bias_ref[pl.ds(blk0 + 1, 1), cols]


def _score_chunk(k_ref, row0, q_ref, s_ref, mx_ref, key_offset, cols, bias=None):
    vis = _chunk_visibility(key_offset, cols)
    if vis == "none":
        return
    st = jnp.dot(k_ref[pl.ds(row0, ATT_TK), :], q_ref[:, cols], preferred_element_type=F32)
    if vis == "part":
        key = lax.broadcasted_iota(jnp.int32, st.shape, 0) + key_offset
        qpos = lax.broadcasted_iota(jnp.int32, st.shape, 1) + cols.start % ATT_TQ
        st = jnp.where(key <= qpos, st, -jnp.inf)
    s_ref[:, cols] = st
    if bias is None:
        mx_ref[:, cols] = jnp.max(st, axis=0, keepdims=True)
    else:
        b_top, b_bot = _block_bias_rows(bias, cols)
        mx_ref[:, cols] = jnp.maximum(
            jnp.max(st[0:MOBA_BLOCK], axis=0, keepdims=True) + b_top,
            jnp.max(st[MOBA_BLOCK:], axis=0, keepdims=True) + b_bot)


def _consume_chunk(s_ref, mx_ref, vt, m_ref, acc_ref, key_offset, cols, split_heads, bias=None):
    if _chunk_visibility(key_offset, cols) == "none":
        return
    m_prev = m_ref[:, cols]
    m_new = jnp.maximum(m_prev, mx_ref[:, cols])
    alpha = jnp.exp2(m_prev - m_new)
    if bias is None:
        p = jnp.exp2(s_ref[:, cols] - m_new).astype(BF16)
    else:
        b_top, b_bot = _block_bias_rows(bias, cols)
        p = jnp.concatenate([jnp.exp2(s_ref[0:MOBA_BLOCK, cols] - (m_new - b_top)),
                             jnp.exp2(s_ref[MOBA_BLOCK:, cols] - (m_new - b_bot))],
                            axis=0).astype(BF16)
    if split_heads:
        vt = vt[0:HEAD_DIM] if cols.start < ATT_TQ else vt[HEAD_DIM:]
    vt_sum = jnp.concatenate([vt, jnp.ones((SUM_ROWS, vt.shape[1]), BF16)], axis=0)
    acc_ref[:, cols] = alpha * acc_ref[:, cols] + jnp.dot(vt_sum, p, preferred_element_type=F32)
    m_ref[:, cols] = m_new


def _run_kv_blocks(i, score_chunk, consume_chunk, chunk_width):
    def fused(sc, cs):
        for c0 in range(0, 2 * ATT_TQ, chunk_width):
            cols = slice(c0, c0 + chunk_width)
            if sc is not None:
                score_chunk(*sc, cols)
            if cs is not None:
                consume_chunk(*cs, cols)

    @pl.when(i == 0)
    def _():
        fused((0, 0, 0), None)
        fused((1, 1, ATT_TK), (0, 0, 0))
        fused(None, (1, 1, ATT_TK))

    @pl.when(i > 0)
    def _():
        fused((0, 0, None), None)

        def pair(b):
            fused((b + 1, 1, None), (b, 0, None))
            fused((b + 2, 0, None), (b + 1, 1, None))

        def two_pairs(q, carry):
            pair(4 * q)
            pair(4 * q + 2)
            return carry

        n_double = lax.shift_right_logical(i - 1, 1)
        lax.fori_loop(0, n_double, two_pairs, 0)

        @pl.when(lax.rem(i - 1, 2) == 1)
        def _():
            pair(4 * n_double)

        b = 2 * i - 2
        fused((b + 1, 1, None), (b, 0, None))
        fused((b + 2, 0, 0), (b + 1, 1, None))
        fused((b + 3, 1, ATT_TK), (b + 2, 0, 0))
        fused(None, (b + 3, 1, ATT_TK))


def _init_stats(m_ref, acc_ref, m_init):
    m_ref[...] = jnp.full(m_ref.shape, m_init, F32)
    acc_ref[...] = jnp.zeros(acc_ref.shape, F32)


def _normalized(acc_ref, lo, hi):
    rows = acc_ref.shape[0] - SUM_ROWS
    return acc_ref[0:rows, lo:hi] / acc_ref[rows:rows + 1, lo:hi]


def _pipeline_scratch(value_rows):
    cols = 2 * ATT_TQ
    return [pltpu.VMEM((ATT_TK, cols), F32), pltpu.VMEM((ATT_TK, cols), F32),
            pltpu.VMEM((1, cols), F32), pltpu.VMEM((1, cols), F32),
            pltpu.VMEM((1, cols), F32), pltpu.VMEM((value_rows + SUM_ROWS, cols), F32)]


def _attn_specs(s):
    qt_spec = pl.BlockSpec((None, ATT_TQ // ATT_TK, PAIR, ATT_TK), lambda h, i: (h, i, 0, 0))
    k_spec = pl.BlockSpec((s, PAIR), lambda h, i: (0, h))
    vt_spec = pl.BlockSpec((None, s // ATT_TK, PAIR, ATT_TK), lambda h, i: (h, 0, 0, 0))
    o_spec = pl.BlockSpec((ATT_TQ, PAIR), lambda h, i: (i, h))
    return qt_spec, k_spec, vt_spec, o_spec


def _diff_attn_kernel(qt_ref, k_ref, vt_ref, lam_ref, subln_ref, o_ref,
                      qs_ref, s0_ref, s1_ref, mx0_ref, mx1_ref, m_ref, acc_ref):
    tq, tk = ATT_TQ, ATT_TK
    i = pl.program_id(1)
    qs_ref[...] = _stacked_queries_t(qt_ref)
    _init_stats(m_ref, acc_ref, -jnp.inf)

    def score_chunk(j, slot, key_offset, cols):
        _score_chunk(k_ref, pl.multiple_of(j * tk, tk), qs_ref, (s0_ref, s1_ref)[slot],
                     (mx0_ref, mx1_ref)[slot], key_offset, cols)

    def consume_chunk(j, slot, key_offset, cols):
        _consume_chunk((s0_ref, s1_ref)[slot], (mx0_ref, mx1_ref)[slot], vt_ref[j], m_ref, acc_ref,
                       key_offset, cols, split_heads=False)

    _run_kv_blocks(i, score_chunk, consume_chunk, DIFF_CW)

    lv = lam_ref[...]
    lam = (jnp.exp(jnp.sum(lv[0:1] * lv[1:2], axis=-1, keepdims=True))
           - jnp.exp(jnp.sum(lv[2:3] * lv[3:4], axis=-1, keepdims=True)) + DIFF_LAMBDA_INIT)
    ot = _normalized(acc_ref, 0, tq) - lam * _normalized(acc_ref, tq, 2 * tq)
    o_ref[...] = (_rms(ot.T, subln_ref[...]) * (1.0 - DIFF_LAMBDA_INIT)).astype(BF16)


def _diff_attention(qt, k, vt, lam_vecs, subln):
    s = k.shape[0]
    qt_spec, k_spec, vt_spec, o_spec = _attn_specs(s)
    return pl.pallas_call(
        _diff_attn_kernel,
        grid=(N_PAIRS, s // ATT_TQ),
        in_specs=[qt_spec, k_spec, vt_spec,
                  pl.BlockSpec((4, HEAD_DIM), lambda h, i: (0, 0)),
                  pl.BlockSpec((1, PAIR), lambda h, i: (0, 0))],
        out_specs=o_spec,
        out_shape=jax.ShapeDtypeStruct((s, D_MODEL), BF16),
        scratch_shapes=[pltpu.VMEM((PAIR, 2 * ATT_TQ), BF16)] + _pipeline_scratch(PAIR),
        compiler_params=pltpu.CompilerParams(
            dimension_semantics=("parallel", "arbitrary"), vmem_limit_bytes=VMEM_LIMIT),
        name="diff_attn",
    )(qt, k, vt, lam_vecs, subln.reshape(1, PAIR))


def _moba_bias_t(gate, qblk):
    n = lax.broadcasted_iota(jnp.int32, gate.shape, 0)
    nf = n.astype(F32)
    past = n < qblk
    g = jnp.where(past, gate, -jnp.inf)
    picked = jnp.zeros(gate.shape, F32)
    for _ in range(MOBA_TOPK):
        mx = jnp.max(g, axis=0, keepdims=True)
        first = jnp.min(jnp.where(g == mx, nf, float(PAIR)), axis=0, keepdims=True)
        hit = nf == first
        picked = jnp.where(hit, 1.0, picked)
        g = jnp.where(hit, -jnp.inf, g)
    sel_past = jnp.where(past, jnp.where(picked > 0.0, 0.0, MASK_NEG), MASK_NEG)
    return jnp.where(n == qblk, 0.0, sel_past)


def _moba_attn_kernel(qt_ref, k_ref, vt_ref, kmean_ref, o_ref,
                      qs_ref, bias_ref, s0_ref, s1_ref, mx0_ref, mx1_ref, m_ref, acc_ref):
    tq, tk = ATT_TQ, ATT_TK
    i = pl.program_id(1)

    qs = _stacked_queries_t(qt_ref)
    gate, rest = None, kmean_ref[...]
    for _ in range(3):
        piece = rest.astype(BF16)
        part = jnp.dot(piece, qs, preferred_element_type=F32)
        gate = part if gate is None else gate + part
        rest = rest - piece.astype(F32)
    col = lax.broadcasted_iota(jnp.int32, (1, 2 * tq), 1)
    qblk = (i * tq + jnp.where(col >= tq, col - tq, col)) // MOBA_BLOCK
    qs_ref[...] = qs
    bias_ref[...] = _moba_bias_t(gate, qblk)
    _init_stats(m_ref, acc_ref, M_FLOOR)

    def block_bias(j):
        return bias_ref, j * (tk // MOBA_BLOCK)

    def score_chunk(j, slot, key_offset, cols):
        _score_chunk(k_ref, pl.multiple_of(j * tk, tk), qs_ref, (s0_ref, s1_ref)[slot],
                     (mx0_ref, mx1_ref)[slot], key_offset, cols, bias=block_bias(j))

    def consume_chunk(j, slot, key_offset, cols):
        _consume_chunk((s0_ref, s1_ref)[slot], (mx0_ref, mx1_ref)[slot], vt_ref[j], m_ref, acc_ref,
                       key_offset, cols, split_heads=True, bias=block_bias(j))

    _run_kv_blocks(i, score_chunk, consume_chunk, MOBA_CW)

    ot = jnp.concatenate([_normalized(acc_ref, 0, tq), _normalized(acc_ref, tq, 2 * tq)], axis=0)
    o_ref[...] = ot.T.astype(BF16)


def _moba_attention(qt, k, vt, kmean):
    s = k.shape[0]
    qt_spec, k_spec, vt_spec, o_spec = _attn_specs(s)
    return pl.pallas_call(
        _moba_attn_kernel,
        grid=(N_PAIRS, s // ATT_TQ),
        in_specs=[qt_spec, k_spec, vt_spec,
                  pl.BlockSpec((s // MOBA_BLOCK, PAIR), lambda h, i: (0, h))],
        out_specs=o_spec,
        out_shape=jax.ShapeDtypeStruct((s, D_MODEL), BF16),
        scratch_shapes=[pltpu.VMEM((PAIR, 2 * ATT_TQ), BF16),
                        pltpu.VMEM((s // MOBA_BLOCK, 2 * ATT_TQ), F32)] + _pipeline_scratch(HEAD_DIM),
        compiler_params=pltpu.CompilerParams(
            dimension_semantics=("parallel", "arbitrary"), vmem_limit_bytes=VMEM_LIMIT),
        name="moba_attn",
    )(qt, k, vt, kmean)


def kernel(x, ffn_norm, ffn_w_in, ffn_w_out, mix_norm, diff_w_qkv, diff_lambda, diff_subln,
           diff_w_o, kv_norm, kv_w, moba_w_q, moba_w_o):
    b, s, d = x.shape
    assert b == 1 and d == D_MODEL and s % ATT_TQ == 0
    assert ATT_TK == 2 * MOBA_BLOCK and (s // MOBA_BLOCK) % 8 == 0
    assert ffn_w_in.shape[0] == 2 and diff_w_qkv.shape[0] == 1 and moba_w_q.shape[0] == 1
    tables = _rope_tables(s)
    w_in, w_out = ffn_w_in, ffn_w_out

    def ffn(xx, layer, slot, mixer_out=None):
        return _ffn_half(xx, ffn_norm, w_in, w_out, layer, slot, mixer_out)

    xs = x.reshape(s, d)

    xs = ffn(xs, 0, 0)
    qt, k, vt = _project(xs, mix_norm[0, 0], diff_w_qkv[0].astype(BF16), tables, n_rope=2,
                         scale0=Q_SCALE, transposed=(True, False, True), with_kmean=False)
    o = _diff_attention(qt, k, vt, diff_lambda[0], diff_subln[0])
    xs = ffn(xs, 0, 1, mixer_out=(o, diff_w_o[0].astype(BF16), mix_norm[0, 1]))

    k, vt, kmean = _project(xs, kv_norm, kv_w.astype(BF16), tables, n_rope=1, scale0=1.0,
                            transposed=(False, True), with_kmean=True)
    kmean = kmean.reshape(s // MOBA_BLOCK, d)
    xs = ffn(xs, 1, 0)
    (qt,) = _project(xs, mix_norm[1, 0], moba_w_q[0].astype(BF16), tables, n_rope=1,
                     scale0=Q_SCALE, transposed=(True,), with_kmean=False)
    o = _moba_attention(qt, k, vt, kmean)
    xs = ffn(xs, 1, 1, mixer_out=(o, moba_w_o[0].astype(BF16), mix_norm[1, 1]))
    return xs.reshape(b, s, d)
```

```python
import functools
import math

import jax
import jax.numpy as jnp
from jax import lax
from jax.experimental import pallas as pl
from jax.experimental.pallas import tpu as pltpu

F32 = jnp.float32
BF16 = jnp.bfloat16

D_MODEL = 1024
HEAD_DIM = 64
PAIR = 2 * HEAD_DIM
N_PAIRS = D_MODEL // PAIR
D_FF = 2816
ROPE_THETA = 10000.0
MOBA_BLOCK = 256
MOBA_TOPK = 3
FFN_RESIDUAL_SCALE = 0.5
NORM_EPS = 1e-6
DIFF_LAMBDA_INIT = 0.2
MASK_NEG = -1e30
M_FLOOR = -1e29
Q_SCALE = HEAD_DIM ** -0.5 * math.log2(math.e)

V7X_VMEM_BYTES = 64 * 1024 * 1024
V7X_MXU_WIDTH = 256
V7X_BF16_SUBLANES = 16

FFN_TM = 512
FFN_CW = V7X_MXU_WIDTH
PROJ_CW = V7X_MXU_WIDTH
ATT_TK = 512
ATT_TQ = 2 * ATT_TK
DIFF_CW = 2 * V7X_MXU_WIDTH
MOBA_CW = V7X_MXU_WIDTH
SUM_ROWS = V7X_BF16_SUBLANES
VMEM_LIMIT = V7X_VMEM_BYTES * 7 // 8


def _rms(x, g):
    return x * lax.rsqrt(jnp.mean(x * x, axis=-1, keepdims=True) + NORM_EPS) * g


def _ffn_kernel(*refs, with_mixer_out):
    if with_mixer_out:
        attn_ref, wmix_ref, gmix_ref, *refs = refs
    x_ref, gpre_ref, gpost_ref, win_ref, wout_ref, o_ref, act_ref = refs
    x = x_ref[...]
    if with_mixer_out:
        mixed = jnp.dot(attn_ref[...], wmix_ref[...], preferred_element_type=F32)
        x = x + _rms(mixed, gmix_ref[...])
    h = _rms(x, gpre_ref[...]).astype(BF16)
    for c in range(D_FF // FFN_CW):
        lo = c * FFN_CW
        w_gate = win_ref[:, lo:lo + FFN_CW].astype(BF16)
        w_up = win_ref[:, D_FF + lo:D_FF + lo + FFN_CW].astype(BF16)
        gate = jnp.dot(h, w_gate, preferred_element_type=F32)
        up = jnp.dot(h, w_up, preferred_element_type=F32)
        act_ref[:, lo:lo + FFN_CW] = (gate * (1.0 / (1.0 + jnp.exp(-gate))) * up).astype(BF16)
    act = act_ref[...]
    y = jnp.concatenate(
        [jnp.dot(act, wout_ref[:, n:n + FFN_CW].astype(BF16), preferred_element_type=F32)
         for n in range(0, D_MODEL, FFN_CW)], axis=1)
    o_ref[...] = x + FFN_RESIDUAL_SCALE * _rms(y, gpost_ref[...])


def _ffn_half(x, norms, w_in, w_out, layer, slot, mixer_out=None):
    s = x.shape[0]
    row = lambda i: (i, 0)
    const = lambda i: (0, 0)
    weights = lambda i: (layer, slot, 0, 0)
    resident = pl.Buffered(1)
    g_pre, g_post = norms[layer, slot, 0], norms[layer, slot, 1]
    operands = [x, g_pre.reshape(1, D_MODEL), g_post.reshape(1, D_MODEL), w_in, w_out]
    in_specs = [
        pl.BlockSpec((FFN_TM, D_MODEL), row),
        pl.BlockSpec((1, D_MODEL), const),
        pl.BlockSpec((1, D_MODEL), const),
        pl.BlockSpec((None, None, D_MODEL, 2 * D_FF), weights, pipeline_mode=resident),
        pl.BlockSpec((None, None, D_FF, D_MODEL), weights, pipeline_mode=resident),
    ]
    if mixer_out is not None:
        attn, w_o, g_mix = mixer_out
        operands = [attn, w_o, g_mix.reshape(1, D_MODEL)] + operands
        in_specs = [pl.BlockSpec((FFN_TM, D_MODEL), row),
                    pl.BlockSpec((D_MODEL, D_MODEL), const, pipeline_mode=resident),
                    pl.BlockSpec((1, D_MODEL), const)] + in_specs
    return pl.pallas_call(
        functools.partial(_ffn_kernel, with_mixer_out=mixer_out is not None),
        grid=(s // FFN_TM,),
        in_specs=in_specs,
        out_specs=pl.BlockSpec((FFN_TM, D_MODEL), row),
        out_shape=jax.ShapeDtypeStruct((s, D_MODEL), F32),
        scratch_shapes=[pltpu.VMEM((FFN_TM, D_FF), BF16)],
        compiler_params=pltpu.CompilerParams(
            dimension_semantics=("parallel",), vmem_limit_bytes=VMEM_LIMIT),
        name="ffn_half",
    )(*operands)


def _rope_tables(s):
    pos = jnp.arange(s, dtype=F32)
    inv_freq = 1.0 / (ROPE_THETA ** (jnp.arange(0, HEAD_DIM, 2, dtype=F32) / HEAD_DIM))
    ang = pos[:, None] * inv_freq[None, :]
    cos, sin = jnp.cos(ang)[:, None, :], jnp.sin(ang)[:, None, :]
    quarter = lambda coef: jnp.asarray(coef, F32).reshape(1, 4, 1)
    cos_t = (cos * quarter([1, 1, 1, 1])).reshape(s, PAIR)
    sin_lo = (sin * quarter([-1, 0, -1, 0])).reshape(s, PAIR)
    sin_hi = (sin * quarter([0, 1, 0, 1])).reshape(s, PAIR)
    return cos_t, sin_lo, sin_hi


def _proj_kernel(x_ref, g_ref, w_ref, cos_ref, sinlo_ref, sinhi_ref, *out_refs,
                 n_rope, scale0, transposed, with_kmean):
    n_chunks = len(transposed)
    h = _rms(x_ref[...], g_ref[...]).astype(BF16)
    tm = h.shape[0]
    reps = PROJ_CW // PAIR
    cos = jnp.concatenate([cos_ref[...]] * reps, axis=1)
    sin_lo = jnp.concatenate([sinlo_ref[...]] * reps, axis=1)
    sin_hi = jnp.concatenate([sinhi_ref[...]] * reps, axis=1)
    for c in range(n_chunks):
        for lo in range(0, D_MODEL, PROJ_CW):
            cols = slice(lo, lo + PROJ_CW)
            y = jnp.dot(h, w_ref[:, c * D_MODEL + lo:c * D_MODEL + lo + PROJ_CW],
                        preferred_element_type=F32)
            if c < n_rope:
                half = HEAD_DIM // 2
                y = (y * cos + pltpu.roll(y, PROJ_CW - half, 1) * sin_lo
                     + pltpu.roll(y, half, 1) * sin_hi)
                if c == 0 and with_kmean:
                    km = jnp.mean(y.reshape(tm // MOBA_BLOCK, MOBA_BLOCK, PROJ_CW), axis=1)
                    out_refs[n_chunks][0, :, cols] = km
            if c == 0 and scale0 != 1.0:
                y = y * scale0
            if transposed[c]:
                pairs = slice(lo // PAIR, (lo + PROJ_CW) // PAIR)
                out_refs[c][pairs] = y.T.reshape(reps, 1, PAIR, tm).astype(BF16)
            else:
                out_refs[c][:, cols] = y.astype(BF16)


def _project(x, g, w, tables, *, n_rope, scale0, transposed, with_kmean):
    s = x.shape[0]
    tm = ATT_TK
    row = lambda i: (i, 0)
    out_shape, out_specs = [], []
    for tr in transposed:
        if tr:
            out_shape.append(jax.ShapeDtypeStruct((N_PAIRS, s // tm, PAIR, tm), BF16))
            out_specs.append(pl.BlockSpec((N_PAIRS, 1, PAIR, tm), lambda i: (0, i, 0, 0)))
        else:
            out_shape.append(jax.ShapeDtypeStruct((s, D_MODEL), BF16))
            out_specs.append(pl.BlockSpec((tm, D_MODEL), row))
    if with_kmean:
        nb = tm // MOBA_BLOCK
        out_shape.append(jax.ShapeDtypeStruct((s // tm, nb, D_MODEL), F32))
        out_specs.append(pl.BlockSpec((1, nb, D_MODEL), lambda i: (i, 0, 0)))
    return pl.pallas_call(
        functools.partial(_proj_kernel, n_rope=n_rope, scale0=scale0, transposed=transposed,
                          with_kmean=with_kmean),
        grid=(s // tm,),
        in_specs=[
            pl.BlockSpec((tm, D_MODEL), row),
            pl.BlockSpec((1, D_MODEL), lambda i: (0, 0)),
            pl.BlockSpec(w.shape, lambda i: (0, 0)),
            pl.BlockSpec((tm, PAIR), row),
            pl.BlockSpec((tm, PAIR), row),
            pl.BlockSpec((tm, PAIR), row),
        ],
        out_specs=out_specs,
        out_shape=out_shape,
        compiler_params=pltpu.CompilerParams(
            dimension_semantics=("parallel",), vmem_limit_bytes=VMEM_LIMIT),
        name="project",
    )(x, g.reshape(1, D_MODEL), w, *tables)


def _stacked_queries_t(qt_ref):
    qt = jnp.concatenate([qt_ref[r] for r in range(qt_ref.shape[0])], axis=1)
    feat = lax.broadcasted_iota(jnp.int32, qt.shape, 0)
    zero = jnp.zeros_like(qt)
    return jnp.concatenate([jnp.where(feat < HEAD_DIM, qt, zero),
                            jnp.where(feat >= HEAD_DIM, qt, zero)], axis=1)


def _chunk_visibility(key_offset, cols):
    if key_offset is None:
        return "all"
    q_first, q_last = cols.start % ATT_TQ, (cols.stop - 1) % ATT_TQ
    if key_offset > q_last:
        return "none"
    if key_offset + ATT_TK - 1 <= q_first:
        return "all"
    return "part"


def _block_bias_rows(bias, cols):
    bias_ref, blk0 = bias
    return bias_ref[pl.ds(blk0, 1), cols], bias_ref[pl.ds(blk0 + 1, 1), cols]


def _score_chunk(k_ref, row0, q_ref, s_ref, mx_ref, key_offset, cols, bias=None):
    vis = _chunk_visibility(key_offset, cols)
    if vis == "none":
        return
    st = jnp.dot(k_ref[pl.ds(row0, ATT_TK), :], q_ref[:, cols], preferred_element_type=F32)
    if vis == "part":
        key = lax.broadcasted_iota(jnp.int32, st.shape, 0) + key_offset
        qpos = lax.broadcasted_iota(jnp.int32, st.shape, 1) + cols.start % ATT_TQ
        st = jnp.where(key <= qpos, st, -jnp.inf)
    s_ref[:, cols] = st
    if bias is None:
        mx_ref[:, cols] = jnp.max(st, axis=0, keepdims=True)
    else:
        b_top, b_bot = _block_bias_rows(bias, cols)
        mx_ref[:, cols] = jnp.maximum(
            jnp.max(st[0:MOBA_BLOCK], axis=0, keepdims=True) + b_top,
            jnp.max(st[MOBA_BLOCK:], axis=0, keepdims=True) + b_bot)


def _consume_chunk(s_ref, mx_ref, vt, m_ref, acc_ref, key_offset, cols, split_heads, bias=None):
    if _chunk_visibility(key_offset, cols) == "none":
        return
    m_prev = m_ref[:, cols]
    m_new = jnp.maximum(m_prev, mx_ref[:, cols])
    alpha = jnp.exp2(m_prev - m_new)
    if bias is None:
        p = jnp.exp2(s_ref[:, cols] - m_new).astype(BF16)
    else:
        b_top, b_bot = _block_bias_rows(bias, cols)
        p = jnp.concatenate([jnp.exp2(s_ref[0:MOBA_BLOCK, cols] - (m_new - b_top)),
                             jnp.exp2(s_ref[MOBA_BLOCK:, cols] - (m_new - b_bot))],
                            axis=0).astype(BF16)
    if split_heads:
        vt = vt[0:HEAD_DIM] if cols.start < ATT_TQ else vt[HEAD_DIM:]
    vt_sum = jnp.concatenate([vt, jnp.ones((SUM_ROWS, vt.shape[1]), BF16)], axis=0)
    acc_ref[:, cols] = alpha * acc_ref[:, cols] + jnp.dot(vt_sum, p, preferred_element_type=F32)
    m_ref[:, cols] = m_new


def _run_kv_blocks(i, score_chunk, consume_chunk, chunk_width):
    def fused(sc, cs):
        for c0 in range(0, 2 * ATT_TQ, chunk_width):
            cols = slice(c0, c0 + chunk_width)
            if sc is not None:
                score_chunk(*sc, cols)
            if cs is not None:
                consume_chunk(*cs, cols)

    @pl.when(i == 0)
    def _():
        fused((0, 0, 0), None)
        fused((1, 1, ATT_TK), (0, 0, 0))
        fused(None, (1, 1, ATT_TK))

    @pl.when(i > 0)
    def _():
        fused((0, 0, None), None)

        def pair(b):
            fused((b + 1, 1, None), (b, 0, None))
            fused((b + 2, 0, None), (b + 1, 1, None))

        def two_pairs(q, carry):
            pair(4 * q)
            pair(4 * q + 2)
            return carry

        n_double = lax.shift_right_logical(i - 1, 1)
        lax.fori_loop(0, n_double, two_pairs, 0)

        @pl.when(lax.rem(i - 1, 2) == 1)
        def _():
            pair(4 * n_double)

        b = 2 * i - 2
        fused((b + 1, 1, None), (b, 0, None))
        fused((b + 2, 0, 0), (b + 1, 1, None))
        fused((b + 3, 1, ATT_TK), (b + 2, 0, 0))
        fused(None, (b + 3, 1, ATT_TK))


def _init_stats(m_ref, acc_ref, m_init):
    m_ref[...] = jnp.full(m_ref.shape, m_init, F32)
    acc_ref[...] = jnp.zeros(acc_ref.shape, F32)


def _normalized(acc_ref, lo, hi):
    rows = acc_ref.shape[0] - SUM_ROWS
    return acc_ref[0:rows, lo:hi] / acc_ref[rows:rows + 1, lo:hi]


def _pipeline_scratch(value_rows):
    cols = 2 * ATT_TQ
    return [pltpu.VMEM((ATT_TK, cols), F32), pltpu.VMEM((ATT_TK, cols), F32),
            pltpu.VMEM((1, cols), F32), pltpu.VMEM((1, cols), F32),
            pltpu.VMEM((1, cols), F32), pltpu.VMEM((value_rows + SUM_ROWS, cols), F32)]


def _attn_specs(s):
    qt_spec = pl.BlockSpec((None, ATT_TQ // ATT_TK, PAIR, ATT_TK), lambda h, i: (h, i, 0, 0))
    k_spec = pl.BlockSpec((s, PAIR), lambda h, i: (0, h))
    vt_spec = pl.BlockSpec((None, s // ATT_TK, PAIR, ATT_TK), lambda h, i: (h, 0, 0, 0))
    o_spec = pl.BlockSpec((ATT_TQ, PAIR), lambda h, i: (i, h))
    return qt_spec, k_spec, vt_spec, o_spec


def _diff_attn_kernel(qt_ref, k_ref, vt_ref, lam_ref, subln_ref, o_ref,
                      qs_ref, s0_ref, s1_ref, mx0_ref, mx1_ref, m_ref, acc_ref):
    tq, tk = ATT_TQ, ATT_TK
    i = pl.program_id(1)
    qs_ref[...] = _stacked_queries_t(qt_ref)
    _init_stats(m_ref, acc_ref, -jnp.inf)

    def score_chunk(j, slot, key_offset, cols):
        _score_chunk(k_ref, pl.multiple_of(j * tk, tk), qs_ref, (s0_ref, s1_ref)[slot],
                     (mx0_ref, mx1_ref)[slot], key_offset, cols)

    def consume_chunk(j, slot, key_offset, cols):
        _consume_chunk((s0_ref, s1_ref)[slot], (mx0_ref, mx1_ref)[slot], vt_ref[j], m_ref, acc_ref,
                       key_offset, cols, split_heads=False)

    _run_kv_blocks(i, score_chunk, consume_chunk, DIFF_CW)

    lv = lam_ref[...]
    lam = (jnp.exp(jnp.sum(lv[0:1] * lv[1:2], axis=-1, keepdims=True))
           - jnp.exp(jnp.sum(lv[2:3] * lv[3:4], axis=-1, keepdims=True)) + DIFF_LAMBDA_INIT)
    ot = _normalized(acc_ref, 0, tq) - lam * _normalized(acc_ref, tq, 2 * tq)
    o_ref[...] = (_rms(ot.T, subln_ref[...]) * (1.0 - DIFF_LAMBDA_INIT)).astype(BF16)


def _diff_attention(qt, k, vt, lam_vecs, subln):
    s = k.shape[0]
    qt_spec, k_spec, vt_spec, o_spec = _attn_specs(s)
    return pl.pallas_call(
        _diff_attn_kernel,
        grid=(N_PAIRS, s // ATT_TQ),
        in_specs=[qt_spec, k_spec, vt_spec,
                  pl.BlockSpec((4, HEAD_DIM), lambda h, i: (0, 0)),
                  pl.BlockSpec((1, PAIR), lambda h, i: (0, 0))],
        out_specs=o_spec,
        out_shape=jax.ShapeDtypeStruct((s, D_MODEL), BF16),
        scratch_shapes=[pltpu.VMEM((PAIR, 2 * ATT_TQ), BF16)] + _pipeline_scratch(PAIR),
        compiler_params=pltpu.CompilerParams(
            dimension_semantics=("parallel", "arbitrary"), vmem_limit_bytes=VMEM_LIMIT),
        name="diff_attn",
    )(qt, k, vt, lam_vecs, subln.reshape(1, PAIR))


def _moba_bias_t(gate, qblk):
    n = lax.broadcasted_iota(jnp.int32, gate.shape, 0)
    nf = n.astype(F32)
    past = n < qblk
    g = jnp.where(past, gate, -jnp.inf)
    picked = jnp.zeros(gate.shape, F32)
    for _ in range(MOBA_TOPK):
        mx = jnp.max(g, axis=0, keepdims=True)
        first = jnp.min(jnp.where(g == mx, nf, float(PAIR)), axis=0, keepdims=True)
        hit = nf == first
        picked = jnp.where(hit, 1.0, picked)
        g = jnp.where(hit, -jnp.inf, g)
    sel_past = jnp.where(past, jnp.where(picked > 0.0, 0.0, MASK_NEG), MASK_NEG)
    return jnp.where(n == qblk, 0.0, sel_past)


def _moba_attn_kernel(qt_ref, k_ref, vt_ref, kmean_ref, o_ref,
                      qs_ref, bias_ref, s0_ref, s1_ref, mx0_ref, mx1_ref, m_ref, acc_ref):
    tq, tk = ATT_TQ, ATT_TK
    i = pl.program_id(1)

    qs = _stacked_queries_t(qt_ref)
    gate, rest = None, kmean_ref[...]
    for _ in range(3):
        piece = rest.astype(BF16)
        part = jnp.dot(piece, qs, preferred_element_type=F32)
        gate = part if gate is None else gate + part
        rest = rest - piece.astype(F32)
    col = lax.broadcasted_iota(jnp.int32, (1, 2 * tq), 1)
    qblk = (i * tq + jnp.where(col >= tq, col - tq, col)) // MOBA_BLOCK
    qs_ref[...] = qs
    bias_ref[...] = _moba_bias_t(gate, qblk)
    _init_stats(m_ref, acc_ref, M_FLOOR)

    def block_bias(j):
        return bias_ref, j * (tk // MOBA_BLOCK)

    def score_chunk(j, slot, key_offset, cols):
        _score_chunk(k_ref, pl.multiple_of(j * tk, tk), qs_ref, (s0_ref, s1_ref)[slot],
                     (mx0_ref, mx1_ref)[slot], key_offset, cols, bias=block_bias(j))

    def consume_chunk(j, slot, key_offset, cols):
        _consume_chunk((s0_ref, s1_ref)[slot], (mx0_ref, mx1_ref)[slot], vt_ref[j], m_ref, acc_ref,
                       key_offset, cols, split_heads=True, bias=block_bias(j))

    _run_kv_blocks(i, score_chunk, consume_chunk, MOBA_CW)

    ot = jnp.concatenate([_normalized(acc_ref, 0, tq), _normalized(acc_ref, tq, 2 * tq)], axis=0)
    o_ref[...] = ot.T.astype(BF16)


def _moba_attention(qt, k, vt, kmean):
    s = k.shape[0]
    qt_spec, k_spec, vt_spec, o_spec = _attn_specs(s)
    return pl.pallas_call(
        _moba_attn_kernel,
        grid=(N_PAIRS, s // ATT_TQ),
        in_specs=[qt_spec, k_spec, vt_spec,
                  pl.BlockSpec((s // MOBA_BLOCK, PAIR), lambda h, i: (0, h))],
        out_specs=o_spec,
        out_shape=jax.ShapeDtypeStruct((s, D_MODEL), BF16),
        scratch_shapes=[pltpu.VMEM((PAIR, 2 * ATT_TQ), BF16),
                        pltpu.VMEM((s // MOBA_BLOCK, 2 * ATT_TQ), F32)] + _pipeline_scratch(HEAD_DIM),
        compiler_params=pltpu.CompilerParams(
            dimension_semantics=("parallel", "arbitrary"), vmem_limit_bytes=VMEM_LIMIT),
        name="moba_attn",
    )(qt, k, vt, kmean)


def kernel(x, ffn_norm, ffn_w_in, ffn_w_out, mix_norm, diff_w_qkv, diff_lambda, diff_subln,
           diff_w_o, kv_norm, kv_w, moba_w_q, moba_w_o):
    b, s, d = x.shape
    assert b == 1 and d == D_MODEL and s % ATT_TQ == 0
    assert ATT_TK == 2 * MOBA_BLOCK and (s // MOBA_BLOCK) % 8 == 0
    assert ffn_w_in.shape[0] == 2 and diff_w_qkv.shape[0] == 1 and moba_w_q.shape[0] == 1
    tables = _rope_tables(s)
    w_in, w_out = ffn_w_in, ffn_w_out

    def ffn(xx, layer, slot, mixer_out=None):
        return _ffn_half(xx, ffn_norm, w_in, w_out, layer, slot, mixer_out)

    xs = x.reshape(s, d)

    xs = ffn(xs, 0, 0)
    qt, k, vt = _project(xs, mix_norm[0, 0], diff_w_qkv[0].astype(BF16), tables, n_rope=2,
                         scale0=Q_SCALE, transposed=(True, False, True), with_kmean=False)
    o = _diff_attention(qt, k, vt, diff_lambda[0], diff_subln[0])
    xs = ffn(xs, 0, 1, mixer_out=(o, diff_w_o[0].astype(BF16), mix_norm[0, 1]))

    k, vt, kmean = _project(xs, kv_norm, kv_w.astype(BF16), tables, n_rope=1, scale0=1.0,
                            transposed=(False, True), with_kmean=True)
    kmean = kmean.reshape(s // MOBA_BLOCK, d)
    xs = ffn(xs, 1, 0)
    (qt,) = _project(xs, mix_norm[1, 0], moba_w_q[0].astype(BF16), tables, n_rope=1,
                     scale0=Q_SCALE, transposed=(True,), with_kmean=False)
    o = _moba_attention(qt, k, vt, kmean)
    xs = ffn(xs, 1, 1, mixer_out=(o, moba_w_o[0].astype(BF16), mix_norm[1, 1]))
    return xs.reshape(b, s, d)
```

```python
import functools
import math

import jax
import jax.numpy as jnp
from jax import lax
from jax.experimental import pallas as pl
from jax.experimental.pallas import tpu as pltpu

F32 = jnp.float32
BF16 = jnp.bfloat16

D_MODEL = 1024
HEAD_DIM = 64
PAIR = 2 * HEAD_DIM
N_PAIRS = D_MODEL // PAIR
D_FF = 2816
ROPE_THETA = 10000.0
MOBA_BLOCK = 256
MOBA_TOPK = 3
FFN_RESIDUAL_SCALE = 0.5
NORM_EPS = 1e-6
DIFF_LAMBDA_INIT = 0.2
MASK_NEG = -1e30
M_FLOOR = -1e29
Q_SCALE = HEAD_DIM ** -0.5 * math.log2(math.e)

V7X_VMEM_BYTES = 64 * 1024 * 1024
V7X_MXU_WIDTH = 256
V7X_BF16_SUBLANES = 16

FFN_TM = 512
FFN_CW = V7X_MXU_WIDTH
PROJ_CW = V7X_MXU_WIDTH
ATT_TK = 512
ATT_TQ = 2 * ATT_TK
DIFF_CW = 2 * V7X_MXU_WIDTH
MOBA_CW = V7X_MXU_WIDTH
SUM_ROWS = V7X_BF16_SUBLANES
VMEM_LIMIT = V7X_VMEM_BYTES * 7 // 8


def _rms(x, g):
    return x * lax.rsqrt(jnp.mean(x * x, axis=-1, keepdims=True) + NORM_EPS) * g


def _ffn_kernel(*refs, with_mixer_out):
    if with_mixer_out:
        attn_ref, wmix_ref, gmix_ref, *refs = refs
    x_ref, gpre_ref, gpost_ref, win_ref, wout_ref, o_ref, act_ref = refs
    x = x_ref[...]
    if with_mixer_out:
        mixed = jnp.dot(attn_ref[...], wmix_ref[...], preferred_element_type=F32)
        x = x + _rms(mixed, gmix_ref[...])
    h = _rms(x, gpre_ref[...]).astype(BF16)
    for c in range(D_FF // FFN_CW):
        lo = c * FFN_CW
        w_gate = win_ref[:, lo:lo + FFN_CW].astype(BF16)
        w_up = win_ref[:, D_FF + lo:D_FF + lo + FFN_CW].astype(BF16)
        gate = jnp.dot(h, w_gate, preferred_element_type=F32)
        up = jnp.dot(h, w_up, preferred_element_type=F32)
        act_ref[:, lo:lo + FFN_CW] = (gate * (1.0 / (1.0 + jnp.exp(-gate))) * up).astype(BF16)
    act = act_ref[...]
    y = jnp.concatenate(
        [jnp.dot(act, wout_ref[:, n:n + FFN_CW].astype(BF16), preferred_element_type=F32)
         for n in range(0, D_MODEL, FFN_CW)], axis=1)
    o_ref[...] = x + FFN_RESIDUAL_SCALE * _rms(y, gpost_ref[...])


def _ffn_half(x, norms, w_in, w_out, layer, slot, mixer_out=None):
    s = x.shape[0]
    row = lambda i: (i, 0)
    const = lambda i: (0, 0)
    weights = lambda i: (layer, slot, 0, 0)
    resident = pl.Buffered(1)
    g_pre, g_post = norms[layer, slot, 0], norms[layer, slot, 1]
    operands = [x, g_pre.reshape(1, D_MODEL), g_post.reshape(1, D_MODEL), w_in, w_out]
    in_specs = [
        pl.BlockSpec((FFN_TM, D_MODEL), row),
        pl.BlockSpec((1, D_MODEL), const),
        pl.BlockSpec((1, D_MODEL), const),
        pl.BlockSpec((None, None, D_MODEL, 2 * D_FF), weights, pipeline_mode=resident),
        pl.BlockSpec((None, None, D_FF, D_MODEL), weights, pipeline_mode=resident),
    ]
    if mixer_out is not None:
        attn, w_o, g_mix = mixer_out
        operands = [attn, w_o, g_mix.reshape(1, D_MODEL)] + operands
        in_specs = [pl.BlockSpec((FFN_TM, D_MODEL), row),
                    pl.BlockSpec((D_MODEL, D_MODEL), const, pipeline_mode=resident),
                    pl.BlockSpec((1, D_MODEL), const)] + in_specs
    return pl.pallas_call(
        functools.partial(_ffn_kernel, with_mixer_out=mixer_out is not None),
        grid=(s // FFN_TM,),
        in_specs=in_specs,
        out_specs=pl.BlockSpec((FFN_TM, D_MODEL), row),
        out_shape=jax.ShapeDtypeStruct((s, D_MODEL), F32),
        scratch_shapes=[pltpu.VMEM((FFN_TM, D_FF), BF16)],
        compiler_params=pltpu.CompilerParams(
            dimension_semantics=("parallel",), vmem_limit_bytes=VMEM_LIMIT),
        name="ffn_half",
    )(*operands)


def _rope_tables(s):
    pos = jnp.arange(s, dtype=F32)
    inv_freq = 1.0 / (ROPE_THETA ** (jnp.arange(0, HEAD_DIM, 2, dtype=F32) / HEAD_DIM))
    ang = pos[:, None] * inv_freq[None, :]
    cos, sin = jnp.cos(ang)[:, None, :], jnp.sin(ang)[:, None, :]
    quarter = lambda coef: jnp.asarray(coef, F32).reshape(1, 4, 1)
    cos_t = (cos * quarter([1, 1, 1, 1])).reshape(s, PAIR)
    sin_lo = (sin * quarter([-1, 0, -1, 0])).reshape(s, PAIR)
    sin_hi = (sin * quarter([0, 1, 0, 1])).reshape(s, PAIR)
    return cos_t, sin_lo, sin_hi


def _proj_kernel(x_ref, g_ref, w_ref, cos_ref, sinlo_ref, sinhi_ref, *out_refs,
                 n_rope, scale0, transposed, with_kmean):
    n_chunks = len(transposed)
    h = _rms(x_ref[...], g_ref[...]).astype(BF16)
    tm = h.shape[0]
    reps = PROJ_CW // PAIR
    cos = jnp.concatenate([cos_ref[...]] * reps, axis=1)
    sin_lo = jnp.concatenate([sinlo_ref[...]] * reps, axis=1)
    sin_hi = jnp.concatenate([sinhi_ref[...]] * reps, axis=1)
    for c in range(n_chunks):
        for lo in range(0, D_MODEL, PROJ_CW):
            cols = slice(lo, lo + PROJ_CW)
            y = jnp.dot(h, w_ref[:, c * D_MODEL + lo:c * D_MODEL + lo + PROJ_CW],
                        preferred_element_type=F32)
            if c < n_rope:
                half = HEAD_DIM // 2
                y = (y * cos + pltpu.roll(y, PROJ_CW - half, 1) * sin_lo
                     + pltpu.roll(y, half, 1) * sin_hi)
                if c == 0 and with_kmean:
                    km = jnp.mean(y.reshape(tm // MOBA_BLOCK, MOBA_BLOCK, PROJ_CW), axis=1)
                    out_refs[n_chunks][0, :, cols] = km
            if c == 0 and scale0 != 1.0:
                y = y * scale0
            if transposed[c]:
                pairs = slice(lo // PAIR, (lo + PROJ_CW) // PAIR)
                out_refs[c][pairs] = y.T.reshape(reps, 1, PAIR, tm).astype(BF16)
            else:
                out_refs[c][:, cols] = y.astype(BF16)


def _project(x, g, w, tables, *, n_rope, scale0, transposed, with_kmean):
    s = x.shape[0]
    tm = ATT_TK
    row = lambda i: (i, 0)
    out_shape, out_specs = [], []
    for tr in transposed:
        if tr:
            out_shape.append(jax.ShapeDtypeStruct((N_PAIRS, s // tm, PAIR, tm), BF16))
            out_specs.append(pl.BlockSpec((N_PAIRS, 1, PAIR, tm), lambda i: (0, i, 0, 0)))
        else:
            out_shape.append(jax.ShapeDtypeStruct((s, D_MODEL), BF16))
            out_specs.append(pl.BlockSpec((tm, D_MODEL), row))
    if with_kmean:
        nb = tm // MOBA_BLOCK
        out_shape.append(jax.ShapeDtypeStruct((s // tm, nb, D_MODEL), F32))
        out_specs.append(pl.BlockSpec((1, nb, D_MODEL), lambda i: (i, 0, 0)))
    return pl.pallas_call(
        functools.partial(_proj_kernel, n_rope=n_rope, scale0=scale0, transposed=transposed,
                          with_kmean=with_kmean),
        grid=(s // tm,),
        in_specs=[
            pl.BlockSpec((tm, D_MODEL), row),
            pl.BlockSpec((1, D_MODEL), lambda i: (0, 0)),
            pl.BlockSpec(w.shape, lambda i: (0, 0)),
            pl.BlockSpec((tm, PAIR), row),
            pl.BlockSpec((tm, PAIR), row),
            pl.BlockSpec((tm, PAIR), row),
        ],
        out_specs=out_specs,
        out_shape=out_shape,
        compiler_params=pltpu.CompilerParams(
            dimension_semantics=("parallel",), vmem_limit_bytes=VMEM_LIMIT),
        name="project",
    )(x, g.reshape(1, D_MODEL), w, *tables)


def _stacked_queries_t(qt_ref):
    qt = jnp.concatenate([qt_ref[r] for r in range(qt_ref.shape[0])], axis=1)
    feat = lax.broadcasted_iota(jnp.int32, qt.shape, 0)
    zero = jnp.zeros_like(qt)
    return jnp.concatenate([jnp.where(feat < HEAD_DIM, qt, zero),
                            jnp.where(feat >= HEAD_DIM, qt, zero)], axis=1)


def _chunk_visibility(key_offset, cols):
    if key_offset is None:
        return "all"
    q_first, q_last = cols.start % ATT_TQ, (cols.stop - 1) % ATT_TQ
    if key_offset > q_last:
        return "none"
    if key_offset + ATT_TK - 1 <= q_first:
        return "all"
    return "part"


def _block_bias_rows(bias, cols):
    bias_ref, parity, blk0 = bias
    return bias_ref[parity, pl.ds(blk0, 1), cols], bias_ref[parity, pl.ds(blk0 + 1, 1), cols]


def _score_chunk(k_ref, row0, q_ref, s_ref, mx_ref, key_offset, cols, bias=None):
    vis = _chunk_visibility(key_offset, cols)
    if vis == "none":
        return
    st = jnp.dot(k_ref[pl.ds(row0, ATT_TK), :], q_ref[:, cols], preferred_element_type=F32)
    if vis == "part":
        key = lax.broadcasted_iota(jnp.int32, st.shape, 0) + key_offset
        qpos = lax.broadcasted_iota(jnp.int32, st.shape, 1) + cols.start % ATT_TQ
        st = jnp.where(key <= qpos, st, -jnp.inf)
    s_ref[:, cols] = st
    if bias is None:
        mx_ref[:, cols] = jnp.max(st, axis=0, keepdims=True)
    else:
        b_top, b_bot = _block_bias_rows(bias, cols)
        mx_ref[:, cols] = jnp.maximum(
            jnp.max(st[0:MOBA_BLOCK], axis=0, keepdims=True) + b_top,
            jnp.max(st[MOBA_BLOCK:], axis=0, keepdims=True) + b_bot)


def _consume_chunk(s_ref, mx_ref, vt, m_ref, acc_ref, key_offset, cols, split_heads, bias=None):
    if _chunk_visibility(key_offset, cols) == "none":
        return
    m_prev = m_ref[:, cols]
    m_new = jnp.maximum(m_prev, mx_ref[:, cols])
    alpha = jnp.exp2(m_prev - m_new)
    if bias is None:
        p = jnp.exp2(s_ref[:, cols] - m_new).astype(BF16)
    else:
        b_top, b_bot = _block_bias_rows(bias, cols)
        p = jnp.concatenate([jnp.exp2(s_ref[0:MOBA_BLOCK, cols] - (m_new - b_top)),
                             jnp.exp2(s_ref[MOBA_BLOCK:, cols] - (m_new - b_bot))],
                            axis=0).astype(BF16)
    if split_heads:
        vt = vt[0:HEAD_DIM] if cols.start < ATT_TQ else vt[HEAD_DIM:]
    vt_sum = jnp.concatenate([vt, jnp.ones((SUM_ROWS, vt.shape[1]), BF16)], axis=0)
    acc_ref[:, cols] = alpha * acc_ref[:, cols] + jnp.dot(vt_sum, p, preferred_element_type=F32)
    m_ref[:, cols] = m_new


def _run_kv_blocks(i, n_tiles, score_chunk, consume_chunk, prepare_next, chunk_width):
    def fused(sc, cs):
        for c0 in range(0, 2 * ATT_TQ, chunk_width):
            cols = slice(c0, c0 + chunk_width)
            if sc is not None:
                score_chunk(*sc, cols)
            if cs is not None:
                consume_chunk(*cs, cols)

    def last_block(cs):
        @pl.when(i + 1 < n_tiles)
        def _():
            prepare_next()
            fused((1, 0, 0, None), cs)

        @pl.when(i + 1 == n_tiles)
        def _():
            fused(None, cs)

    @pl.when(i == 0)
    def _():
        fused((0, 0, 0, 0), None)
        fused((0, 1, 1, ATT_TK), (0, 0, 0))
        last_block((1, 1, ATT_TK))

    @pl.when(i > 0)
    def _():
        def pair(b):
            fused((0, b + 1, 1, None), (b, 0, None))
            fused((0, b + 2, 0, None), (b + 1, 1, None))

        def two_pairs(q, carry):
            pair(4 * q)
            pair(4 * q + 2)
            return carry

        n_double = lax.shift_right_logical(i - 1, 1)
        lax.fori_loop(0, n_double, two_pairs, 0)

        @pl.when(lax.rem(i - 1, 2) == 1)
        def _():
            pair(4 * n_double)

        b = 2 * i - 2
        fused((0, b + 1, 1, None), (b, 0, None))
        fused((0, b + 2, 0, 0), (b + 1, 1, None))
        fused((0, b + 3, 1, ATT_TK), (b + 2, 0, 0))
        last_block((b + 3, 1, ATT_TK))


def _init_stats(m_ref, acc_ref, m_init):
    m_ref[...] = jnp.full(m_ref.shape, m_init, F32)
    acc_ref[...] = jnp.zeros(acc_ref.shape, F32)


def _normalized(acc_ref, lo, hi):
    rows = acc_ref.shape[0] - SUM_ROWS
    return acc_ref[0:rows, lo:hi] / acc_ref[rows:rows + 1, lo:hi]


def _pipeline_scratch(value_rows):
    cols = 2 * ATT_TQ
    return [pltpu.VMEM((ATT_TK, cols), F32), pltpu.VMEM((ATT_TK, cols), F32),
            pltpu.VMEM((1, cols), F32), pltpu.VMEM((1, cols), F32),
            pltpu.VMEM((1, cols), F32), pltpu.VMEM((value_rows + SUM_ROWS, cols), F32)]


def _attn_specs(s):
    last = s // ATT_TQ - 1
    qt_spec = pl.BlockSpec((None, ATT_TQ // ATT_TK, PAIR, ATT_TK), lambda h, i: (h, i, 0, 0))
    qt_next_spec = pl.BlockSpec((None, ATT_TQ // ATT_TK, PAIR, ATT_TK),
                                lambda h, i: (h, jnp.minimum(i + 1, last), 0, 0))
    k_spec = pl.BlockSpec((s, PAIR), lambda h, i: (0, h))
    vt_spec = pl.BlockSpec((None, s // ATT_TK, PAIR, ATT_TK), lambda h, i: (h, 0, 0, 0))
    o_spec = pl.BlockSpec((ATT_TQ, PAIR), lambda h, i: (i, h))
    return qt_spec, qt_next_spec, k_spec, vt_spec, o_spec


def _diff_attn_kernel(qt_ref, qt_next_ref, k_ref, vt_ref, lam_ref, subln_ref, o_ref,
                      qs_ref, s0_ref, s1_ref, mx0_ref, mx1_ref, m_ref, acc_ref):
    tq, tk = ATT_TQ, ATT_TK
    i = pl.program_id(1)

    @pl.when(i == 0)
    def _():
        qs_ref[...] = _stacked_queries_t(qt_ref)

    def prepare_next():
        qs_ref[...] = _stacked_queries_t(qt_next_ref)

    _init_stats(m_ref, acc_ref, -jnp.inf)

    def score_chunk(tile, j, slot, key_offset, cols):
        _score_chunk(k_ref, pl.multiple_of(j * tk, tk), qs_ref, (s0_ref, s1_ref)[slot],
                     (mx0_ref, mx1_ref)[slot], key_offset, cols)

    def consume_chunk(j, slot, key_offset, cols):
        _consume_chunk((s0_ref, s1_ref)[slot], (mx0_ref, mx1_ref)[slot], vt_ref[j], m_ref, acc_ref,
                       key_offset, cols, split_heads=False)

    _run_kv_blocks(i, pl.num_programs(1), score_chunk, consume_chunk, prepare_next, DIFF_CW)

    lv = lam_ref[...]
    lam = (jnp.exp(jnp.sum(lv[0:1] * lv[1:2], axis=-1, keepdims=True))
           - jnp.exp(jnp.sum(lv[2:3] * lv[3:4], axis=-1, keepdims=True)) + DIFF_LAMBDA_INIT)
    ot = _normalized(acc_ref, 0, tq) - lam * _normalized(acc_ref, tq, 2 * tq)
    o_ref[...] = (_rms(ot.T, subln_ref[...]) * (1.0 - DIFF_LAMBDA_INIT)).astype(BF16)


def _diff_attention(qt, k, vt, lam_vecs, subln):
    s = k.shape[0]
    qt_spec, qt_next_spec, k_spec, vt_spec, o_spec = _attn_specs(s)
    return pl.pallas_call(
        _diff_attn_kernel,
        grid=(N_PAIRS, s // ATT_TQ),
        in_specs=[qt_spec, qt_next_spec, k_spec, vt_spec,
                  pl.BlockSpec((4, HEAD_DIM), lambda h, i: (0, 0)),
                  pl.BlockSpec((1, PAIR), lambda h, i: (0, 0))],
        out_specs=o_spec,
        out_shape=jax.ShapeDtypeStruct((s, D_MODEL), BF16),
        scratch_shapes=[pltpu.VMEM((PAIR, 2 * ATT_TQ), BF16)] + _pipeline_scratch(PAIR),
        compiler_params=pltpu.CompilerParams(
            dimension_semantics=("arbitrary", "arbitrary"), vmem_limit_bytes=VMEM_LIMIT),
        name="diff_attn",
    )(qt, qt, k, vt, lam_vecs, subln.reshape(1, PAIR))


def _moba_bias_t(gate, qblk):
    n = lax.broadcasted_iota(jnp.int32, gate.shape, 0)
    nf = n.astype(F32)
    past = n < qblk
    g = jnp.where(past, gate, -jnp.inf)
    picked = jnp.zeros(gate.shape, F32)
    for _ in range(MOBA_TOPK):
        mx = jnp.max(g, axis=0, keepdims=True)
        first = jnp.min(jnp.where(g == mx, nf, float(PAIR)), axis=0, keepdims=True)
        hit = nf == first
        picked = jnp.where(hit, 1.0, picked)
        g = jnp.where(hit, -jnp.inf, g)
    sel_past = jnp.where(past, jnp.where(picked > 0.0, 0.0, MASK_NEG), MASK_NEG)
    return jnp.where(n == qblk, 0.0, sel_past)


def _moba_attn_kernel(qt_ref, qt_next_ref, k_ref, vt_ref, kmean_ref, o_ref,
                      qs_ref, bias_ref, s0_ref, s1_ref, mx0_ref, mx1_ref, m_ref, acc_ref):
    tq, tk = ATT_TQ, ATT_TK
    i = pl.program_id(1)
    parity = lax.rem(i, 2)

    def prepare(q_ref, tile, tile_parity):
        qs = _stacked_queries_t(q_ref)
        gate, rest = None, kmean_ref[...]
        for _ in range(3):
            piece = rest.astype(BF16)
            part = jnp.dot(piece, qs, preferred_element_type=F32)
            gate = part if gate is None else gate + part
            rest = rest - piece.astype(F32)
        col = lax.broadcasted_iota(jnp.int32, (1, 2 * tq), 1)
        qblk = (tile * tq + jnp.where(col >= tq, col - tq, col)) // MOBA_BLOCK
        qs_ref[...] = qs
        bias_ref[tile_parity] = _moba_bias_t(gate, qblk)

    @pl.when(i == 0)
    def _():
        prepare(qt_ref, i, parity)

    def prepare_next():
        prepare(qt_next_ref, i + 1, 1 - parity)

    _init_stats(m_ref, acc_ref, M_FLOOR)

    def block_bias(tile_parity, j):
        return bias_ref, tile_parity, j * (tk // MOBA_BLOCK)

    def score_chunk(tile, j, slot, key_offset, cols):
        tile_parity = parity if tile == 0 else 1 - parity
        _score_chunk(k_ref, pl.multiple_of(j * tk, tk), qs_ref, (s0_ref, s1_ref)[slot],
                     (mx0_ref, mx1_ref)[slot], key_offset, cols, bias=block_bias(tile_parity, j))

    def consume_chunk(j, slot, key_offset, cols):
        _consume_chunk((s0_ref, s1_ref)[slot], (mx0_ref, mx1_ref)[slot], vt_ref[j], m_ref, acc_ref,
                       key_offset, cols, split_heads=True, bias=block_bias(parity, j))

    _run_kv_blocks(i, pl.num_programs(1), score_chunk, consume_chunk, prepare_next, MOBA_CW)

    ot = jnp.concatenate([_normalized(acc_ref, 0, tq), _normalized(acc_ref, tq, 2 * tq)], axis=0)
    o_ref[...] = ot.T.astype(BF16)


def _moba_attention(qt, k, vt, kmean):
    s = k.shape[0]
    qt_spec, qt_next_spec, k_spec, vt_spec, o_spec = _attn_specs(s)
    return pl.pallas_call(
        _moba_attn_kernel,
        grid=(N_PAIRS, s // ATT_TQ),
        in_specs=[qt_spec, qt_next_spec, k_spec, vt_spec,
                  pl.BlockSpec((s // MOBA_BLOCK, PAIR), lambda h, i: (0, h))],
        out_specs=o_spec,
        out_shape=jax.ShapeDtypeStruct((s, D_MODEL), BF16),
        scratch_shapes=[pltpu.VMEM((PAIR, 2 * ATT_TQ), BF16),
                        pltpu.VMEM((2, s // MOBA_BLOCK, 2 * ATT_TQ), F32)]
        + _pipeline_scratch(HEAD_DIM),
        compiler_params=pltpu.CompilerParams(
            dimension_semantics=("arbitrary", "arbitrary"), vmem_limit_bytes=VMEM_LIMIT),
        name="moba_attn",
    )(qt, qt, k, vt, kmean)


def kernel(x, ffn_norm, ffn_w_in, ffn_w_out, mix_norm, diff_w_qkv, diff_lambda, diff_subln,
           diff_w_o, kv_norm, kv_w, moba_w_q, moba_w_o):
    b, s, d = x.shape
    assert b == 1 and d == D_MODEL and s % ATT_TQ == 0
    assert ATT_TK == 2 * MOBA_BLOCK and (s // MOBA_BLOCK) % 8 == 0
    assert ffn_w_in.shape[0] == 2 and diff_w_qkv.shape[0] == 1 and moba_w_q.shape[0] == 1
    tables = _rope_tables(s)
    w_in, w_out = ffn_w_in, ffn_w_out

    def ffn(xx, layer, slot, mixer_out=None):
        return _ffn_half(xx, ffn_norm, w_in, w_out, layer, slot, mixer_out)

    xs = x.reshape(s, d)

    xs = ffn(xs, 0, 0)
    qt, k, vt = _project(xs, mix_norm[0, 0], diff_w_qkv[0].astype(BF16), tables, n_rope=2,
                         scale0=Q_SCALE, transposed=(True, False, True), with_kmean=False)
    o = _diff_attention(qt, k, vt, diff_lambda[0], diff_subln[0])
    xs = ffn(xs, 0, 1, mixer_out=(o, diff_w_o[0].astype(BF16), mix_norm[0, 1]))

    k, vt, kmean = _project(xs, kv_norm, kv_w.astype(BF16), tables, n_rope=1, scale0=1.0,
                            transposed=(False, True), with_kmean=True)
    kmean = kmean.reshape(s // MOBA_BLOCK, d)
    xs = ffn(xs, 1, 0)
    (qt,) = _project(xs, mix_norm[1, 0], moba_w_q[0].astype(BF16), tables, n_rope=1,
                     scale0=Q_SCALE, transposed=(True,), with_kmean=False)
    o = _moba_attention(qt, k, vt, kmean)
    xs = ffn(xs, 1, 1, mixer_out=(o, moba_w_o[0].astype(BF16), mix_norm[1, 1]))
    return xs.reshape(b, s, d)
```

```python
import functools
import math

import jax
import jax.numpy as jnp
from jax import lax
from jax.experimental import pallas as pl
from jax.experimental.pallas import tpu as pltpu

F32 = jnp.float32
BF16 = jnp.bfloat16

D_MODEL = 1024
HEAD_DIM = 64
PAIR = 2 * HEAD_DIM
N_PAIRS = D_MODEL // PAIR
D_FF = 2816
ROPE_THETA = 10000.0
MOBA_BLOCK = 256
MOBA_TOPK = 3
FFN_RESIDUAL_SCALE = 0.5
NORM_EPS = 1e-6
DIFF_LAMBDA_INIT = 0.2
MASK_NEG = -1e30
M_FLOOR = -1e29
Q_SCALE = HEAD_DIM ** -0.5 * math.log2(math.e)

V7X_VMEM_BYTES = 64 * 1024 * 1024
V7X_MXU_WIDTH = 256
V7X_BF16_SUBLANES = 16

FFN_TM = 512
FFN_CW = V7X_MXU_WIDTH
PROJ_CW = V7X_MXU_WIDTH
ATT_TK = 512
ATT_TQ = 2 * ATT_TK
DIFF_CW = 2 * V7X_MXU_WIDTH
MOBA_CW = V7X_MXU_WIDTH
SUM_ROWS = V7X_BF16_SUBLANES
VMEM_LIMIT = V7X_VMEM_BYTES * 7 // 8


def _rms(x, g):
    return x * lax.rsqrt(jnp.mean(x * x, axis=-1, keepdims=True) + NORM_EPS) * g


def _ffn_kernel(*refs, with_mixer_out):
    if with_mixer_out:
        attn_ref, wmix_ref, gmix_ref, *refs = refs
    x_ref, gpre_ref, gpost_ref, win_ref, wout_ref, o_ref, act_ref = refs
    x = x_ref[...]
    if with_mixer_out:
        mixed = jnp.dot(attn_ref[...], wmix_ref[...], preferred_element_type=F32)
        x = x + _rms(mixed, gmix_ref[...])
    h = _rms(x, gpre_ref[...]).astype(BF16)
    for c in range(D_FF // FFN_CW):
        lo = c * FFN_CW
        w_gate = win_ref[:, lo:lo + FFN_CW].astype(BF16)
        w_up = win_ref[:, D_FF + lo:D_FF + lo + FFN_CW].astype(BF16)
        gate = jnp.dot(h, w_gate, preferred_element_type=F32)
        up = jnp.dot(h, w_up, preferred_element_type=F32)
        act_ref[:, lo:lo + FFN_CW] = (gate * (1.0 / (1.0 + jnp.exp(-gate))) * up).astype(BF16)
    act = act_ref[...]
    y = jnp.concatenate(
        [jnp.dot(act, wout_ref[:, n:n + FFN_CW].astype(BF16), preferred_element_type=F32)
         for n in range(0, D_MODEL, FFN_CW)], axis=1)
    o_ref[...] = x + FFN_RESIDUAL_SCALE * _rms(y, gpost_ref[...])


def _ffn_half(x, norms, w_in, w_out, layer, slot, mixer_out=None):
    s = x.shape[0]
    row = lambda i: (i, 0)
    const = lambda i: (0, 0)
    weights = lambda i: (layer, slot, 0, 0)
    resident = pl.Buffered(1)
    g_pre, g_post = norms[layer, slot, 0], norms[layer, slot, 1]
    operands = [x, g_pre.reshape(1, D_MODEL), g_post.reshape(1, D_MODEL), w_in, w_out]
    in_specs = [
        pl.BlockSpec((FFN_TM, D_MODEL), row),
        pl.BlockSpec((1, D_MODEL), const),
        pl.BlockSpec((1, D_MODEL), const),
        pl.BlockSpec((None, None, D_MODEL, 2 * D_FF), weights, pipeline_mode=resident),
        pl.BlockSpec((None, None, D_FF, D_MODEL), weights, pipeline_mode=resident),
    ]
    if mixer_out is not None:
        attn, w_o, g_mix = mixer_out
        operands = [attn, w_o, g_mix.reshape(1, D_MODEL)] + operands
        in_specs = [pl.BlockSpec((FFN_TM, D_MODEL), row),
                    pl.BlockSpec((D_MODEL, D_MODEL), const, pipeline_mode=resident),
                    pl.BlockSpec((1, D_MODEL), const)] + in_specs
    return pl.pallas_call(
        functools.partial(_ffn_kernel, with_mixer_out=mixer_out is not None),
        grid=(s // FFN_TM,),
        in_specs=in_specs,
        out_specs=pl.BlockSpec((FFN_TM, D_MODEL), row),
        out_shape=jax.ShapeDtypeStruct((s, D_MODEL), F32),
        scratch_shapes=[pltpu.VMEM((FFN_TM, D_FF), BF16)],
        compiler_params=pltpu.CompilerParams(
            dimension_semantics=("parallel",), vmem_limit_bytes=VMEM_LIMIT),
        name="ffn_half",
    )(*operands)


def _rope_tables(s):
    pos = jnp.arange(s, dtype=F32)
    inv_freq = 1.0 / (ROPE_THETA ** (jnp.arange(0, HEAD_DIM, 2, dtype=F32) / HEAD_DIM))
    ang = pos[:, None] * inv_freq[None, :]
    cos, sin = jnp.cos(ang)[:, None, :], jnp.sin(ang)[:, None, :]
    quarter = lambda coef: jnp.asarray(coef, F32).reshape(1, 4, 1)
    cos_t = (cos * quarter([1, 1, 1, 1])).reshape(s, PAIR)
    sin_lo = (sin * quarter([-1, 0, -1, 0])).reshape(s, PAIR)
    sin_hi = (sin * quarter([0, 1, 0, 1])).reshape(s, PAIR)
    return cos_t, sin_lo, sin_hi


def _proj_kernel(x_ref, g_ref, w_ref, cos_ref, sinlo_ref, sinhi_ref, *out_refs,
                 n_rope, scale0, transposed, with_kmean):
    n_chunks = len(transposed)
    h = _rms(x_ref[...], g_ref[...]).astype(BF16)
    tm = h.shape[0]
    reps = PROJ_CW // PAIR
    cos = jnp.concatenate([cos_ref[...]] * reps, axis=1)
    sin_lo = jnp.concatenate([sinlo_ref[...]] * reps, axis=1)
    sin_hi = jnp.concatenate([sinhi_ref[...]] * reps, axis=1)
    for c in range(n_chunks):
        for lo in range(0, D_MODEL, PROJ_CW):
            cols = slice(lo, lo + PROJ_CW)
            y = jnp.dot(h, w_ref[:, c * D_MODEL + lo:c * D_MODEL + lo + PROJ_CW],
                        preferred_element_type=F32)
            if c < n_rope:
                half = HEAD_DIM // 2
                y = (y * cos + pltpu.roll(y, PROJ_CW - half, 1) * sin_lo
                     + pltpu.roll(y, half, 1) * sin_hi)
                if c == 0 and with_kmean:
                    km = jnp.mean(y.reshape(tm // MOBA_BLOCK, MOBA_BLOCK, PROJ_CW), axis=1)
                    out_refs[n_chunks][0, :, cols] = km
            if c == 0 and scale0 != 1.0:
                y = y * scale0
            if transposed[c]:
                pairs = slice(lo // PAIR, (lo + PROJ_CW) // PAIR)
                out_refs[c][pairs] = y.T.reshape(reps, 1, PAIR, tm).astype(BF16)
            else:
                out_refs[c][:, cols] = y.astype(BF16)


def _project(x, g, w, tables, *, n_rope, scale0, transposed, with_kmean):
    s = x.shape[0]
    tm = ATT_TK
    row = lambda i: (i, 0)
    out_shape, out_specs = [], []
    for tr in transposed:
        if tr:
            out_shape.append(jax.ShapeDtypeStruct((N_PAIRS, s // tm, PAIR, tm), BF16))
            out_specs.append(pl.BlockSpec((N_PAIRS, 1, PAIR, tm), lambda i: (0, i, 0, 0)))
        else:
            out_shape.append(jax.ShapeDtypeStruct((s, D_MODEL), BF16))
            out_specs.append(pl.BlockSpec((tm, D_MODEL), row))
    if with_kmean:
        nb = tm // MOBA_BLOCK
        out_shape.append(jax.ShapeDtypeStruct((s // tm, nb, D_MODEL), F32))
        out_specs.append(pl.BlockSpec((1, nb, D_MODEL), lambda i: (i, 0, 0)))
    return pl.pallas_call(
        functools.partial(_proj_kernel, n_rope=n_rope, scale0=scale0, transposed=transposed,
                          with_kmean=with_kmean),
        grid=(s // tm,),
        in_specs=[
            pl.BlockSpec((tm, D_MODEL), row),
            pl.BlockSpec((1, D_MODEL), lambda i: (0, 0)),
            pl.BlockSpec(w.shape, lambda i: (0, 0)),
            pl.BlockSpec((tm, PAIR), row),
            pl.BlockSpec((tm, PAIR), row),
            pl.BlockSpec((tm, PAIR), row),
        ],
        out_specs=out_specs,
        out_shape=out_shape,
        compiler_params=pltpu.CompilerParams(
            dimension_semantics=("parallel",), vmem_limit_bytes=VMEM_LIMIT),
        name="project",
    )(x, g.reshape(1, D_MODEL), w, *tables)


def _stacked_queries_t(qt_ref):
    qt = jnp.concatenate([qt_ref[r] for r in range(qt_ref.shape[0])], axis=1)
    feat = lax.broadcasted_iota(jnp.int32, qt.shape, 0)
    zero = jnp.zeros_like(qt)
    return jnp.concatenate([jnp.where(feat < HEAD_DIM, qt, zero),
                            jnp.where(feat >= HEAD_DIM, qt, zero)], axis=1)


def _chunk_visibility(key_offset, cols):
    if key_offset is None:
        return "all"
    q_first, q_last = cols.start % ATT_TQ, (cols.stop - 1) % ATT_TQ
    if key_offset > q_last:
        return "none"
    if key_offset + ATT_TK - 1 <= q_first:
        return "all"
    return "part"


def _chunk_index(cols):
    return cols.start // (cols.stop - cols.start)


def _block_bias_rows(bias, cols):
    bias_ref, blk0 = bias
    return bias_ref[pl.ds(blk0, 1), cols], bias_ref[pl.ds(blk0 + 1, 1), cols]


def _score_chunk(k_ref, row0, q_ref, s_ref, mx_ref, key_offset, cols, bias=None):
    vis = _chunk_visibility(key_offset, cols)
    if vis == "none":
        return
    st = jnp.dot(k_ref[pl.ds(row0, ATT_TK), :], q_ref[:, cols], preferred_element_type=F32)
    if vis == "part":
        key = lax.broadcasted_iota(jnp.int32, st.shape, 0) + key_offset
        qpos = lax.broadcasted_iota(jnp.int32, st.shape, 1) + cols.start % ATT_TQ
        st = jnp.where(key <= qpos, st, -jnp.inf)
    s_ref[_chunk_index(cols)] = st
    if bias is None:
        mx_ref[:, cols] = jnp.max(st, axis=0, keepdims=True)
    else:
        b_top, b_bot = _block_bias_rows(bias, cols)
        mx_ref[:, cols] = jnp.maximum(
            jnp.max(st[0:MOBA_BLOCK], axis=0, keepdims=True) + b_top,
            jnp.max(st[MOBA_BLOCK:], axis=0, keepdims=True) + b_bot)


def _consume_chunk(s_ref, mx_ref, vt, m_ref, acc_ref, key_offset, cols, split_heads, bias=None):
    if _chunk_visibility(key_offset, cols) == "none":
        return
    m_prev = m_ref[:, cols]
    m_new = jnp.maximum(m_prev, mx_ref[:, cols])
    alpha = jnp.exp2(m_prev - m_new)
    if bias is None:
        p = jnp.exp2(s_ref[_chunk_index(cols)] - m_new).astype(BF16)
    else:
        b_top, b_bot = _block_bias_rows(bias, cols)
        c = _chunk_index(cols)
        p = jnp.concatenate([jnp.exp2(s_ref[c, 0:MOBA_BLOCK] - (m_new - b_top)),
                             jnp.exp2(s_ref[c, MOBA_BLOCK:] - (m_new - b_bot))],
                            axis=0).astype(BF16)
    if split_heads:
        vt = vt[0:HEAD_DIM] if cols.start < ATT_TQ else vt[HEAD_DIM:]
    vt_sum = jnp.concatenate([vt, jnp.ones((SUM_ROWS, vt.shape[1]), BF16)], axis=0)
    acc_ref[:, cols] = alpha * acc_ref[:, cols] + jnp.dot(vt_sum, p, preferred_element_type=F32)
    m_ref[:, cols] = m_new


def _run_kv_blocks(i, score_chunk, consume_chunk, chunk_width):
    def fused(sc, cs):
        for c0 in range(0, 2 * ATT_TQ, chunk_width):
            cols = slice(c0, c0 + chunk_width)
            if sc is not None:
                score_chunk(*sc, cols)
            if cs is not None:
                consume_chunk(*cs, cols)

    @pl.when(i == 0)
    def _():
        fused((0, 0, 0), None)
        fused((1, 1, ATT_TK), (0, 0, 0))
        fused(None, (1, 1, ATT_TK))

    @pl.when(i > 0)
    def _():
        fused((0, 0, None), None)

        def pair(b):
            fused((b + 1, 1, None), (b, 0, None))
            fused((b + 2, 0, None), (b + 1, 1, None))

        def two_pairs(q, carry):
            pair(4 * q)
            pair(4 * q + 2)
            return carry

        n_double = lax.shift_right_logical(i - 1, 1)
        lax.fori_loop(0, n_double, two_pairs, 0)

        @pl.when(lax.rem(i - 1, 2) == 1)
        def _():
            pair(4 * n_double)

        b = 2 * i - 2
        fused((b + 1, 1, None), (b, 0, None))
        fused((b + 2, 0, 0), (b + 1, 1, None))
        fused((b + 3, 1, ATT_TK), (b + 2, 0, 0))
        fused(None, (b + 3, 1, ATT_TK))


def _init_stats(m_ref, acc_ref, m_init):
    m_ref[...] = jnp.full(m_ref.shape, m_init, F32)
    acc_ref[...] = jnp.zeros(acc_ref.shape, F32)


def _normalized(acc_ref, lo, hi):
    rows = acc_ref.shape[0] - SUM_ROWS
    return acc_ref[0:rows, lo:hi] / acc_ref[rows:rows + 1, lo:hi]


def _pipeline_scratch(value_rows, chunk_width):
    cols = 2 * ATT_TQ
    parked = pltpu.VMEM((cols // chunk_width, ATT_TK, chunk_width), F32)
    return [parked, parked,
            pltpu.VMEM((1, cols), F32), pltpu.VMEM((1, cols), F32),
            pltpu.VMEM((1, cols), F32), pltpu.VMEM((value_rows + SUM_ROWS, cols), F32)]


def _attn_specs(s):
    qt_spec = pl.BlockSpec((None, ATT_TQ // ATT_TK, PAIR, ATT_TK), lambda h, i: (h, i, 0, 0))
    k_spec = pl.BlockSpec((s, PAIR), lambda h, i: (0, h))
    vt_spec = pl.BlockSpec((None, s // ATT_TK, PAIR, ATT_TK), lambda h, i: (h, 0, 0, 0))
    o_spec = pl.BlockSpec((ATT_TQ, PAIR), lambda h, i: (i, h))
    return qt_spec, k_spec, vt_spec, o_spec


def _diff_attn_kernel(qt_ref, k_ref, vt_ref, lam_ref, subln_ref, o_ref,
                      qs_ref, s0_ref, s1_ref, mx0_ref, mx1_ref, m_ref, acc_ref):
    tq, tk = ATT_TQ, ATT_TK
    i = pl.program_id(1)
    qs_ref[...] = _stacked_queries_t(qt_ref)
    _init_stats(m_ref, acc_ref, -jnp.inf)

    def score_chunk(j, slot, key_offset, cols):
        _score_chunk(k_ref, pl.multiple_of(j * tk, tk), qs_ref, (s0_ref, s1_ref)[slot],
                     (mx0_ref, mx1_ref)[slot], key_offset, cols)

    def consume_chunk(j, slot, key_offset, cols):
        _consume_chunk((s0_ref, s1_ref)[slot], (mx0_ref, mx1_ref)[slot], vt_ref[j], m_ref, acc_ref,
                       key_offset, cols, split_heads=False)

    _run_kv_blocks(i, score_chunk, consume_chunk, DIFF_CW)

    lv = lam_ref[...]
    lam = (jnp.exp(jnp.sum(lv[0:1] * lv[1:2], axis=-1, keepdims=True))
           - jnp.exp(jnp.sum(lv[2:3] * lv[3:4], axis=-1, keepdims=True)) + DIFF_LAMBDA_INIT)
    ot = _normalized(acc_ref, 0, tq) - lam * _normalized(acc_ref, tq, 2 * tq)
    o_ref[...] = (_rms(ot.T, subln_ref[...]) * (1.0 - DIFF_LAMBDA_INIT)).astype(BF16)


def _diff_attention(qt, k, vt, lam_vecs, subln):
    s = k.shape[0]
    qt_spec, k_spec, vt_spec, o_spec = _attn_specs(s)
    return pl.pallas_call(
        _diff_attn_kernel,
        grid=(N_PAIRS, s // ATT_TQ),
        in_specs=[qt_spec, k_spec, vt_spec,
                  pl.BlockSpec((4, HEAD_DIM), lambda h, i: (0, 0)),
                  pl.BlockSpec((1, PAIR), lambda h, i: (0, 0))],
        out_specs=o_spec,
        out_shape=jax.ShapeDtypeStruct((s, D_MODEL), BF16),
        scratch_shapes=[pltpu.VMEM((PAIR, 2 * ATT_TQ), BF16)] + _pipeline_scratch(PAIR, DIFF_CW),
        compiler_params=pltpu.CompilerParams(
            dimension_semantics=("parallel", "arbitrary"), vmem_limit_bytes=VMEM_LIMIT),
        name="diff_attn",
    )(qt, k, vt, lam_vecs, subln.reshape(1, PAIR))


def _moba_bias_t(gate, qblk):
    n = lax.broadcasted_iota(jnp.int32, gate.shape, 0)
    nf = n.astype(F32)
    past = n < qblk
    g = jnp.where(past, gate, -jnp.inf)
    picked = jnp.zeros(gate.shape, F32)
    for _ in range(MOBA_TOPK):
        mx = jnp.max(g, axis=0, keepdims=True)
        first = jnp.min(jnp.where(g == mx, nf, float(PAIR)), axis=0, keepdims=True)
        hit = nf == first
        picked = jnp.where(hit, 1.0, picked)
        g = jnp.where(hit, -jnp.inf, g)
    sel_past = jnp.where(past, jnp.where(picked > 0.0, 0.0, MASK_NEG), MASK_NEG)
    return jnp.where(n == qblk, 0.0, sel_past)


def _moba_attn_kernel(qt_ref, k_ref, vt_ref, kmean_ref, o_ref,
                      qs_ref, bias_ref, s0_ref, s1_ref, mx0_ref, mx1_ref, m_ref, acc_ref):
    tq, tk = ATT_TQ, ATT_TK
    i = pl.program_id(1)

    qs = _stacked_queries_t(qt_ref)
    gate, rest = None, kmean_ref[...]
    for _ in range(3):
        piece = rest.astype(BF16)
        part = jnp.dot(piece, qs, preferred_element_type=F32)
        gate = part if gate is None else gate + part
        rest = rest - piece.astype(F32)
    col = lax.broadcasted_iota(jnp.int32, (1, 2 * tq), 1)
    qblk = (i * tq + jnp.where(col >= tq, col - tq, col)) // MOBA_BLOCK
    qs_ref[...] = qs
    bias_ref[...] = _moba_bias_t(gate, qblk)
    _init_stats(m_ref, acc_ref, M_FLOOR)

    def block_bias(j):
        return bias_ref, j * (tk // MOBA_BLOCK)

    def score_chunk(j, slot, key_offset, cols):
        _score_chunk(k_ref, pl.multiple_of(j * tk, tk), qs_ref, (s0_ref, s1_ref)[slot],
                     (mx0_ref, mx1_ref)[slot], key_offset, cols, bias=block_bias(j))

    def consume_chunk(j, slot, key_offset, cols):
        _consume_chunk((s0_ref, s1_ref)[slot], (mx0_ref, mx1_ref)[slot], vt_ref[j], m_ref, acc_ref,
                       key_offset, cols, split_heads=True, bias=block_bias(j))

    _run_kv_blocks(i, score_chunk, consume_chunk, MOBA_CW)

    ot = jnp.concatenate([_normalized(acc_ref, 0, tq), _normalized(acc_ref, tq, 2 * tq)], axis=0)
    o_ref[...] = ot.T.astype(BF16)


def _moba_attention(qt, k, vt, kmean):
    s = k.shape[0]
    qt_spec, k_spec, vt_spec, o_spec = _attn_specs(s)
    return pl.pallas_call(
        _moba_attn_kernel,
        grid=(N_PAIRS, s // ATT_TQ),
        in_specs=[qt_spec, k_spec, vt_spec,
                  pl.BlockSpec((s // MOBA_BLOCK, PAIR), lambda h, i: (0, h))],
        out_specs=o_spec,
        out_shape=jax.ShapeDtypeStruct((s, D_MODEL), BF16),
        scratch_shapes=[pltpu.VMEM((PAIR, 2 * ATT_TQ), BF16),
                        pltpu.VMEM((s // MOBA_BLOCK, 2 * ATT_TQ), F32)] + _pipeline_scratch(HEAD_DIM, MOBA_CW),
        compiler_params=pltpu.CompilerParams(
            dimension_semantics=("parallel", "arbitrary"), vmem_limit_bytes=VMEM_LIMIT),
        name="moba_attn",
    )(qt, k, vt, kmean)


def kernel(x, ffn_norm, ffn_w_in, ffn_w_out, mix_norm, diff_w_qkv, diff_lambda, diff_subln,
           diff_w_o, kv_norm, kv_w, moba_w_q, moba_w_o):
    b, s, d = x.shape
    assert b == 1 and d == D_MODEL and s % ATT_TQ == 0
    assert ATT_TK == 2 * MOBA_BLOCK and (s // MOBA_BLOCK) % 8 == 0
    assert ffn_w_in.shape[0] == 2 and diff_w_qkv.shape[0] == 1 and moba_w_q.shape[0] == 1
    tables = _rope_tables(s)
    w_in, w_out = ffn_w_in, ffn_w_out

    def ffn(xx, layer, slot, mixer_out=None):
        return _ffn_half(xx, ffn_norm, w_in, w_out, layer, slot, mixer_out)

    xs = x.reshape(s, d)

    xs = ffn(xs, 0, 0)
    qt, k, vt = _project(xs, mix_norm[0, 0], diff_w_qkv[0].astype(BF16), tables, n_rope=2,
                         scale0=Q_SCALE, transposed=(True, False, True), with_kmean=False)
    o = _diff_attention(qt, k, vt, diff_lambda[0], diff_subln[0])
    xs = ffn(xs, 0, 1, mixer_out=(o, diff_w_o[0].astype(BF16), mix_norm[0, 1]))

    k, vt, kmean = _project(xs, kv_norm, kv_w.astype(BF16), tables, n_rope=1, scale0=1.0,
                            transposed=(False, True), with_kmean=True)
    kmean = kmean.reshape(s // MOBA_BLOCK, d)
    xs = ffn(xs, 1, 0)
    (qt,) = _project(xs, mix_norm[1, 0], moba_w_q[0].astype(BF16), tables, n_rope=1,
                     scale0=Q_SCALE, transposed=(True,), with_kmean=False)
    o = _moba_attention(qt, k, vt, kmean)
    xs = ffn(xs, 1, 1, mixer_out=(o, moba_w_o[0].astype(BF16), mix_norm[1, 1]))
    return xs.reshape(b, s, d)
```

```python
import functools
import math

import jax
import jax.numpy as jnp
from jax import lax
from jax.experimental import pallas as pl
from jax.experimental.pallas import tpu as pltpu

F32 = jnp.float32
BF16 = jnp.bfloat16

D_MODEL = 1024
HEAD_DIM = 64
PAIR = 2 * HEAD_DIM
N_PAIRS = D_MODEL // PAIR
D_FF = 2816
ROPE_THETA = 10000.0
MOBA_BLOCK = 256
MOBA_TOPK = 3
FFN_RESIDUAL_SCALE = 0.5
NORM_EPS = 1e-6
DIFF_LAMBDA_INIT = 0.2
MASK_NEG = -1e30
M_FLOOR = -1e29
Q_SCALE = HEAD_DIM ** -0.5 * math.log2(math.e)

V7X_VMEM_BYTES = 64 * 1024 * 1024
V7X_MXU_WIDTH = 256
V7X_BF16_SUBLANES = 16

FFN_TM = 512
FFN_CW = V7X_MXU_WIDTH
PROJ_CW = V7X_MXU_WIDTH
ATT_TK = 512
ATT_TQ = 2 * ATT_TK
DIFF_CW = 2 * V7X_MXU_WIDTH
MOBA_CW = V7X_MXU_WIDTH
SUM_ROWS = V7X_BF16_SUBLANES
VMEM_LIMIT = V7X_VMEM_BYTES * 7 // 8


def _rms(x, g):
    return x * lax.rsqrt(jnp.mean(x * x, axis=-1, keepdims=True) + NORM_EPS) * g


def _ffn_kernel(*refs, with_mixer_out):
    if with_mixer_out:
        attn_ref, wmix_ref, gmix_ref, *refs = refs
    x_ref, gpre_ref, gpost_ref, win_ref, wout_ref, o_ref, act_ref = refs
    x = x_ref[...]
    if with_mixer_out:
        mixed = jnp.dot(attn_ref[...], wmix_ref[...], preferred_element_type=F32)
        x = x + _rms(mixed, gmix_ref[...])
    h = _rms(x, gpre_ref[...]).astype(BF16)
    for c in range(D_FF // FFN_CW):
        lo = c * FFN_CW
        w_gate = win_ref[:, lo:lo + FFN_CW].astype(BF16)
        w_up = win_ref[:, D_FF + lo:D_FF + lo + FFN_CW].astype(BF16)
        gate = jnp.dot(h, w_gate, preferred_element_type=F32)
        up = jnp.dot(h, w_up, preferred_element_type=F32)
        act_ref[:, lo:lo + FFN_CW] = (gate * (1.0 / (1.0 + jnp.exp(-gate))) * up).astype(BF16)
    act = act_ref[...]
    y = jnp.concatenate(
        [jnp.dot(act, wout_ref[:, n:n + FFN_CW].astype(BF16), preferred_element_type=F32)
         for n in range(0, D_MODEL, FFN_CW)], axis=1)
    o_ref[...] = x + FFN_RESIDUAL_SCALE * _rms(y, gpost_ref[...])


def _ffn_half(x, norms, w_in, w_out, layer, slot, mixer_out=None):
    s = x.shape[0]
    row = lambda i: (i, 0)
    const = lambda i: (0, 0)
    weights = lambda i: (layer, slot, 0, 0)
    resident = pl.Buffered(1)
    g_pre, g_post = norms[layer, slot, 0], norms[layer, slot, 1]
    operands = [x, g_pre.reshape(1, D_MODEL), g_post.reshape(1, D_MODEL), w_in, w_out]
    in_specs = [
        pl.BlockSpec((FFN_TM, D_MODEL), row),
        pl.BlockSpec((1, D_MODEL), const),
        pl.BlockSpec((1, D_MODEL), const),
        pl.BlockSpec((None, None, D_MODEL, 2 * D_FF), weights, pipeline_mode=resident),
        pl.BlockSpec((None, None, D_FF, D_MODEL), weights, pipeline_mode=resident),
    ]
    if mixer_out is not None:
        attn, w_o, g_mix = mixer_out
        operands = [attn, w_o, g_mix.reshape(1, D_MODEL)] + operands
        in_specs = [pl.BlockSpec((FFN_TM, D_MODEL), row),
                    pl.BlockSpec((D_MODEL, D_MODEL), const, pipeline_mode=resident),
                    pl.BlockSpec((1, D_MODEL), const)] + in_specs
    return pl.pallas_call(
        functools.partial(_ffn_kernel, with_mixer_out=mixer_out is not None),
        grid=(s // FFN_TM,),
        in_specs=in_specs,
        out_specs=pl.BlockSpec((FFN_TM, D_MODEL), row),
        out_shape=jax.ShapeDtypeStruct((s, D_MODEL), F32),
        scratch_shapes=[pltpu.VMEM((FFN_TM, D_FF), BF16)],
        compiler_params=pltpu.CompilerParams(
            dimension_semantics=("parallel",), vmem_limit_bytes=VMEM_LIMIT),
        name="ffn_half",
    )(*operands)


def _rope_tables(s):
    pos = jnp.arange(s, dtype=F32)
    inv_freq = 1.0 / (ROPE_THETA ** (jnp.arange(0, HEAD_DIM, 2, dtype=F32) / HEAD_DIM))
    ang = pos[:, None] * inv_freq[None, :]
    cos, sin = jnp.cos(ang)[:, None, :], jnp.sin(ang)[:, None, :]
    quarter = lambda coef: jnp.asarray(coef, F32).reshape(1, 4, 1)
    cos_t = (cos * quarter([1, 1, 1, 1])).reshape(s, PAIR)
    sin_lo = (sin * quarter([-1, 0, -1, 0])).reshape(s, PAIR)
    sin_hi = (sin * quarter([0, 1, 0, 1])).reshape(s, PAIR)
    return cos_t, sin_lo, sin_hi


def _proj_kernel(x_ref, g_ref, w_ref, cos_ref, sinlo_ref, sinhi_ref, *out_refs,
                 n_rope, scale0, transposed, with_kmean):
    n_chunks = len(transposed)
    h = _rms(x_ref[...], g_ref[...]).astype(BF16)
    tm = h.shape[0]
    reps = PROJ_CW // PAIR
    cos = jnp.concatenate([cos_ref[...]] * reps, axis=1)
    sin_lo = jnp.concatenate([sinlo_ref[...]] * reps, axis=1)
    sin_hi = jnp.concatenate([sinhi_ref[...]] * reps, axis=1)
    for c in range(n_chunks):
        for lo in range(0, D_MODEL, PROJ_CW):
            cols = slice(lo, lo + PROJ_CW)
            y = jnp.dot(h, w_ref[:, c * D_MODEL + lo:c * D_MODEL + lo + PROJ_CW],
                        preferred_element_type=F32)
            if c < n_rope:
                half = HEAD_DIM // 2
                y = (y * cos + pltpu.roll(y, PROJ_CW - half, 1) * sin_lo
                     + pltpu.roll(y, half, 1) * sin_hi)
                if c == 0 and with_kmean:
                    km = jnp.mean(y.reshape(tm // MOBA_BLOCK, MOBA_BLOCK, PROJ_CW), axis=1)
                    out_refs[n_chunks][0, :, cols] = km
            if c == 0 and scale0 != 1.0:
                y = y * scale0
            if transposed[c]:
                pairs = slice(lo // PAIR, (lo + PROJ_CW) // PAIR)
                out_refs[c][pairs] = y.T.reshape(reps, 1, PAIR, tm).astype(BF16)
            else:
                out_refs[c][:, cols] = y.astype(BF16)


def _project(x, g, w, tables, *, n_rope, scale0, transposed, with_kmean):
    s = x.shape[0]
    tm = ATT_TK
    row = lambda i: (i, 0)
    out_shape, out_specs = [], []
    for tr in transposed:
        if tr:
            out_shape.append(jax.ShapeDtypeStruct((N_PAIRS, s // tm, PAIR, tm), BF16))
            out_specs.append(pl.BlockSpec((N_PAIRS, 1, PAIR, tm), lambda i: (0, i, 0, 0)))
        else:
            out_shape.append(jax.ShapeDtypeStruct((s, D_MODEL), BF16))
            out_specs.append(pl.BlockSpec((tm, D_MODEL), row))
    if with_kmean:
        nb = tm // MOBA_BLOCK
        out_shape.append(jax.ShapeDtypeStruct((s // tm, nb, D_MODEL), F32))
        out_specs.append(pl.BlockSpec((1, nb, D_MODEL), lambda i: (i, 0, 0)))
    return pl.pallas_call(
        functools.partial(_proj_kernel, n_rope=n_rope, scale0=scale0, transposed=transposed,
                          with_kmean=with_kmean),
        grid=(s // tm,),
        in_specs=[
            pl.BlockSpec((tm, D_MODEL), row),
            pl.BlockSpec((1, D_MODEL), lambda i: (0, 0)),
            pl.BlockSpec(w.shape, lambda i: (0, 0)),
            pl.BlockSpec((tm, PAIR), row),
            pl.BlockSpec((tm, PAIR), row),
            pl.BlockSpec((tm, PAIR), row),
        ],
        out_specs=out_specs,
        out_shape=out_shape,
        compiler_params=pltpu.CompilerParams(
            dimension_semantics=("parallel",), vmem_limit_bytes=VMEM_LIMIT),
        name="project",
    )(x, g.reshape(1, D_MODEL), w, *tables)


def _stacked_queries_t(qt_ref):
    qt = jnp.concatenate([qt_ref[r] for r in range(qt_ref.shape[0])], axis=1)
    feat = lax.broadcasted_iota(jnp.int32, qt.shape, 0)
    zero = jnp.zeros_like(qt)
    return jnp.concatenate([jnp.where(feat < HEAD_DIM, qt, zero),
                            jnp.where(feat >= HEAD_DIM, qt, zero)], axis=1)


def _chunk_visibility(key_offset, cols):
    if key_offset is None:
        return "all"
    q_first, q_last = cols.start % ATT_TQ, (cols.stop - 1) % ATT_TQ
    if key_offset > q_last:
        return "none"
    if key_offset + ATT_TK - 1 <= q_first:
        return "all"
    return "part"


def _block_bias_rows(bias, cols):
    bias_ref, blk0 = bias
    return bias_ref[pl.ds(blk0, 1), cols], bias_ref[pl.ds(blk0 + 1, 1), cols]


def _score_chunk(k_ref, row0, q_ref, s_ref, mx_ref, key_offset, cols, bias=None):
    vis = _chunk_visibility(key_offset, cols)
    if vis == "none":
        return
    st = jnp.dot(k_ref[pl.ds(row0, ATT_TK), :], q_ref[:, cols], preferred_element_type=F32)
    if vis == "part":
        key = lax.broadcasted_iota(jnp.int32, st.shape, 0) + key_offset
        qpos = lax.broadcasted_iota(jnp.int32, st.shape, 1) + cols.start % ATT_TQ
        st = jnp.where(key <= qpos, st, -jnp.inf)
    s_ref[:, cols] = st
    if bias is None:
        mx_ref[:, cols] = jnp.max(st, axis=0, keepdims=True)
    else:
        b_top, b_bot = _block_bias_rows(bias, cols)
        mx_ref[:, cols] = jnp.maximum(
            jnp.max(st[0:MOBA_BLOCK], axis=0, keepdims=True) + b_top,
            jnp.max(st[MOBA_BLOCK:], axis=0, keepdims=True) + b_bot)


def _consume_chunk(s_ref, mx_ref, vt, m_ref, acc_ref, key_offset, cols, split_heads, bias=None):
    if _chunk_visibility(key_offset, cols) == "none":
        return
    m_prev = m_ref[:, cols]
    m_new = jnp.maximum(m_prev, mx_ref[:, cols])
    alpha = jnp.exp2(m_prev - m_new)
    if bias is None:
        p = jnp.exp2(s_ref[:, cols] - m_new).astype(BF16)
    else:
        b_top, b_bot = _block_bias_rows(bias, cols)
        p = jnp.concatenate([jnp.exp2(s_ref[0:MOBA_BLOCK, cols] - (m_new - b_top)),
                             jnp.exp2(s_ref[MOBA_BLOCK:, cols] - (m_new - b_bot))],
                            axis=0).astype(BF16)
    if split_heads:
        vt = vt[0:HEAD_DIM] if cols.start < ATT_TQ else vt[HEAD_DIM:]
    vt_sum = jnp.concatenate([vt, jnp.ones((SUM_ROWS, vt.shape[1]), BF16)], axis=0)
    acc_ref[:, cols] = alpha * acc_ref[:, cols] + jnp.dot(vt_sum, p, preferred_element_type=F32)
    m_ref[:, cols] = m_new


def _run_kv_blocks(i, score_chunk, consume_chunk, chunk_width):
    def fused(sc, cs):
        for c0 in range(0, 2 * ATT_TQ, chunk_width):
            cols = slice(c0, c0 + chunk_width)
            if sc is not None:
                score_chunk(*sc, cols)
            if cs is not None:
                consume_chunk(*cs, cols)

    @pl.when(i == 0)
    def _():
        fused((0, 0, 0), None)
        fused((1, 1, ATT_TK), (0, 0, 0))
        fused(None, (1, 1, ATT_TK))

    @pl.when(i > 0)
    def _():
        fused((0, 0, None), None)

        def pair(b):
            fused((b + 1, 1, None), (b, 0, None))
            fused((b + 2, 0, None), (b + 1, 1, None))

        def four_pairs(q, carry):
            for t in range(4):
                pair(8 * q + 2 * t)
            return carry

        n_pairs = i - 1
        n_quad = lax.shift_right_logical(n_pairs, 2)
        lax.fori_loop(0, n_quad, four_pairs, 0)
        has_two = lax.bitwise_and(n_pairs, 2) != 0

        @pl.when(has_two)
        def _():
            pair(8 * n_quad)
            pair(8 * n_quad + 2)

        @pl.when(lax.bitwise_and(n_pairs, 1) != 0)
        def _():
            pair(8 * n_quad + jnp.where(has_two, 4, 0))

        b = 2 * i - 2
        fused((b + 1, 1, None), (b, 0, None))
        fused((b + 2, 0, 0), (b + 1, 1, None))
        fused((b + 3, 1, ATT_TK), (b + 2, 0, 0))
        fused(None, (b + 3, 1, ATT_TK))


def _init_stats(m_ref, acc_ref, m_init):
    m_ref[...] = jnp.full(m_ref.shape, m_init, F32)
    acc_ref[...] = jnp.zeros(acc_ref.shape, F32)


def _normalized(acc_ref, lo, hi):
    rows = acc_ref.shape[0] - SUM_ROWS
    return acc_ref[0:rows, lo:hi] / acc_ref[rows:rows + 1, lo:hi]


def _pipeline_scratch(value_rows):
    cols = 2 * ATT_TQ
    return [pltpu.VMEM((ATT_TK, cols), F32), pltpu.VMEM((ATT_TK, cols), F32),
            pltpu.VMEM((1, cols), F32), pltpu.VMEM((1, cols), F32),
            pltpu.VMEM((1, cols), F32), pltpu.VMEM((value_rows + SUM_ROWS, cols), F32)]


def _attn_specs(s):
    qt_spec = pl.BlockSpec((None, ATT_TQ // ATT_TK, PAIR, ATT_TK), lambda h, i: (h, i, 0, 0))
    k_spec = pl.BlockSpec((s, PAIR), lambda h, i: (0, h))
    vt_spec = pl.BlockSpec((None, s // ATT_TK, PAIR, ATT_TK), lambda h, i: (h, 0, 0, 0))
    o_spec = pl.BlockSpec((ATT_TQ, PAIR), lambda h, i: (i, h))
    return qt_spec, k_spec, vt_spec, o_spec


def _diff_attn_kernel(qt_ref, k_ref, vt_ref, lam_ref, subln_ref, o_ref,
                      qs_ref, s0_ref, s1_ref, mx0_ref, mx1_ref, m_ref, acc_ref):
    tq, tk = ATT_TQ, ATT_TK
    i = pl.program_id(1)
    qs_ref[...] = _stacked_queries_t(qt_ref)
    _init_stats(m_ref, acc_ref, -jnp.inf)

    def score_chunk(j, slot, key_offset, cols):
        _score_chunk(k_ref, pl.multiple_of(j * tk, tk), qs_ref, (s0_ref, s1_ref)[slot],
                     (mx0_ref, mx1_ref)[slot], key_offset, cols)

    def consume_chunk(j, slot, key_offset, cols):
        _consume_chunk((s0_ref, s1_ref)[slot], (mx0_ref, mx1_ref)[slot], vt_ref[j], m_ref, acc_ref,
                       key_offset, cols, split_heads=False)

    _run_kv_blocks(i, score_chunk, consume_chunk, DIFF_CW)

    lv = lam_ref[...]
    lam = (jnp.exp(jnp.sum(lv[0:1] * lv[1:2], axis=-1, keepdims=True))
           - jnp.exp(jnp.sum(lv[2:3] * lv[3:4], axis=-1, keepdims=True)) + DIFF_LAMBDA_INIT)
    ot = _normalized(acc_ref, 0, tq) - lam * _normalized(acc_ref, tq, 2 * tq)
    o_ref[...] = (_rms(ot.T, subln_ref[...]) * (1.0 - DIFF_LAMBDA_INIT)).astype(BF16)


def _diff_attention(qt, k, vt, lam_vecs, subln):
    s = k.shape[0]
    qt_spec, k_spec, vt_spec, o_spec = _attn_specs(s)
    return pl.pallas_call(
        _diff_attn_kernel,
        grid=(N_PAIRS, s // ATT_TQ),
        in_specs=[qt_spec, k_spec, vt_spec,
                  pl.BlockSpec((4, HEAD_DIM), lambda h, i: (0, 0)),
                  pl.BlockSpec((1, PAIR), lambda h, i: (0, 0))],
        out_specs=o_spec,
        out_shape=jax.ShapeDtypeStruct((s, D_MODEL), BF16),
        scratch_shapes=[pltpu.VMEM((PAIR, 2 * ATT_TQ), BF16)] + _pipeline_scratch(PAIR),
        compiler_params=pltpu.CompilerParams(
            dimension_semantics=("parallel", "arbitrary"), vmem_limit_bytes=VMEM_LIMIT),
        name="diff_attn",
    )(qt, k, vt, lam_vecs, subln.reshape(1, PAIR))


def _moba_bias_t(gate, qblk):
    n = lax.broadcasted_iota(jnp.int32, gate.shape, 0)
    nf = n.astype(F32)
    past = n < qblk
    g = jnp.where(past, gate, -jnp.inf)
    picked = jnp.zeros(gate.shape, F32)
    for _ in range(MOBA_TOPK):
        mx = jnp.max(g, axis=0, keepdims=True)
        first = jnp.min(jnp.where(g == mx, nf, float(PAIR)), axis=0, keepdims=True)
        hit = nf == first
        picked = jnp.where(hit, 1.0, picked)
        g = jnp.where(hit, -jnp.inf, g)
    sel_past = jnp.where(past, jnp.where(picked > 0.0, 0.0, MASK_NEG), MASK_NEG)
    return jnp.where(n == qblk, 0.0, sel_past)


def _moba_attn_kernel(qt_ref, k_ref, vt_ref, kmean_ref, o_ref,
                      qs_ref, bias_ref, s0_ref, s1_ref, mx0_ref, mx1_ref, m_ref, acc_ref):
    tq, tk = ATT_TQ, ATT_TK
    i = pl.program_id(1)

    qs = _stacked_queries_t(qt_ref)
    gate, rest = None, kmean_ref[...]
    for _ in range(3):
        piece = rest.astype(BF16)
        part = jnp.dot(piece, qs, preferred_element_type=F32)
        gate = part if gate is None else gate + part
        rest = rest - piece.astype(F32)
    col = lax.broadcasted_iota(jnp.int32, (1, 2 * tq), 1)
    qblk = (i * tq + jnp.where(col >= tq, col - tq, col)) // MOBA_BLOCK
    qs_ref[...] = qs
    bias_ref[...] = _moba_bias_t(gate, qblk)
    _init_stats(m_ref, acc_ref, M_FLOOR)

    def block_bias(j):
        return bias_ref, j * (tk // MOBA_BLOCK)

    def score_chunk(j, slot, key_offset, cols):
        _score_chunk(k_ref, pl.multiple_of(j * tk, tk), qs_ref, (s0_ref, s1_ref)[slot],
                     (mx0_ref, mx1_ref)[slot], key_offset, cols, bias=block_bias(j))

    def consume_chunk(j, slot, key_offset, cols):
        _consume_chunk((s0_ref, s1_ref)[slot], (mx0_ref, mx1_ref)[slot], vt_ref[j], m_ref, acc_ref,
                       key_offset, cols, split_heads=True, bias=block_bias(j))

    _run_kv_blocks(i, score_chunk, consume_chunk, MOBA_CW)

    ot = jnp.concatenate([_normalized(acc_ref, 0, tq), _normalized(acc_ref, tq, 2 * tq)], axis=0)
    o_ref[...] = ot.T.astype(BF16)


def _moba_attention(qt, k, vt, kmean):
    s = k.shape[0]
    qt_spec, k_spec, vt_spec, o_spec = _attn_specs(s)
    return pl.pallas_call(
        _moba_attn_kernel,
        grid=(N_PAIRS, s // ATT_TQ),
        in_specs=[qt_spec, k_spec, vt_spec,
                  pl.BlockSpec((s // MOBA_BLOCK, PAIR), lambda h, i: (0, h))],
        out_specs=o_spec,
        out_shape=jax.ShapeDtypeStruct((s, D_MODEL), BF16),
        scratch_shapes=[pltpu.VMEM((PAIR, 2 * ATT_TQ), BF16),
                        pltpu.VMEM((s // MOBA_BLOCK, 2 * ATT_TQ), F32)] + _pipeline_scratch(HEAD_DIM),
        compiler_params=pltpu.CompilerParams(
            dimension_semantics=("parallel", "arbitrary"), vmem_limit_bytes=VMEM_LIMIT),
        name="moba_attn",
    )(qt, k, vt, kmean)


def kernel(x, ffn_norm, ffn_w_in, ffn_w_out, mix_norm, diff_w_qkv, diff_lambda, diff_subln,
           diff_w_o, kv_norm, kv_w, moba_w_q, moba_w_o):
    b, s, d = x.shape
    assert b == 1 and d == D_MODEL and s % ATT_TQ == 0
    assert ATT_TK == 2 * MOBA_BLOCK and (s // MOBA_BLOCK) % 8 == 0
    assert ffn_w_in.shape[0] == 2 and diff_w_qkv.shape[0] == 1 and moba_w_q.shape[0] == 1
    tables = _rope_tables(s)
    w_in, w_out = ffn_w_in, ffn_w_out

    def ffn(xx, layer, slot, mixer_out=None):
        return _ffn_half(xx, ffn_norm, w_in, w_out, layer, slot, mixer_out)

    xs = x.reshape(s, d)

    xs = ffn(xs, 0, 0)
    qt, k, vt = _project(xs, mix_norm[0, 0], diff_w_qkv[0].astype(BF16), tables, n_rope=2,
                         scale0=Q_SCALE, transposed=(True, False, True), with_kmean=False)
    o = _diff_attention(qt, k, vt, diff_lambda[0], diff_subln[0])
    xs = ffn(xs, 0, 1, mixer_out=(o, diff_w_o[0].astype(BF16), mix_norm[0, 1]))

    k, vt, kmean = _project(xs, kv_norm, kv_w.astype(BF16), tables, n_rope=1, scale0=1.0,
                            transposed=(False, True), with_kmean=True)
    kmean = kmean.reshape(s // MOBA_BLOCK, d)
    xs = ffn(xs, 1, 0)
    (qt,) = _project(xs, mix_norm[1, 0], moba_w_q[0].astype(BF16), tables, n_rope=1,
                     scale0=Q_SCALE, transposed=(True,), with_kmean=False)
    o = _moba_attention(qt, k, vt, kmean)
    xs = ffn(xs, 1, 1, mixer_out=(o, moba_w_o[0].astype(BF16), mix_norm[1, 1]))
    return xs.reshape(b, s, d)
```
